```python
import math
import jax
import jax.numpy as jnp
from jax import lax
import numpy as np

D_MODEL = 1024
BATCH = 8
SEQ = 2048
DEPTH = 4

GRID_W = 64
CTX_LEN = 256
N_MIXERS = 4
D_FF = 4 * D_MODEL
ALPHA = (2 * DEPTH) ** 0.25
BETA = (8 * DEPTH) ** -0.25
NORM_EPS = 1e-6
ROPE_BASE = 10000.0
QUERY_BLOCK = 128
CHUNK = 64

MLA_HEADS = D_MODEL // 64
MLA_NOPE = 64
MLA_ROPE = 32
MLA_V = 64
MLA_KV_LORA = 4 * MLA_V
MLA_Q_LORA = 3 * MLA_KV_LORA

DIFF_HEAD_DIM = 64
DIFF_HEADS = D_MODEL // (2 * DIFF_HEAD_DIM)
DIFF_WIDTH = DIFF_HEADS * 2 * DIFF_HEAD_DIM

GLA_HEADS = 4
GLA_KEY_DIM = D_MODEL // 2
GLA_VALUE_DIM = D_MODEL
GLA_DK = GLA_KEY_DIM // GLA_HEADS
GLA_DV = GLA_VALUE_DIM // GLA_HEADS
GLA_GATE_RANK = 16
GLA_GATE_NORM = 16.0
GLA_IN_WIDTH = 2 * GLA_KEY_DIM + 2 * GLA_VALUE_DIM + 2 * GLA_GATE_RANK

GDN_HEAD_DIM = 128
GDN_KEY_HEADS = D_MODEL // GDN_HEAD_DIM
GDN_VALUE_HEADS = 2 * GDN_KEY_HEADS
GDN_KEY_DIM = GDN_KEY_HEADS * GDN_HEAD_DIM
GDN_VALUE_DIM = GDN_VALUE_HEADS * GDN_HEAD_DIM
GDN_CONV = 5
GDN_CONV_CH = 2 * GDN_KEY_DIM + GDN_VALUE_DIM
GDN_IN_WIDTH = GDN_CONV_CH + GDN_VALUE_DIM + 4 * GDN_VALUE_HEADS

N_A = (DEPTH + 3) // 4
N_B = (DEPTH + 2) // 4
N_C = (DEPTH + 1) // 4
N_D = DEPTH // 4

kernel_name = "hybrid_interleaved_diffusion_trunk"


def _cuts(*sizes):
    out, acc = [], 0
    for s in sizes:
        acc += s
        out.append(acc)
    return out


def _layer_norm(x, g, b):
    xf = x.astype(jnp.float32)
    mu = jnp.mean(xf, -1, keepdims=True)
    var = jnp.mean(jnp.square(xf - mu), -1, keepdims=True)
    return ((xf - mu) * lax.rsqrt(var + NORM_EPS) * g + b).astype(x.dtype)


def _rms_norm(x, g):
    xf = x.astype(jnp.float32)
    y = xf * lax.rsqrt(jnp.mean(jnp.square(xf), -1, keepdims=True) + NORM_EPS)
    return (y * g).astype(x.dtype)


def _l2_norm(x):
    xf = x.astype(jnp.float32)
    return (xf * lax.rsqrt(jnp.sum(jnp.square(xf), -1, keepdims=True) + NORM_EPS)).astype(x.dtype)


def _heads(t, n_heads):
    b, n, _ = t.shape
    return t.reshape(b, n, n_heads, -1).transpose(0, 2, 1, 3)


def _merge_heads(t):
    b, h, n, d = t.shape
    return t.transpose(0, 2, 1, 3).reshape(b, n, h * d)


def _axial_rope_tables(n, dim):
    rows = n // GRID_W
    row = jnp.repeat(jnp.arange(rows, dtype=jnp.float32), GRID_W)
    col = jnp.tile(jnp.arange(GRID_W, dtype=jnp.float32), rows)
    n_freq = dim // 4
    inv_freq = ROPE_BASE ** (-jnp.arange(n_freq, dtype=jnp.float32) / n_freq)
    ang = jnp.concatenate([row[:, None] * inv_freq, col[:, None] * inv_freq], -1)
    return jnp.cos(ang), jnp.sin(ang)


def _apply_rope(x, cos, sin):
    xf = x.astype(jnp.float32).reshape(x.shape[:-1] + (-1, 2))
    x1, x2 = xf[..., 0], xf[..., 1]
    out = jnp.stack([x1 * cos - x2 * sin, x1 * sin + x2 * cos], -1)
    return out.reshape(x.shape).astype(x.dtype)


def _softmax_probs(q, k, scale):
    s = jnp.einsum("bhqd,bhkd->bhqk", q, k).astype(jnp.float32) * scale
    return jax.nn.softmax(s, axis=-1)


def _softmax_attend(q, k, v, scale):
    p = _softmax_probs(q, k, scale)
    return jnp.einsum("bhqk,bhkd->bhqd", p.astype(v.dtype), v)


def _sweep_query_blocks(fn, qs):
    def split(a):
        b, h, n, d = a.shape
        return a.reshape(b, h, n // QUERY_BLOCK, QUERY_BLOCK, d).transpose(2, 0, 1, 3, 4)
    out = lax.map(fn, tuple(split(a) for a in qs))
    nb, b, h, qb, dv = out.shape
    return out.transpose(1, 2, 0, 3, 4).reshape(b, h, nb * qb, dv)


def _centred_dwconv(t, w):
    pad = w.shape[0] // 2
    return lax.conv_general_dilated(t, w[:, None, :], window_strides=(1,), padding=[(pad, pad)],
                                    dimension_numbers=("NWC", "WIO", "NWC"),
                                    feature_group_count=t.shape[-1])


def _gla_chunked(q, k, v, log_a, s0):
    f32 = jnp.float32
    b, h, n, _ = q.shape
    dv = v.shape[-1]
    nc = n // CHUNK
    q, k, v, log_a = (a.astype(f32).reshape(b, h, nc, CHUNK, a.shape[-1]) for a in (q, k, v, log_a))
    cum = jnp.cumsum(log_a, axis=3)
    last = cum[:, :, :, -1:, :]
    q_dec = q * jnp.exp(cum)
    k_inv = k * jnp.exp(-cum)
    k_end = k * jnp.exp(last - cum)
    idx = jnp.arange(CHUNK)
    a_intra = jnp.where(idx[:, None] >= idx[None, :],
                        jnp.einsum("bhncd,bhnsd->bhncs", q_dec, k_inv), 0.0)
    o_intra = jnp.einsum("bhncs,bhnsv->bhncv", a_intra, v)

    def step(s, inp):
        q_c, k_c, v_c, d_c = inp
        o = jnp.einsum("bhcd,bhdv->bhcv", q_c, s)
        s = s * d_c[..., None] + jnp.einsum("bhcd,bhcv->bhdv", k_c, v_c)
        return s, o

    xs = tuple(jnp.moveaxis(a, 2, 0) for a in (q_dec, k_end, v, jnp.exp(last[:, :, :, 0, :])))
    s_fin, o_inter = lax.scan(step, s0.astype(f32), xs)
    o = o_intra + jnp.moveaxis(o_inter, 0, 2)
    return o.reshape(b, h, n, dv), s_fin


def _gated_delta_chunked(q, k, v, g, beta, s0):
    f32 = jnp.float32
    b, h, n, _ = q.shape
    dv = v.shape[-1]
    nc = n // CHUNK
    q, k, v = (a.astype(f32).reshape(b, h, nc, CHUNK, a.shape[-1]) for a in (q, k, v))
    g, beta = (a.astype(f32).reshape(b, h, nc, CHUNK) for a in (g, beta))
    gc = jnp.cumsum(g, -1)
    idx = jnp.arange(CHUNK)
    incl = idx[:, None] >= idx[None, :]
    strict = idx[:, None] > idx[None, :]
    gamma = jnp.exp(jnp.where(incl, gc[..., :, None] - gc[..., None, :], -jnp.inf))
    kb = k * beta[..., None]
    lmat = jnp.where(strict, jnp.einsum("bhncd,bhnsd->bhncs", kb, k) * gamma, 0.0)
    eye = jnp.broadcast_to(jnp.eye(CHUNK, dtype=f32), lmat.shape)
    t_inv = lax.linalg.triangular_solve(lmat, eye, left_side=True, lower=True,
                                        unit_diagonal=True)
    u_vals = t_inv @ (v * beta[..., None])
    w_vals = t_inv @ (kb * jnp.exp(gc)[..., None])
    q_dec = q * jnp.exp(gc)[..., None]
    a_intra = jnp.einsum("bhncd,bhnsd->bhncs", q, k) * gamma
    g_last = gc[..., -1]
    k_dec = k * jnp.exp(g_last[..., None] - gc)[..., None]
    d_last = jnp.exp(g_last)

    def step(s, inp):
        u_c, w_c, q_c, a_c, k_c, d_c = inp
        v_new = u_c - w_c @ s
        o = q_c @ s + a_c @ v_new
        s = s * d_c[..., None, None] + jnp.swapaxes(k_c, -1, -2) @ v_new
        return s, o

    xs = tuple(jnp.moveaxis(a, 2, 0) for a in (u_vals, w_vals, q_dec, a_intra, k_dec, d_last))
    s_fin, o = lax.scan(step, s0.astype(f32), xs)
    return jnp.moveaxis(o, 0, 2).reshape(b, h, n, dv), s_fin


def _flip(a):
    return jnp.flip(a, axis=2)


def _mla_mixer(u, uc, w_in, q_norm, kv_norm, w_qb, w_kvb, w_out, with_ctx):
    def project(t, positioned):
        b, n, _ = t.shape
        cq, ckv, k_rope = jnp.split(t @ w_in, _cuts(MLA_Q_LORA, MLA_KV_LORA), axis=-1)
        k_rope = k_rope[:, None]
        q = _heads(_rms_norm(cq, q_norm) @ w_qb, MLA_HEADS)
        kv = _heads(_rms_norm(ckv, kv_norm) @ w_kvb, MLA_HEADS)
        q_nope, q_rope = q[..., :MLA_NOPE], q[..., MLA_NOPE:]
        if positioned:
            cos, sin = _axial_rope_tables(n, MLA_ROPE)
            q_rope = _apply_rope(q_rope, cos, sin)
            k_rope = _apply_rope(k_rope, cos, sin)
        q = jnp.concatenate([q_nope, q_rope], -1)
        k = jnp.concatenate([kv[..., :MLA_NOPE],
                             jnp.broadcast_to(k_rope, (b, MLA_HEADS, n, MLA_ROPE))], -1)
        return q, k, kv[..., MLA_NOPE:]

    scale = (MLA_NOPE + MLA_ROPE) ** -0.5
    q, k, v = project(u, True)
    qc, kc, vc = project(uc, False)
    k_all = jnp.concatenate([k, kc], axis=2)
    v_all = jnp.concatenate([v, vc], axis=2)
    o = _sweep_query_blocks(lambda blk: _softmax_attend(blk[0], k_all, v_all, scale), (q,))
    y = _merge_heads(o) @ w_out
    yc = _merge_heads(_softmax_attend(qc, kc, vc, scale)) @ w_out if with_ctx else None
    return y, yc


def _diff_mixer(u, uc, w_in, lam_q1, lam_k1, lam_q2, lam_k2, subln, w_out, lambda_init, with_ctx):
    hd = DIFF_HEAD_DIM

    def project(t, positioned):
        n = t.shape[1]
        q, k, v = (_heads(a, DIFF_HEADS) for a in jnp.split(t @ w_in, 3, axis=-1))
        if positioned:
            cos, sin = _axial_rope_tables(n, hd)
            rot = lambda a: jnp.concatenate([_apply_rope(a[..., :hd], cos, sin),
                                             _apply_rope(a[..., hd:], cos, sin)], -1)
            q, k = rot(q), rot(k)
        return q[..., :hd], q[..., hd:], k[..., :hd], k[..., hd:], v

    lam = (jnp.exp(jnp.sum(lam_q1.astype(jnp.float32) * lam_k1))
           - jnp.exp(jnp.sum(lam_q2.astype(jnp.float32) * lam_k2)) + lambda_init)
    scale = hd ** -0.5

    def attend(q1, q2, k1, k2, v):
        p = _softmax_probs(q1, k1, scale) - lam * _softmax_probs(q2, k2, scale)
        o = jnp.einsum("bhqk,bhkd->bhqd", p.astype(v.dtype), v)
        return _rms_norm(o, subln) * (1.0 - lambda_init)

    q1, q2, k1, k2, v = project(u, True)
    q1c, q2c, k1c, k2c, vc = project(uc, False)
    k1a = jnp.concatenate([k1, k1c], axis=2)
    k2a = jnp.concatenate([k2, k2c], axis=2)
    va = jnp.concatenate([v, vc], axis=2)
    o = _sweep_query_blocks(lambda blk: attend(blk[0], blk[1], k1a, k2a, va), (q1, q2))
    y = _merge_heads(o) @ w_out
    yc = _merge_heads(attend(q1c, q2c, k1c, k2c, vc)) @ w_out if with_ctx else None
    return y, yc


def _gla_mixer(u, uc, w_in, gate_w_fwd, gate_b_fwd, gate_w_bwd, gate_b_bwd, norm_g, w_out, with_ctx):
    cuts = _cuts(GLA_KEY_DIM, GLA_KEY_DIM, GLA_VALUE_DIM, GLA_VALUE_DIM, GLA_GATE_RANK)

    def project(t):
        q, k, v, g, r_f, r_b = jnp.split(t @ w_in, cuts, axis=-1)
        la_f = jax.nn.log_sigmoid((r_f @ gate_w_fwd + gate_b_fwd).astype(jnp.float32)) / GLA_GATE_NORM
        la_b = jax.nn.log_sigmoid((r_b @ gate_w_bwd + gate_b_bwd).astype(jnp.float32)) / GLA_GATE_NORM
        return (_heads(q, GLA_HEADS) * GLA_DK ** -0.5, _heads(k, GLA_HEADS), _heads(v, GLA_HEADS), g,
                _heads(la_f, GLA_HEADS), _heads(la_b, GLA_HEADS))

    q, k, v, g, la_f, la_b = project(u)
    qc, kc, vc, gc, lac_f, lac_b = project(uc)
    zero = jnp.zeros((u.shape[0], GLA_HEADS, GLA_DK, GLA_DV), jnp.float32)
    oc_f, s_f = _gla_chunked(qc, kc, vc, lac_f, zero)
    oc_b, s_b = _gla_chunked(_flip(qc), _flip(kc), _flip(vc), _flip(lac_b), zero)
    o_f, _ = _gla_chunked(q, k, v, la_f, s_f)
    o_b, _ = _gla_chunked(_flip(q), _flip(k), _flip(v), _flip(la_b), s_b)

    def out(o, gate):
        return (_merge_heads(_rms_norm(o, norm_g)).astype(gate.dtype) * jax.nn.silu(gate)) @ w_out

    y = out(o_f + _flip(o_b), g)
    yc = out(oc_f + _flip(oc_b), gc) if with_ctx else None
    return y, yc


def _gdn_mixer(u, uc, w_in, conv_w, a_log_fwd, dt_bias_fwd, a_log_bwd, dt_bias_bwd, norm_g, w_out,
               with_ctx):
    hv = GDN_VALUE_HEADS
    cuts = _cuts(GDN_CONV_CH, GDN_VALUE_DIM, hv, hv, hv)
    rep = GDN_VALUE_HEADS // GDN_KEY_HEADS

    def decay(a, a_log, dt_bias):
        return (-jnp.exp(a_log) * jax.nn.softplus(a.astype(jnp.float32) + dt_bias)).transpose(0, 2, 1)

    def project(t):
        qkv, z, b_f, b_b, a_f, a_b = jnp.split(t @ w_in, cuts, axis=-1)
        qkv = jax.nn.silu(_centred_dwconv(qkv, conv_w))
        q, k, v = jnp.split(qkv, _cuts(GDN_KEY_DIM, GDN_KEY_DIM), axis=-1)
        q = jnp.repeat(_l2_norm(_heads(q, GDN_KEY_HEADS)), rep, axis=1) * GDN_HEAD_DIM ** -0.5
        k = jnp.repeat(_l2_norm(_heads(k, GDN_KEY_HEADS)), rep, axis=1)
        v = _heads(v, GDN_VALUE_HEADS)
        beta_f = jax.nn.sigmoid(b_f.astype(jnp.float32)).transpose(0, 2, 1)
        beta_b = jax.nn.sigmoid(b_b.astype(jnp.float32)).transpose(0, 2, 1)
        return (q, k, v, z, decay(a_f, a_log_fwd, dt_bias_fwd), beta_f,
                decay(a_b, a_log_bwd, dt_bias_bwd), beta_b)

    q, k, v, z, g_f, be_f, g_b, be_b = project(u)
    qc, kc, vc, zc, gc_f, bec_f, gc_b, bec_b = project(uc)
    zero = jnp.zeros((u.shape[0], GDN_VALUE_HEADS, GDN_HEAD_DIM, GDN_HEAD_DIM), jnp.float32)
    oc_f, s_f = _gated_delta_chunked(qc, kc, vc, gc_f, bec_f, zero)
    oc_b, s_b = _gated_delta_chunked(_flip(qc), _flip(kc), _flip(vc), _flip(gc_b), _flip(bec_b), zero)
    o_f, _ = _gated_delta_chunked(q, k, v, g_f, be_f, s_f)
    o_b, _ = _gated_delta_chunked(_flip(q), _flip(k), _flip(v), _flip(g_b), _flip(be_b), s_b)

    def out(o, gate):
        return (_merge_heads(_rms_norm(o, norm_g)).astype(gate.dtype) * jax.nn.silu(gate)) @ w_out

    y = out(o_f + _flip(o_b), z)
    yc = out(oc_f + _flip(oc_b), zc) if with_ctx else None
    return y, yc


def _sq_relu_mlp(u, w1, w2):
    return jnp.square(jax.nn.relu(u @ w1)) @ w2


def setup_inputs(seed: int = 0) -> dict:
    key = jax.random.key(seed)
    ks = iter(jax.random.split(key, 48))
    f32 = jnp.float32
    D = D_MODEL

    def normal(shape, std):
        return std * jax.random.normal(next(ks), shape, f32)

    def gain(shape):
        return 1.0 + normal(shape, 0.02)

    def a_log(n):
        return jnp.log(jax.random.uniform(next(ks), (n, GDN_VALUE_HEADS), f32, 1.0, 16.0))

    def dt_bias(n):
        dt = jnp.exp(jax.random.uniform(next(ks), (n, GDN_VALUE_HEADS), f32,
                                        math.log(1e-3), math.log(1e-1)))
        return dt + jnp.log(-jnp.expm1(-dt))

    return {
        "x": normal((BATCH, SEQ, D), 1.0),
        "c": normal((BATCH, D), 1.0),
        "ctx": normal((BATCH, CTX_LEN, D), 1.0),
        "c_ctx": normal((D,), 1.0),
        "ada_w": normal((DEPTH, D, 6 * D), D ** -0.5),
        "ada_b": normal((DEPTH, 6 * D), 0.02),
        "ln1_g": gain((DEPTH, D)),
        "ln1_b": normal((DEPTH, D), 0.02),
        "ln2_g": gain((DEPTH, D)),
        "ln2_b": normal((DEPTH, D), 0.02),
        "mlp_w1": normal((DEPTH, D, D_FF), D ** -0.5),
        "mlp_w2": normal((DEPTH, D_FF, D), BETA * D_FF ** -0.5),
        "mla_w_in": normal((N_A, D, MLA_Q_LORA + MLA_KV_LORA + MLA_ROPE), D ** -0.5),
        "mla_q_norm": gain((N_A, MLA_Q_LORA)),
        "mla_kv_norm": gain((N_A, MLA_KV_LORA)),
        "mla_w_qb": normal((N_A, MLA_Q_LORA, MLA_HEADS * (MLA_NOPE + MLA_ROPE)), MLA_Q_LORA ** -0.5),
        "mla_w_kvb": normal((N_A, MLA_KV_LORA, MLA_HEADS * (MLA_NOPE + MLA_V)), MLA_KV_LORA ** -0.5),
        "mla_w_out": normal((N_A, MLA_HEADS * MLA_V, D), BETA * (MLA_HEADS * MLA_V) ** -0.5),
        "diff_w_in": normal((N_B, D, 3 * DIFF_WIDTH), D ** -0.5),
        "diff_lambda_q1": normal((N_B, DIFF_HEAD_DIM), 0.1),
        "diff_lambda_k1": normal((N_B, DIFF_HEAD_DIM), 0.1),
        "diff_lambda_q2": normal((N_B, DIFF_HEAD_DIM), 0.1),
        "diff_lambda_k2": normal((N_B, DIFF_HEAD_DIM), 0.1),
        "diff_subln": gain((N_B, 2 * DIFF_HEAD_DIM)),
        "diff_w_out": normal((N_B, DIFF_WIDTH, D), BETA * DIFF_WIDTH ** -0.5),
        "gla_w_in": normal((N_C, D, GLA_IN_WIDTH), D ** -0.5),
        "gla_gate_w_fwd": normal((N_C, GLA_GATE_RANK, GLA_KEY_DIM), GLA_GATE_RANK ** -0.5),
        "gla_gate_b_fwd": normal((N_C, GLA_KEY_DIM), 0.02),
        "gla_gate_w_bwd": normal((N_C, GLA_GATE_RANK, GLA_KEY_DIM), GLA_GATE_RANK ** -0.5),
        "gla_gate_b_bwd": normal((N_C, GLA_KEY_DIM), 0.02),
        "gla_norm": gain((N_C, GLA_DV)),
        "gla_w_out": normal((N_C, GLA_VALUE_DIM, D), BETA * GLA_VALUE_DIM ** -0.5),
        "gdn_w_in": normal((N_D, D, GDN_IN_WIDTH), D ** -0.5),
        "gdn_conv_w": normal((N_D, GDN_CONV, GDN_CONV_CH), GDN_CONV ** -0.5),
        "gdn_a_log_fwd": a_log(N_D),
        "gdn_dt_bias_fwd": dt_bias(N_D),
        "gdn_a_log_bwd": a_log(N_D),
        "gdn_dt_bias_bwd": dt_bias(N_D),
        "gdn_norm": gain((N_D, GDN_HEAD_DIM)),
        "gdn_w_out": normal((N_D, GDN_VALUE_DIM, D), BETA * GDN_VALUE_DIM ** -0.5),
    }


def reference(x, c, ctx, c_ctx, ada_w, ada_b, ln1_g, ln1_b, ln2_g, ln2_b, mlp_w1, mlp_w2,
              mla_w_in, mla_q_norm, mla_kv_norm, mla_w_qb, mla_w_kvb, mla_w_out,
              diff_w_in, diff_lambda_q1, diff_lambda_k1, diff_lambda_q2, diff_lambda_k2, diff_subln,
              diff_w_out,
              gla_w_in, gla_gate_w_fwd, gla_gate_b_fwd, gla_gate_w_bwd, gla_gate_b_bwd, gla_norm,
              gla_w_out,
              gdn_w_in, gdn_conv_w, gdn_a_log_fwd, gdn_dt_bias_fwd, gdn_a_log_bwd, gdn_dt_bias_bwd,
              gdn_norm, gdn_w_out):
    s_lat = jax.nn.silu(c)
    s_ctx = jax.nn.silu(c_ctx)
    h, hc = x, ctx
    for i in range(DEPTH):
        with_ctx = i < DEPTH - 1
        kind, j = i % N_MIXERS, i // N_MIXERS
        sh1, sc1, g1, sh2, sc2, g2 = jnp.split((s_lat @ ada_w[i] + ada_b[i])[:, None, :], 6, axis=-1)
        csh1, csc1, cg1, csh2, csc2, cg2 = jnp.split(s_ctx @ ada_w[i] + ada_b[i], 6, axis=-1)
        u = h * (1.0 + sc1) + sh1
        uc = hc * (1.0 + csc1) + csh1
        if kind == 0:
            y, yc = _mla_mixer(u, uc, mla_w_in[j], mla_q_norm[j], mla_kv_norm[j], mla_w_qb[j],
                               mla_w_kvb[j], mla_w_out[j], with_ctx)
        elif kind == 1:
            lambda_init = 0.8 - 0.6 * math.exp(-0.3 * i)
            y, yc = _diff_mixer(u, uc, diff_w_in[j], diff_lambda_q1[j], diff_lambda_k1[j],
                                diff_lambda_q2[j], diff_lambda_k2[j], diff_subln[j], diff_w_out[j],
                                lambda_init, with_ctx)
        elif kind == 2:
            y, yc = _gla_mixer(u, uc, gla_w_in[j], gla_gate_w_fwd[j], gla_gate_b_fwd[j],
                               gla_gate_w_bwd[j], gla_gate_b_bwd[j], gla_norm[j], gla_w_out[j],
                               with_ctx)
        else:
            y, yc = _gdn_mixer(u, uc, gdn_w_in[j], gdn_conv_w[j], gdn_a_log_fwd[j], gdn_dt_bias_fwd[j],
                               gdn_a_log_bwd[j], gdn_dt_bias_bwd[j], gdn_norm[j], gdn_w_out[j],
                               with_ctx)
        h = _layer_norm(ALPHA * h + g1 * y, ln1_g[i], ln1_b[i])
        h = _layer_norm(ALPHA * h + g2 * _sq_relu_mlp(h * (1.0 + sc2) + sh2, mlp_w1[i], mlp_w2[i]),
                        ln2_g[i], ln2_b[i])
        if with_ctx:
            hc = _layer_norm(ALPHA * hc + cg1 * yc, ln1_g[i], ln1_b[i])
            hc = _layer_norm(ALPHA * hc + cg2 * _sq_relu_mlp(hc * (1.0 + csc2) + csh2, mlp_w1[i],
                                                              mlp_w2[i]), ln2_g[i], ln2_b[i])
    return h
```

```python
import functools
import math

import jax
import jax.numpy as jnp
from jax import lax
from jax.experimental import pallas as pl
from jax.experimental.pallas import tpu as pltpu

F32 = jnp.float32
BF16 = jnp.bfloat16

D_MODEL = 1024
DEPTH = 4
GRID_W = 64
D_FF = 4 * D_MODEL
ALPHA = (2 * DEPTH) ** 0.25
NORM_EPS = 1e-6
ROPE_BASE = 10000.0
CHUNK = 64

MLA_HEADS = 16
MLA_NOPE = 64
MLA_ROPE = 32
MLA_V = 64
MLA_KV_LORA = 256
MLA_Q_LORA = 768

DIFF_HEAD_DIM = 64
DIFF_HEADS = 8

GLA_HEADS = 4
GLA_KEY_DIM = 512
GLA_VALUE_DIM = 1024
GLA_DK = 128
GLA_DV = 256
GLA_GATE_RANK = 16
GLA_GATE_NORM = 16.0

GDN_HEAD_DIM = 128
GDN_KEY_HEADS = 8
GDN_VALUE_HEADS = 16
GDN_KEY_DIM = 1024
GDN_VALUE_DIM = 2048
GDN_CONV = 5
GDN_CONV_CH = 4096

LANE = 128
ROW_TILE = 256
HALO = 8
VMEM_LIMIT = 56 * 1024 * 1024


def _cparams(*sem):
    return pltpu.CompilerParams(dimension_semantics=sem, vmem_limit_bytes=VMEM_LIMIT)


def _const_spec(shape):
    nd = len(shape)
    return pl.BlockSpec(shape, lambda *_: (0,) * nd, pipeline_mode=pl.Buffered(1))


def _dot(a, b):
    return jnp.dot(a.astype(BF16), b.astype(BF16), preferred_element_type=F32)


def _dot_nt(a, b):
    return lax.dot_general(a.astype(BF16), b.astype(BF16), (((1,), (1,)), ((), ())),
                           preferred_element_type=F32)


def _dot_exact(a, b):
    return jnp.dot(a, b, preferred_element_type=F32, precision=lax.Precision.HIGHEST)


def _dot_nt_exact(a, b):
    return lax.dot_general(a, b, (((1,), (1,)), ((), ())), preferred_element_type=F32,
                           precision=lax.Precision.HIGHEST)


def _rms(x, g):
    return x * lax.rsqrt(jnp.mean(jnp.square(x), -1, keepdims=True) + NORM_EPS) * g


def _layer_norm(z, g, b):
    mu = jnp.mean(z, -1, keepdims=True)
    zc = z - mu
    var = jnp.mean(jnp.square(zc), -1, keepdims=True)
    return zc * lax.rsqrt(var + NORM_EPS) * g + b


def _softplus(x):
    return jnp.maximum(x, 0.0) + jnp.log1p(jnp.exp(-jnp.abs(x)))


def _silu(x):
    return x * jax.nn.sigmoid(x)


def _adaln_kernel(s_ref, w_ref, b_ref, o_ref):
    o_ref[...] = _dot(_silu(s_ref[...]), w_ref[...]) + b_ref[...]


def _adaln(s, ada_w, ada_b):
    depth, d, n = ada_w.shape
    rows = s.shape[0]
    tn = 1536
    return pl.pallas_call(
        _adaln_kernel,
        grid=(depth, n // tn),
        in_specs=[pl.BlockSpec((rows, d), lambda l, j: (0, 0)),
                  pl.BlockSpec((None, d, tn), lambda l, j: (l, 0, j)),
                  pl.BlockSpec((None, 1, tn), lambda l, j: (l, 0, j))],
        out_specs=pl.BlockSpec((None, rows, tn), lambda l, j: (l, 0, j)),
        out_shape=jax.ShapeDtypeStruct((depth, rows, n), F32),
        compiler_params=_cparams("parallel", "parallel"),
        name="adaln",
    )(s, ada_w, ada_b.reshape(depth, 1, n))


class _Geom:
    def __init__(self, batch, seq, ctx):
        assert seq % ROW_TILE == 0 and ctx % ROW_TILE == 0
        self.batch, self.seq, self.ctx = batch, seq, ctx
        self.nt = seq + ctx
        self.lat_tiles = seq // ROW_TILE
        self.tiles = self.nt // ROW_TILE

    def mod_spec(self, k):
        lat_tiles, batch = self.lat_tiles, self.batch
        return pl.BlockSpec((None, 1, D_MODEL),
                            lambda b, t: (jnp.where(t < lat_tiles, b, batch), 0, k))

    def row_spec(self, width, tile=ROW_TILE):
        return pl.BlockSpec((None, tile, width), lambda b, t: (b, t, 0))


def _rope_angles(n, dim):
    rows = n // GRID_W
    row = jnp.repeat(jnp.arange(rows, dtype=F32), GRID_W)
    col = jnp.tile(jnp.arange(GRID_W, dtype=F32), rows)
    n_freq = dim // 4
    inv_freq = ROPE_BASE ** (-jnp.arange(n_freq, dtype=F32) / n_freq)
    ang = jnp.concatenate([row[:, None] * inv_freq, col[:, None] * inv_freq], -1)
    return jnp.cos(ang), jnp.sin(ang)


def _rope_tables(geom, dim, layout):
    cos, sin = _rope_angles(geom.seq, dim)
    one, zero = jnp.ones_like(cos), jnp.zeros_like(cos)
    c = jnp.concatenate([one if g == "-" else cos for g in layout], -1)
    sa = jnp.concatenate([-sin if g == "e" else zero for g in layout], -1)
    sb = jnp.concatenate([sin if g == "o" else zero for g in layout], -1)
    pad = lambda t, v: jnp.concatenate([t, jnp.full((geom.ctx, LANE), v, F32)], 0)
    return pad(c, 1.0), pad(sa, 0.0), pad(sb, 0.0)


def _rope(x, c, sa, sb, w):
    return x * c + pltpu.roll(x, LANE - w, 1) * sa + pltpu.roll(x, w, 1) * sb


def _deinterleave(n):
    return list(range(0, n, 2)) + list(range(1, n, 2))


def _mla_proj_kernel(h_ref, sh_ref, sc_ref, win_ref, qn_ref, kvn_ref, wqb_ref, wkvb_ref,
                     c_ref, sa_ref, sb_ref, q_ref, k_ref, v_ref):
    u = h_ref[...] * (1.0 + sc_ref[...]) + sh_ref[...]
    t = _dot(u, win_ref[...])
    cq, ckv, kr = t[:, :MLA_Q_LORA], t[:, MLA_Q_LORA:MLA_Q_LORA + MLA_KV_LORA], t[:, -LANE:]
    q = _dot(_rms(cq, qn_ref[...]), wqb_ref[...])
    kv = _dot(_rms(ckv, kvn_ref[...]), wkvb_ref[...])
    c, sa, sb = c_ref[...], sa_ref[...], sb_ref[...]
    half = MLA_ROPE // 2
    scale = (MLA_NOPE + MLA_ROPE) ** -0.5
    kr = _rope(pltpu.roll(kr, MLA_NOPE, 1), c, sa, sb, half)
    for hd in range(MLA_HEADS):
        sl = slice(hd * LANE, (hd + 1) * LANE)
        q_ref[:, sl] = (_rope(q[:, sl], c, sa, sb, half) * scale).astype(q_ref.dtype)
        k_ref[:, sl] = (kv[:, sl] + kr).astype(k_ref.dtype)
    v_ref[...] = kv[:, MLA_HEADS * LANE:].astype(v_ref.dtype)


def _softmax_pv(s, v):
    m = jnp.max(s, -1, keepdims=True)
    p = jnp.exp(s - m)
    l = jnp.sum(p, -1, keepdims=True)
    return _dot(p, v) / l


def _mla_attn_kernel(q_ref, k_ref, v_ref, o_ref):
    v = v_ref[...]
    outs = []
    for j in range(2):
        sl = slice(j * LANE, (j + 1) * LANE)
        outs.append(_softmax_pv(_dot_nt(q_ref[:, sl], k_ref[:, sl]), v))
    lane = lax.broadcasted_iota(jnp.int32, outs[0].shape, 1)
    o_ref[...] = jnp.where(lane < MLA_V, outs[0], outs[1]).astype(o_ref.dtype)


def _attention(kernel, geom, q, k, v, extra, head_groups, qw, kw, vw, ow, name):
    batch = geom.batch
    extra_specs = [_const_spec(e.shape) for e in extra]
    out_shape = jax.ShapeDtypeStruct((batch, geom.nt, head_groups * ow), BF16)

    def call(q_tiles, q_off, k_rows, k_blk, alias):
        specs = extra_specs + [
            pl.BlockSpec((None, ROW_TILE, qw), lambda b, g, t: (b, t + q_off, g)),
            pl.BlockSpec((None, k_rows, kw), lambda b, g, t: (b, k_blk, g)),
            pl.BlockSpec((None, k_rows, vw), lambda b, g, t: (b, k_blk, g)),
        ]
        args = list(extra) + [q, k, v]
        aliases = {}
        if alias is not None:
            specs.append(pl.BlockSpec(memory_space=pl.ANY))
            args.append(alias)
            aliases = {len(args) - 1: 0}
        body = kernel if alias is None else (lambda *refs: kernel(*refs[:-2], refs[-1]))
        return pl.pallas_call(
            body,
            grid=(batch, head_groups, q_tiles),
            in_specs=specs,
            out_specs=pl.BlockSpec((None, ROW_TILE, ow), lambda b, g, t: (b, t + q_off, g)),
            out_shape=out_shape,
            input_output_aliases=aliases,
            compiler_params=_cparams("parallel", "parallel", "arbitrary"),
            name=name,
        )(*args)

    o = call(geom.lat_tiles, 0, geom.nt, 0, None)
    ctx_tiles = geom.tiles - geom.lat_tiles
    return call(ctx_tiles, geom.lat_tiles, geom.ctx, geom.seq // geom.ctx, o)


def _mla_layer(geom, hx, mod, w_in, q_norm, kv_norm, w_qb, w_kvb):
    d = D_MODEL
    perm = jnp.array(_deinterleave(MLA_ROPE))
    w_in_p = jnp.concatenate(
        [w_in[:, :MLA_Q_LORA + MLA_KV_LORA], w_in[:, MLA_Q_LORA + MLA_KV_LORA:][:, perm],
         jnp.zeros((d, LANE - MLA_ROPE), F32)], -1).astype(BF16)
    wq = w_qb.reshape(MLA_Q_LORA, MLA_HEADS, MLA_NOPE + MLA_ROPE)
    wq = jnp.concatenate([wq[..., :MLA_NOPE], wq[..., MLA_NOPE:][..., perm],
                          jnp.zeros((MLA_Q_LORA, MLA_HEADS, LANE - MLA_NOPE - MLA_ROPE), F32)], -1)
    wq = wq.reshape(MLA_Q_LORA, MLA_HEADS * LANE).astype(BF16)
    wkv = w_kvb.reshape(MLA_KV_LORA, MLA_HEADS, MLA_NOPE + MLA_V)
    wk = jnp.concatenate([wkv[..., :MLA_NOPE],
                          jnp.zeros((MLA_KV_LORA, MLA_HEADS, LANE - MLA_NOPE), F32)], -1)
    wkv = jnp.concatenate([wk.reshape(MLA_KV_LORA, MLA_HEADS * LANE),
                           wkv[..., MLA_NOPE:].reshape(MLA_KV_LORA, MLA_HEADS * MLA_V)], -1)
    wkv = wkv.astype(BF16)
    tables = _rope_tables(geom, MLA_ROPE, ["-"] * 4 + ["e", "o"] + ["-"] * 2)
    tab_spec = pl.BlockSpec((ROW_TILE, LANE), lambda b, t: (t, 0))
    qk_w = MLA_HEADS * LANE
    q, k, v = pl.pallas_call(
        _mla_proj_kernel,
        grid=(geom.batch, geom.tiles),
        in_specs=[geom.row_spec(d), geom.mod_spec(0), geom.mod_spec(1),
                  _const_spec(w_in_p.shape), _const_spec((1, MLA_Q_LORA)),
                  _const_spec((1, MLA_KV_LORA)), _const_spec(wq.shape), _const_spec(wkv.shape),
                  tab_spec, tab_spec, tab_spec],
        out_specs=[geom.row_spec(qk_w), geom.row_spec(qk_w), geom.row_spec(MLA_HEADS * MLA_V)],
        out_shape=[jax.ShapeDtypeStruct((geom.batch, geom.nt, qk_w), BF16),
                   jax.ShapeDtypeStruct((geom.batch, geom.nt, qk_w), BF16),
                   jax.ShapeDtypeStruct((geom.batch, geom.nt, MLA_HEADS * MLA_V), BF16)],
        compiler_params=_cparams("parallel", "parallel"),
        name="mla_proj",
    )(hx, mod, mod, w_in_p, q_norm.reshape(1, -1), kv_norm.reshape(1, -1), wq, wkv, *tables)
    return _attention(_mla_attn_kernel, geom, q, k, v, [], MLA_HEADS // 2,
                      2 * LANE, 2 * LANE, LANE, LANE, "mla_attn")


def _diff_proj_kernel(h_ref, sh_ref, sc_ref, win_ref, c_ref, sa_ref, sb_ref, q_ref, k_ref, v_ref):
    u = h_ref[...] * (1.0 + sc_ref[...]) + sh_ref[...]
    t = _dot(u, win_ref[...])
    c, sa, sb = c_ref[...], sa_ref[...], sb_ref[...]
    half = DIFF_HEAD_DIM // 2
    width = DIFF_HEADS * LANE
    scale = DIFF_HEAD_DIM ** -0.5
    for hd in range(DIFF_HEADS):
        sl = slice(hd * LANE, (hd + 1) * LANE)
        q_ref[:, sl] = (_rope(t[:, sl], c, sa, sb, half) * scale).astype(q_ref.dtype)
        k_ref[:, sl] = _rope(t[:, width + hd * LANE:width + (hd + 1) * LANE],
                             c, sa, sb, half).astype(k_ref.dtype)
    v_ref[...] = t[:, 2 * width:].astype(v_ref.dtype)


def _diff_attn_kernel(lam_ref, subln_ref, q_ref, k_ref, v_ref, o_ref, *, lambda_init):
    lam_p = lam_ref[...]
    lam = (jnp.exp(jnp.sum(lam_p[0:1] * lam_p[1:2], -1, keepdims=True))
           - jnp.exp(jnp.sum(lam_p[2:3] * lam_p[3:4], -1, keepdims=True)) + lambda_init)
    q, k, v = q_ref[...], k_ref[...], v_ref[...]
    lane = lax.broadcasted_iota(jnp.int32, q.shape, 1)
    zero = jnp.zeros_like(q)
    o1 = _softmax_pv(_dot_nt(jnp.where(lane < DIFF_HEAD_DIM, q, zero), k), v)
    o2 = _softmax_pv(_dot_nt(jnp.where(lane < DIFF_HEAD_DIM, zero, q), k), v)
    o = o1 - lam * o2
    o_ref[...] = (_rms(o, subln_ref[...]) * (1.0 - lambda_init)).astype(o_ref.dtype)


def _diff_layer(geom, hx, mod, w_in, lam_q1, lam_k1, lam_q2, lam_k2, subln, lambda_init):
    d = D_MODEL
    hd = DIFF_HEAD_DIM
    half_perm = _deinterleave(hd)
    head_perm = half_perm + [hd + p for p in half_perm]
    width = DIFF_HEADS * LANE
    qk_perm = jnp.array([h * LANE + p for h in range(DIFF_HEADS) for p in head_perm])
    w_in_p = jnp.concatenate([w_in[:, :width][:, qk_perm], w_in[:, width:2 * width][:, qk_perm],
                              w_in[:, 2 * width:]], -1).astype(BF16)
    tables = _rope_tables(geom, hd, ["e", "o", "e", "o"])
    tab_spec = pl.BlockSpec((ROW_TILE, LANE), lambda b, t: (t, 0))
    out = jax.ShapeDtypeStruct((geom.batch, geom.nt, width), BF16)
    q, k, v = pl.pallas_call(
        _diff_proj_kernel,
        grid=(geom.batch, geom.tiles),
        in_specs=[geom.row_spec(d), geom.mod_spec(0), geom.mod_spec(1), _const_spec(w_in_p.shape),
                  tab_spec, tab_spec, tab_spec],
        out_specs=[geom.row_spec(width)] * 3,
        out_shape=[out] * 3,
        compiler_params=_cparams("parallel", "parallel"),
        name="diff_proj",
    )(hx, mod, mod, w_in_p, *tables)
    lam_p = jnp.stack([lam_q1, lam_k1, lam_q2, lam_k2]).astype(F32)
    kern = functools.partial(_diff_attn_kernel, lambda_init=lambda_init)
    return _attention(kern, geom, q, k, v, [lam_p, subln.reshape(1, -1)], DIFF_HEADS,
                      LANE, LANE, LANE, LANE, "diff_attn")


def _chain_block(geom, rev):
    tiles, lat = geom.tiles, geom.lat_tiles
    if rev:
        return lambda j: tiles - 1 - j
    ctx_tiles = tiles - lat
    return lambda j: jnp.where(j < ctx_tiles, j + lat, j - ctx_tiles)


def _tri_mask(n, rev, strict):
    r = lax.broadcasted_iota(jnp.int32, (n, n), 0)
    c = lax.broadcasted_iota(jnp.int32, (n, n), 1)
    if rev:
        r, c = c, r
    return r > c if strict else r >= c


def _gla_proj_kernel(h_ref, sh_ref, sc_ref, win_ref, q_ref, k_ref, v_ref, g_ref, r_ref):
    u = h_ref[...] * (1.0 + sc_ref[...]) + sh_ref[...]
    t = _dot(u, win_ref[...])
    kd, vd = GLA_KEY_DIM, GLA_VALUE_DIM
    q_ref[...] = t[:, :kd] * GLA_DK ** -0.5
    k_ref[...] = t[:, kd:2 * kd]
    v_ref[...] = t[:, 2 * kd:2 * kd + vd].astype(v_ref.dtype)
    g_ref[...] = t[:, 2 * kd + vd:2 * kd + 2 * vd]
    r_ref[...] = t[:, 2 * kd + 2 * vd:]


def _gla_scan_kernel(q_ref, k_ref, v_ref, r_ref, gw_ref, gb_ref, o_ref, st_ref, *, rev):
    @pl.when(pl.program_id(1) == 0)
    def _():
        st_ref[...] = jnp.zeros_like(st_ref)

    la = _dot(r_ref[...], gw_ref[...]) + gb_ref[...]
    la = -_softplus(-la) / GLA_GATE_NORM
    tri = _tri_mask(CHUNK, rev, False)
    tri_f = tri.astype(F32)
    n_chunks = ROW_TILE // CHUNK
    edge = 0 if rev else CHUNK - 1
    for ci in (range(n_chunks - 1, -1, -1) if rev else range(n_chunks)):
        rows = slice(ci * CHUNK, (ci + 1) * CHUNK)
        for hd in range(GLA_HEADS):
            ksl = slice(hd * GLA_DK, (hd + 1) * GLA_DK)
            vsl = slice(hd * GLA_DV, (hd + 1) * GLA_DV)
            cum = _dot_exact(tri_f, la[rows, ksl])
            last = cum[edge:edge + 1]
            q, k, v = q_ref[rows, ksl], k_ref[rows, ksl], v_ref[rows, vsl]
            q_dec = q * jnp.exp(cum)
            k_inv = k * jnp.exp(-cum)
            k_end = k * jnp.exp(last - cum)
            a = jnp.where(tri, _dot_nt(q_dec, k_inv), 0.0)
            st = st_ref[hd]
            o_ref[rows, vsl] = _dot(a, v) + _dot_nt(q_dec, st)
            st_ref[hd] = st * jnp.exp(last) + _dot(v.astype(F32).T, k_end)


def _gla_layer(geom, hx, mod, w_in, gate_w_fwd, gate_b_fwd, gate_w_bwd, gate_b_bwd):
    d = D_MODEL
    kd, vd, rk = GLA_KEY_DIM, GLA_VALUE_DIM, GLA_GATE_RANK
    main = 2 * kd + 2 * vd
    w_in_p = jnp.concatenate([w_in, jnp.zeros((d, LANE - 2 * rk), F32)], -1).astype(BF16)
    b, nt = geom.batch, geom.nt
    q, k, v, g, r = pl.pallas_call(
        _gla_proj_kernel,
        grid=(b, geom.tiles),
        in_specs=[geom.row_spec(d), geom.mod_spec(0), geom.mod_spec(1), _const_spec(w_in_p.shape)],
        out_specs=[geom.row_spec(kd), geom.row_spec(kd), geom.row_spec(vd), geom.row_spec(vd),
                   geom.row_spec(LANE)],
        out_shape=[jax.ShapeDtypeStruct((b, nt, kd), F32), jax.ShapeDtypeStruct((b, nt, kd), F32),
                   jax.ShapeDtypeStruct((b, nt, vd), BF16), jax.ShapeDtypeStruct((b, nt, vd), F32),
                   jax.ShapeDtypeStruct((b, nt, LANE), F32)],
        compiler_params=_cparams("parallel", "parallel"),
        name="gla_proj",
    )(hx, mod, mod, w_in_p)
    assert w_in_p.shape[1] == main + LANE

    def scan(rev, gate_w, gate_b, lane_off):
        gw = jnp.zeros((LANE, kd), F32).at[lane_off:lane_off + rk].set(gate_w).astype(BF16)
        blk = _chain_block(geom, rev)
        row = lambda w: pl.BlockSpec((None, ROW_TILE, w), lambda bi, j: (bi, blk(j), 0))
        return pl.pallas_call(
            functools.partial(_gla_scan_kernel, rev=rev),
            grid=(b, geom.tiles),
            in_specs=[row(kd), row(kd), row(vd), row(LANE), _const_spec(gw.shape),
                      _const_spec((1, kd))],
            out_specs=row(vd),
            out_shape=jax.ShapeDtypeStruct((b, nt, vd), F32),
            scratch_shapes=[pltpu.VMEM((GLA_HEADS, GLA_DV, GLA_DK), F32)],
            compiler_params=_cparams("parallel", "arbitrary"),
            name="gla_scan_bwd" if rev else "gla_scan_fwd",
        )(q, k, v, r, gw, gate_b.reshape(1, kd))

    return scan(False, gate_w_fwd, gate_b_fwd, 0), scan(True, gate_w_bwd, gate_b_bwd, rk), g


def _gdn_proj_kernel(hp_ref, h_ref, hn_ref, sh_ref, sc_ref, wc_ref, wz_ref, ws_ref, wst_ref,
                     cw_ref, q_ref, k_ref, v_ref, z_ref, s_ref, st_ref, *, seq, nt):
    t_idx = pl.program_id(1)
    sc, sh = 1.0 + sc_ref[...], sh_ref[...]
    u = h_ref[...] * sc + sh
    row0 = t_idx * ROW_TILE
    has_prev = jnp.logical_and(row0 != 0, row0 != seq)
    has_next = jnp.logical_and(row0 + ROW_TILE != seq, row0 + ROW_TILE != nt)
    u_prev = jnp.where(has_prev, hp_ref[...] * sc + sh, 0.0)
    u_next = jnp.where(has_next, hn_ref[...] * sc + sh, 0.0)
    u_ext = jnp.concatenate([u_prev, u, u_next], 0).astype(BF16)
    u = u.astype(BF16)
    ext = ROW_TILE + 2 * HALO
    blk = 512
    pad = GDN_CONV // 2
    for cb in range(GDN_CONV_CH // blk):
        cols = slice(cb * blk, (cb + 1) * blk)
        te = jnp.dot(u_ext, wc_ref[:, cols], preferred_element_type=F32)
        acc = None
        for j in range(GDN_CONV):
            shifted = te if j == pad else pltpu.roll(te, (pad - j) % ext, 0)
            term = shifted[HALO:HALO + ROW_TILE] * cw_ref[j:j + 1, cols]
            acc = term if acc is None else acc + term
        y = _silu(acc)
        if cols.start < 2 * GDN_KEY_DIM:
            dst, scale, off = ((q_ref, GDN_HEAD_DIM ** -0.5, 0) if cols.start < GDN_KEY_DIM
                               else (k_ref, 1.0, GDN_KEY_DIM))
            for hh in range(blk // LANE):
                yh = y[:, hh * LANE:(hh + 1) * LANE]
                yh = yh * lax.rsqrt(jnp.sum(jnp.square(yh), -1, keepdims=True) + NORM_EPS)
                c0 = cols.start - off + hh * LANE
                dst[:, c0:c0 + LANE] = yh * scale
        else:
            c0 = cols.start - 2 * GDN_KEY_DIM
            v_ref[:, c0:c0 + blk] = y
    z_ref[...] = jnp.dot(u, wz_ref[...], preferred_element_type=F32)
    s_ref[...] = jnp.dot(u, ws_ref[...], preferred_element_type=F32)
    st_ref[...] = lax.dot_general(wst_ref[...], u, (((1,), (1,)), ((), ())),
                                  preferred_element_type=F32)


def _tri_inverse(lmat, dot):
    n = lmat.shape[0]
    r = lax.broadcasted_iota(jnp.int32, (n, n), 0)
    c = lax.broadcasted_iota(jnp.int32, (n, n), 1)
    same16 = (r >> 4) == (c >> 4)
    same32 = (r >> 5) == (c >> 5)
    eye = (r == c).astype(F32)
    d1 = jnp.where(same16, lmat, 0.0)
    d2 = dot(d1, d1)
    d4 = dot(d2, d2)
    d8 = dot(d4, d4)
    x = eye - d1
    x = x + dot(x, d2)
    x = x + dot(x, d4)
    x = x + dot(x, d8)
    c32 = jnp.where(jnp.logical_and(same32, jnp.logical_not(same16)), lmat, 0.0)
    x = x - dot(dot(x, c32), x)
    c64 = jnp.where(same32, 0.0, lmat)
    x = x - dot(dot(x, c64), x)
    return x


def _gdn_scan_kernel(q_ref, k_ref, v_ref, s_ref, st_ref, alog_ref, dtb_ref, alogc_ref, dtbc_ref,
                     o_ref, state_ref, *, rev, lane_off, heads):
    @pl.when(pl.program_id(2) == 0)
    def _():
        state_ref[...] = jnp.zeros_like(state_ref)

    grp = pl.program_id(1)
    small = s_ref[...]
    small_t = st_ref[...]
    beta_all = jax.nn.sigmoid(small)
    g_all = -jnp.exp(alog_ref[...]) * _softplus(small + dtb_ref[...])
    hv = GDN_VALUE_HEADS
    a_rows = small_t[2 * hv + lane_off:2 * hv + lane_off + hv]
    g_t = -jnp.exp(alogc_ref[...]) * _softplus(a_rows + dtbc_ref[...])
    incl = _tri_mask(CHUNK, rev, False)
    strict = _tri_mask(CHUNK, rev, True)
    incl_f = incl.astype(F32)
    lane = lax.broadcasted_iota(jnp.int32, (CHUNK, LANE), 1)
    sub = lax.broadcasted_iota(jnp.int32, (hv, CHUNK), 0)
    n_chunks = ROW_TILE // CHUNK
    edge = 0 if rev else CHUNK - 1
    for ci in (range(n_chunks - 1, -1, -1) if rev else range(n_chunks)):
        rows = slice(ci * CHUNK, (ci + 1) * CHUNK)
        gc_all = _dot_exact(incl_f, g_all[rows])
        gct_all = _dot_nt_exact(g_t[:, rows], incl_f)
        beta_c = beta_all[rows]
        for hh in range(heads):
            head = grp * heads + hh
            pick = lane == head + lane_off
            beta = jnp.sum(jnp.where(pick, beta_c, 0.0), -1, keepdims=True)
            gc = jnp.sum(jnp.where(lane == head + 2 * hv + lane_off, gc_all, 0.0), -1,
                         keepdims=True)
            gc_row = jnp.sum(jnp.where(sub == head, gct_all, 0.0), 0, keepdims=True)
            g_last = gc_row[:, edge:edge + 1]
            ksl = slice((hh // 2) * LANE, (hh // 2 + 1) * LANE)
            vsl = slice(hh * LANE, (hh + 1) * LANE)
            q, k, v = q_ref[rows, ksl], k_ref[rows, ksl], v_ref[rows, vsl]
            gamma = jnp.where(incl, jnp.exp(gc - gc_row), 0.0)
            kb = k * beta
            lmat = jnp.where(strict, _dot_nt(kb, k) * gamma, 0.0)
            a_intra = _dot_nt(q, k) * gamma
            t_inv = _tri_inverse(lmat, _dot)
            e_gc = jnp.exp(gc)
            uw = _dot(t_inv, jnp.concatenate([v * beta, kb * e_gc], -1))
            u_val, w_val = uw[:, :LANE], uw[:, LANE:]
            st = state_ref[hh]
            v_new = u_val - _dot_nt(w_val, st)
            o_ref[rows, vsl] = _dot_nt(q * e_gc, st) + _dot(a_intra, v_new)
            k_dec = k * jnp.exp(g_last - gc)
            state_ref[hh] = st * jnp.exp(g_last) + _dot(v_new.T, k_dec)


def _gdn_layer(geom, hx, mod, w_in, conv_w, a_log_fwd, dt_bias_fwd, a_log_bwd, dt_bias_bwd):
    d = D_MODEL
    b, nt = geom.batch, geom.nt
    hv = GDN_VALUE_HEADS
    wc = w_in[:, :GDN_CONV_CH].astype(BF16)
    wz = w_in[:, GDN_CONV_CH:GDN_CONV_CH + GDN_VALUE_DIM].astype(BF16)
    ws_t = w_in[:, GDN_CONV_CH + GDN_VALUE_DIM:].T
    ws = jnp.concatenate([ws_t.T, jnp.zeros((d, LANE - 4 * hv), F32)], -1).astype(BF16)
    ws_t = ws_t.astype(BF16)
    halo_per_tile = ROW_TILE // HALO
    last_halo = nt // HALO - 1
    prev_spec = pl.BlockSpec((None, HALO, d),
                             lambda bi, t: (bi, jnp.maximum(t * halo_per_tile - 1, 0), 0))
    next_spec = pl.BlockSpec((None, HALO, d),
                             lambda bi, t: (bi, jnp.minimum((t + 1) * halo_per_tile, last_halo), 0))
    q, k, v, z, small, small_t = pl.pallas_call(
        functools.partial(_gdn_proj_kernel, seq=geom.seq, nt=nt),
        grid=(b, geom.tiles),
        in_specs=[prev_spec, geom.row_spec(d), next_spec, geom.mod_spec(0), geom.mod_spec(1),
                  _const_spec(wc.shape), _const_spec(wz.shape), _const_spec(ws.shape),
                  _const_spec(ws_t.shape), _const_spec(conv_w.shape)],
        out_specs=[geom.row_spec(GDN_KEY_DIM), geom.row_spec(GDN_KEY_DIM),
                   geom.row_spec(GDN_VALUE_DIM), geom.row_spec(GDN_VALUE_DIM), geom.row_spec(LANE),
                   pl.BlockSpec((None, 4 * hv, ROW_TILE), lambda bi, t: (bi, 0, t))],
        out_shape=[jax.ShapeDtypeStruct((b, nt, GDN_KEY_DIM), F32),
                   jax.ShapeDtypeStruct((b, nt, GDN_KEY_DIM), F32),
                   jax.ShapeDtypeStruct((b, nt, GDN_VALUE_DIM), F32),
                   jax.ShapeDtypeStruct((b, nt, GDN_VALUE_DIM), F32),
                   jax.ShapeDtypeStruct((b, nt, LANE), F32),
                   jax.ShapeDtypeStruct((b, 4 * hv, nt), F32)],
        compiler_params=_cparams("parallel", "parallel"),
        name="gdn_proj",
    )(hx, hx, hx, mod, mod, wc, wz, ws, ws_t, conv_w)

    heads = 4
    groups = hv // heads

    def scan(rev, a_log, dt_bias, lane_off):
        blk = _chain_block(geom, rev)
        place = lambda p: jnp.zeros((1, LANE), F32).at[0, 2 * hv + lane_off:3 * hv + lane_off].set(p)
        return pl.pallas_call(
            functools.partial(_gdn_scan_kernel, rev=rev, lane_off=lane_off, heads=heads),
            grid=(b, groups, geom.tiles),
            in_specs=[
                pl.BlockSpec((None, ROW_TILE, heads // 2 * LANE), lambda bi, g, j: (bi, blk(j), g)),
                pl.BlockSpec((None, ROW_TILE, heads // 2 * LANE), lambda bi, g, j: (bi, blk(j), g)),
                pl.BlockSpec((None, ROW_TILE, heads * LANE), lambda bi, g, j: (bi, blk(j), g)),
                pl.BlockSpec((None, ROW_TILE, LANE), lambda bi, g, j: (bi, blk(j), 0)),
                pl.BlockSpec((None, 4 * hv, ROW_TILE), lambda bi, g, j: (bi, 0, blk(j))),
                _const_spec((1, LANE)), _const_spec((1, LANE)),
                _const_spec((hv, 1)), _const_spec((hv, 1))],
            out_specs=pl.BlockSpec((None, ROW_TILE, heads * LANE),
                                   lambda bi, g, j: (bi, blk(j), g)),
            out_shape=jax.ShapeDtypeStruct((b, nt, GDN_VALUE_DIM), F32),
            scratch_shapes=[pltpu.VMEM((heads, GDN_HEAD_DIM, GDN_HEAD_DIM), F32)],
            compiler_params=_cparams("parallel", "parallel", "arbitrary"),
            name="gdn_scan_bwd" if rev else "gdn_scan_fwd",
        )(q, k, v, small, small_t, place(a_log), place(dt_bias),
          a_log.reshape(hv, 1), dt_bias.reshape(hv, 1))

    return (scan(False, a_log_fwd, dt_bias_fwd, 0), scan(True, a_log_bwd, dt_bias_bwd, hv), z)


def _residual_norm(h, gate, y, g, b):
    return _layer_norm(ALPHA * h + gate * y, g, b)


def _attn_out_kernel(o_ref, h_ref, gate_ref, w_ref, g_ref, b_ref, out_ref):
    y = jnp.dot(o_ref[...], w_ref[...], preferred_element_type=F32)
    out_ref[...] = _residual_norm(h_ref[...], gate_ref[...], y, g_ref[...], b_ref[...])


def _gated_out_kernel(of_ref, ob_ref, z_ref, h_ref, gate_ref, ng_ref, w_ref, g_ref, b_ref,
                      out_ref, *, head_dim):
    o = of_ref[...] + ob_ref[...]
    ng = ng_ref[...]
    parts = [_rms(o[:, c:c + head_dim], ng) for c in range(0, o.shape[1], head_dim)]
    x = jnp.concatenate(parts, -1) * _silu(z_ref[...])
    y = _dot(x, w_ref[...])
    out_ref[...] = _residual_norm(h_ref[...], gate_ref[...], y, g_ref[...], b_ref[...])


def _out_call(kernel, geom, tiles, acts, hx, mod, consts, name):
    d = D_MODEL
    in_specs = ([geom.row_spec(a.shape[-1]) for a in acts] + [geom.row_spec(d), geom.mod_spec(2)]
                + [_const_spec(c.shape) for c in consts])
    return pl.pallas_call(
        kernel,
        grid=(geom.batch, tiles),
        in_specs=in_specs,
        out_specs=geom.row_spec(d),
        out_shape=jax.ShapeDtypeStruct((geom.batch, tiles * ROW_TILE, d), F32),
        compiler_params=_cparams("parallel", "parallel"),
        name=name,
    )(*acts, hx, mod, *consts)


def _mlp_kernel(h_ref, sh_ref, sc_ref, gate_ref, w1_ref, w2_ref, g_ref, b_ref, out_ref):
    h = h_ref[...]
    u = (h * (1.0 + sc_ref[...]) + sh_ref[...]).astype(BF16)
    blk = 1024
    y = None
    for c in range(0, D_FF, blk):
        a = jnp.square(jnp.maximum(jnp.dot(u, w1_ref[:, c:c + blk], preferred_element_type=F32), 0.0))
        part = jnp.dot(a.astype(BF16), w2_ref[c:c + blk, :], preferred_element_type=F32)
        y = part if y is None else y + part
    out_ref[...] = _residual_norm(h, gate_ref[...], y, g_ref[...], b_ref[...])


def _mlp(geom, tiles, hx, mod, w1, w2, g, b):
    d = D_MODEL
    return pl.pallas_call(
        _mlp_kernel,
        grid=(geom.batch, tiles),
        in_specs=[geom.row_spec(d), geom.mod_spec(3), geom.mod_spec(4), geom.mod_spec(5),
                  _const_spec(w1.shape), _const_spec(w2.shape), _const_spec((1, d)),
                  _const_spec((1, d))],
        out_specs=geom.row_spec(d),
        out_shape=jax.ShapeDtypeStruct((geom.batch, tiles * ROW_TILE, d), F32),
        compiler_params=_cparams("parallel", "parallel"),
        name="mlp",
    )(hx, mod, mod, mod, w1, w2, g.reshape(1, d), b.reshape(1, d))


def kernel(x, c, ctx, c_ctx, ada_w, ada_b, ln1_g, ln1_b, ln2_g, ln2_b, mlp_w1, mlp_w2, mla_w_in, mla_q_norm, mla_kv_norm, mla_w_qb, mla_w_kvb, mla_w_out, diff_w_in, diff_lambda_q1, diff_lambda_k1, diff_lambda_q2, diff_lambda_k2, diff_subln, diff_w_out, gla_w_in, gla_gate_w_fwd, gla_gate_b_fwd, gla_gate_w_bwd, gla_gate_b_bwd, gla_norm, gla_w_out, gdn_w_in, gdn_conv_w, gdn_a_log_fwd, gdn_dt_bias_fwd, gdn_a_log_bwd, gdn_dt_bias_bwd, gdn_norm, gdn_w_out):
    batch, seq, d = x.shape
    geom = _Geom(batch, seq, ctx.shape[1])
    depth = ada_w.shape[0]
    cond_rows = -(-(batch + 1) // 8) * 8
    s = jnp.concatenate([c, c_ctx[None], jnp.zeros((cond_rows - batch - 1, d), F32)], 0)
    mods = _adaln(s, ada_w, ada_b)
    hx = jnp.concatenate([x, ctx], 1)
    for i in range(depth):
        last = i == depth - 1
        tiles = geom.lat_tiles if last else geom.tiles
        kind, j = i % 4, i // 4
        mod = mods[i].reshape(cond_rows, 1, 6 * d)
        ln = [ln1_g[i].reshape(1, d), ln1_b[i].reshape(1, d)]
        if kind == 0:
            o = _mla_layer(geom, hx, mod, mla_w_in[j], mla_q_norm[j], mla_kv_norm[j], mla_w_qb[j],
                           mla_w_kvb[j])
            hx = _out_call(_attn_out_kernel, geom, tiles, [o], hx, mod,
                           [mla_w_out[j].astype(BF16)] + ln, "mla_out")
        elif kind == 1:
            lambda_init = 0.8 - 0.6 * math.exp(-0.3 * i)
            o = _diff_layer(geom, hx, mod, diff_w_in[j], diff_lambda_q1[j], diff_lambda_k1[j],
                            diff_lambda_q2[j], diff_lambda_k2[j], diff_subln[j], lambda_init)
            hx = _out_call(_attn_out_kernel, geom, tiles, [o], hx, mod,
                           [diff_w_out[j].astype(BF16)] + ln, "diff_out")
        elif kind == 2:
            o_f, o_b, gate = _gla_layer(geom, hx, mod, gla_w_in[j], gla_gate_w_fwd[j],
                                        gla_gate_b_fwd[j], gla_gate_w_bwd[j], gla_gate_b_bwd[j])
            hx = _out_call(functools.partial(_gated_out_kernel, head_dim=GLA_DV), geom, tiles,
                           [o_f, o_b, gate], hx, mod,
                           [gla_norm[j].reshape(1, -1), gla_w_out[j].astype(BF16)] + ln, "gla_out")
        else:
            o_f, o_b, gate = _gdn_layer(geom, hx, mod, gdn_w_in[j], gdn_conv_w[j], gdn_a_log_fwd[j],
                                        gdn_dt_bias_fwd[j], gdn_a_log_bwd[j], gdn_dt_bias_bwd[j])
            hx = _out_call(functools.partial(_gated_out_kernel, head_dim=GDN_HEAD_DIM), geom, tiles,
                           [o_f, o_b, gate], hx, mod,
                           [gdn_norm[j].reshape(1, -1), gdn_w_out[j].astype(BF16)] + ln, "gdn_out")
        hx = _mlp(geom, tiles, hx, mod, mlp_w1[i].astype(BF16), mlp_w2[i].astype(BF16),
                  ln2_g[i], ln2_b[i])
    return hx[:, :seq]
```

```python
import functools
import math

import jax
import jax.numpy as jnp
from jax import lax
from jax.experimental import pallas as pl
from jax.experimental.pallas import tpu as pltpu

F32 = jnp.float32
BF16 = jnp.bfloat16

D_MODEL = 1024
DEPTH = 4
GRID_W = 64
D_FF = 4 * D_MODEL
ALPHA = (2 * DEPTH) ** 0.25
NORM_EPS = 1e-6
ROPE_BASE = 10000.0
CHUNK = 64

MLA_HEADS = 16
MLA_NOPE = 64
MLA_ROPE = 32
MLA_V = 64
MLA_KV_LORA = 256
MLA_Q_LORA = 768

DIFF_HEAD_DIM = 64
DIFF_HEADS = 8

GLA_HEADS = 4
GLA_KEY_DIM = 512
GLA_VALUE_DIM = 1024
GLA_DK = 128
GLA_DV = 256
GLA_GATE_RANK = 16
GLA_GATE_NORM = 16.0

GDN_HEAD_DIM = 128
GDN_KEY_HEADS = 8
GDN_VALUE_HEADS = 16
GDN_KEY_DIM = 1024
GDN_VALUE_DIM = 2048
GDN_CONV = 5
GDN_CONV_CH = 4096

LANE = 128
ROW_TILE = 256
HALO = 8
VMEM_LIMIT = 56 * 1024 * 1024


def _cparams(*sem):
    return pltpu.CompilerParams(dimension_semantics=sem, vmem_limit_bytes=VMEM_LIMIT)


def _const_spec(shape):
    nd = len(shape)
    return pl.BlockSpec(shape, lambda *_: (0,) * nd, pipeline_mode=pl.Buffered(1))


def _dot(a, b):
    return jnp.dot(a.astype(BF16), b.astype(BF16), preferred_element_type=F32)


def _dot_nt(a, b):
    return lax.dot_general(a.astype(BF16), b.astype(BF16), (((1,), (1,)), ((), ())),
                           preferred_element_type=F32)


def _dot_exact(a, b):
    return jnp.dot(a, b, preferred_element_type=F32, precision=lax.Precision.HIGHEST)


def _dot_nt_exact(a, b):
    return lax.dot_general(a, b, (((1,), (1,)), ((), ())), preferred_element_type=F32,
                           precision=lax.Precision.HIGHEST)


def _rms(x, g):
    return x * lax.rsqrt(jnp.mean(jnp.square(x), -1, keepdims=True) + NORM_EPS) * g


def _layer_norm(z, g, b):
    mu = jnp.mean(z, -1, keepdims=True)
    zc = z - mu
    var = jnp.mean(jnp.square(zc), -1, keepdims=True)
    return zc * lax.rsqrt(var + NORM_EPS) * g + b


def _softplus(x):
    return jnp.maximum(x, 0.0) + jnp.log1p(jnp.exp(-jnp.abs(x)))


def _silu(x):
    return x * jax.nn.sigmoid(x)


def _adaln_kernel(s_ref, w_ref, b_ref, o_ref):
    o_ref[...] = _dot(_silu(s_ref[...]), w_ref[...]) + b_ref[...]


def _adaln(s, ada_w, ada_b):
    depth, d, n = ada_w.shape
    rows = s.shape[0]
    tn = 1536
    return pl.pallas_call(
        _adaln_kernel,
        grid=(depth, n // tn),
        in_specs=[pl.BlockSpec((rows, d), lambda l, j: (0, 0)),
                  pl.BlockSpec((None, d, tn), lambda l, j: (l, 0, j)),
                  pl.BlockSpec((None, 1, tn), lambda l, j: (l, 0, j))],
        out_specs=pl.BlockSpec((None, rows, tn), lambda l, j: (l, 0, j)),
        out_shape=jax.ShapeDtypeStruct((depth, rows, n), F32),
        compiler_params=_cparams("parallel", "parallel"),
        name="adaln",
    )(s, ada_w, ada_b.reshape(depth, 1, n))


class _Geom:
    def __init__(self, batch, seq, ctx):
        assert seq % ROW_TILE == 0 and ctx % ROW_TILE == 0
        self.batch, self.seq, self.ctx = batch, seq, ctx
        self.nt = seq + ctx
        self.lat_tiles = seq // ROW_TILE
        self.tiles = self.nt // ROW_TILE

    def mod_spec(self, k):
        lat_tiles, batch = self.lat_tiles, self.batch
        return pl.BlockSpec((None, 1, D_MODEL),
                            lambda b, t: (jnp.where(t < lat_tiles, b, batch), 0, k))

    def row_spec(self, width, tile=ROW_TILE):
        return pl.BlockSpec((None, tile, width), lambda b, t: (b, t, 0))


def _rope_angles(n, dim):
    rows = n // GRID_W
    row = jnp.repeat(jnp.arange(rows, dtype=F32), GRID_W)
    col = jnp.tile(jnp.arange(GRID_W, dtype=F32), rows)
    n_freq = dim // 4
    inv_freq = ROPE_BASE ** (-jnp.arange(n_freq, dtype=F32) / n_freq)
    ang = jnp.concatenate([row[:, None] * inv_freq, col[:, None] * inv_freq], -1)
    return jnp.cos(ang), jnp.sin(ang)


def _rope_tables(geom, dim, layout):
    cos, sin = _rope_angles(geom.seq, dim)
    one, zero = jnp.ones_like(cos), jnp.zeros_like(cos)
    c = jnp.concatenate([one if g == "-" else cos for g in layout], -1)
    sa = jnp.concatenate([-sin if g == "e" else zero for g in layout], -1)
    sb = jnp.concatenate([sin if g == "o" else zero for g in layout], -1)
    pad = lambda t, v: jnp.concatenate([t, jnp.full((geom.ctx, LANE), v, F32)], 0)
    return pad(c, 1.0), pad(sa, 0.0), pad(sb, 0.0)


def _rope(x, c, sa, sb, w):
    return x * c + pltpu.roll(x, LANE - w, 1) * sa + pltpu.roll(x, w, 1) * sb


def _deinterleave(n):
    return list(range(0, n, 2)) + list(range(1, n, 2))


def _mla_proj_kernel(h_ref, sh_ref, sc_ref, win_ref, qn_ref, kvn_ref, wqb_ref, wkvb_ref,
                     c_ref, sa_ref, sb_ref, q_ref, k_ref, v_ref):
    u = h_ref[...] * (1.0 + sc_ref[...]) + sh_ref[...]
    t = _dot(u, win_ref[...])
    cq, ckv, kr = t[:, :MLA_Q_LORA], t[:, MLA_Q_LORA:MLA_Q_LORA + MLA_KV_LORA], t[:, -LANE:]
    q = _dot(_rms(cq, qn_ref[...]), wqb_ref[...])
    kv = _dot(_rms(ckv, kvn_ref[...]), wkvb_ref[...])
    c, sa, sb = c_ref[...], sa_ref[...], sb_ref[...]
    half = MLA_ROPE // 2
    scale = (MLA_NOPE + MLA_ROPE) ** -0.5
    kr = _rope(pltpu.roll(kr, MLA_NOPE, 1), c, sa, sb, half)
    for hd in range(MLA_HEADS):
        sl = slice(hd * LANE, (hd + 1) * LANE)
        q_ref[:, sl] = (_rope(q[:, sl], c, sa, sb, half) * scale).astype(q_ref.dtype)
        k_ref[:, sl] = (kv[:, sl] + kr).astype(k_ref.dtype)
    v_ref[...] = kv[:, MLA_HEADS * LANE:].astype(v_ref.dtype)


def _softmax_pv(s, v):
    m = jnp.max(s, -1, keepdims=True)
    p = jnp.exp(s - m)
    l = jnp.sum(p, -1, keepdims=True)
    return _dot(p, v) / l


def _mla_attn_kernel(q_ref, k_ref, v_ref, o_ref):
    v = v_ref[...]
    outs = []
    for j in range(2):
        sl = slice(j * LANE, (j + 1) * LANE)
        outs.append(_softmax_pv(_dot_nt(q_ref[:, sl], k_ref[:, sl]), v))
    lane = lax.broadcasted_iota(jnp.int32, outs[0].shape, 1)
    o_ref[...] = jnp.where(lane < MLA_V, outs[0], outs[1]).astype(o_ref.dtype)


def _attention(kernel, geom, q, k, v, extra, head_groups, qw, kw, vw, ow, name):
    batch = geom.batch
    extra_specs = [_const_spec(e.shape) for e in extra]
    out_shape = jax.ShapeDtypeStruct((batch, geom.nt, head_groups * ow), BF16)

    def call(q_tiles, q_off, k_rows, k_blk, alias):
        specs = extra_specs + [
            pl.BlockSpec((None, ROW_TILE, qw), lambda b, g, t: (b, t + q_off, g)),
            pl.BlockSpec((None, k_rows, kw), lambda b, g, t: (b, k_blk, g)),
            pl.BlockSpec((None, k_rows, vw), lambda b, g, t: (b, k_blk, g)),
        ]
        args = list(extra) + [q, k, v]
        aliases = {}
        if alias is not None:
            specs.append(pl.BlockSpec(memory_space=pl.ANY))
            args.append(alias)
            aliases = {len(args) - 1: 0}
        body = kernel if alias is None else (lambda *refs: kernel(*refs[:-2], refs[-1]))
        return pl.pallas_call(
            body,
            grid=(batch, head_groups, q_tiles),
            in_specs=specs,
            out_specs=pl.BlockSpec((None, ROW_TILE, ow), lambda b, g, t: (b, t + q_off, g)),
            out_shape=out_shape,
            input_output_aliases=aliases,
            compiler_params=_cparams("parallel", "parallel", "arbitrary"),
            name=name,
        )(*args)

    o = call(geom.lat_tiles, 0, geom.nt, 0, None)
    ctx_tiles = geom.tiles - geom.lat_tiles
    return call(ctx_tiles, geom.lat_tiles, geom.ctx, geom.seq // geom.ctx, o)


def _mla_layer(geom, hx, mod, w_in, q_norm, kv_norm, w_qb, w_kvb):
    d = D_MODEL
    perm = jnp.array(_deinterleave(MLA_ROPE))
    w_in_p = jnp.concatenate(
        [w_in[:, :MLA_Q_LORA + MLA_KV_LORA], w_in[:, MLA_Q_LORA + MLA_KV_LORA:][:, perm],
         jnp.zeros((d, LANE - MLA_ROPE), F32)], -1).astype(BF16)
    wq = w_qb.reshape(MLA_Q_LORA, MLA_HEADS, MLA_NOPE + MLA_ROPE)
    wq = jnp.concatenate([wq[..., :MLA_NOPE], wq[..., MLA_NOPE:][..., perm],
                          jnp.zeros((MLA_Q_LORA, MLA_HEADS, LANE - MLA_NOPE - MLA_ROPE), F32)], -1)
    wq = wq.reshape(MLA_Q_LORA, MLA_HEADS * LANE).astype(BF16)
    wkv = w_kvb.reshape(MLA_KV_LORA, MLA_HEADS, MLA_NOPE + MLA_V)
    wk = jnp.concatenate([wkv[..., :MLA_NOPE],
                          jnp.zeros((MLA_KV_LORA, MLA_HEADS, LANE - MLA_NOPE), F32)], -1)
    wkv = jnp.concatenate([wk.reshape(MLA_KV_LORA, MLA_HEADS * LANE),
                           wkv[..., MLA_NOPE:].reshape(MLA_KV_LORA, MLA_HEADS * MLA_V)], -1)
    wkv = wkv.astype(BF16)
    tables = _rope_tables(geom, MLA_ROPE, ["-"] * 4 + ["e", "o"] + ["-"] * 2)
    tab_spec = pl.BlockSpec((ROW_TILE, LANE), lambda b, t: (t, 0))
    qk_w = MLA_HEADS * LANE
    q, k, v = pl.pallas_call(
        _mla_proj_kernel,
        grid=(geom.batch, geom.tiles),
        in_specs=[geom.row_spec(d), geom.mod_spec(0), geom.mod_spec(1),
                  _const_spec(w_in_p.shape), _const_spec((1, MLA_Q_LORA)),
                  _const_spec((1, MLA_KV_LORA)), _const_spec(wq.shape), _const_spec(wkv.shape),
                  tab_spec, tab_spec, tab_spec],
        out_specs=[geom.row_spec(qk_w), geom.row_spec(qk_w), geom.row_spec(MLA_HEADS * MLA_V)],
        out_shape=[jax.ShapeDtypeStruct((geom.batch, geom.nt, qk_w), BF16),
                   jax.ShapeDtypeStruct((geom.batch, geom.nt, qk_w), BF16),
                   jax.ShapeDtypeStruct((geom.batch, geom.nt, MLA_HEADS * MLA_V), BF16)],
        compiler_params=_cparams("parallel", "parallel"),
        name="mla_proj",
    )(hx, mod, mod, w_in_p, q_norm.reshape(1, -1), kv_norm.reshape(1, -1), wq, wkv, *tables)
    return _attention(_mla_attn_kernel, geom, q, k, v, [], MLA_HEADS // 2,
                      2 * LANE, 2 * LANE, LANE, LANE, "mla_attn")


def _diff_proj_kernel(h_ref, sh_ref, sc_ref, win_ref, c_ref, sa_ref, sb_ref, q_ref, k_ref, v_ref):
    u = h_ref[...] * (1.0 + sc_ref[...]) + sh_ref[...]
    t = _dot(u, win_ref[...])
    c, sa, sb = c_ref[...], sa_ref[...], sb_ref[...]
    half = DIFF_HEAD_DIM // 2
    width = DIFF_HEADS * LANE
    scale = DIFF_HEAD_DIM ** -0.5
    for hd in range(DIFF_HEADS):
        sl = slice(hd * LANE, (hd + 1) * LANE)
        q_ref[:, sl] = (_rope(t[:, sl], c, sa, sb, half) * scale).astype(q_ref.dtype)
        k_ref[:, sl] = _rope(t[:, width + hd * LANE:width + (hd + 1) * LANE],
                             c, sa, sb, half).astype(k_ref.dtype)
    v_ref[...] = t[:, 2 * width:].astype(v_ref.dtype)


def _diff_attn_kernel(lam_ref, subln_ref, q_ref, k_ref, v_ref, o_ref, *, lambda_init):
    lam_p = lam_ref[...]
    lam = (jnp.exp(jnp.sum(lam_p[0:1] * lam_p[1:2], -1, keepdims=True))
           - jnp.exp(jnp.sum(lam_p[2:3] * lam_p[3:4], -1, keepdims=True)) + lambda_init)
    q, k, v = q_ref[...], k_ref[...], v_ref[...]
    lane = lax.broadcasted_iota(jnp.int32, q.shape, 1)
    zero = jnp.zeros_like(q)
    o1 = _softmax_pv(_dot_nt(jnp.where(lane < DIFF_HEAD_DIM, q, zero), k), v)
    o2 = _softmax_pv(_dot_nt(jnp.where(lane < DIFF_HEAD_DIM, zero, q), k), v)
    o = o1 - lam * o2
    o_ref[...] = (_rms(o, subln_ref[...]) * (1.0 - lambda_init)).astype(o_ref.dtype)


def _diff_layer(geom, hx, mod, w_in, lam_q1, lam_k1, lam_q2, lam_k2, subln, lambda_init):
    d = D_MODEL
    hd = DIFF_HEAD_DIM
    half_perm = _deinterleave(hd)
    head_perm = half_perm + [hd + p for p in half_perm]
    width = DIFF_HEADS * LANE
    qk_perm = jnp.array([h * LANE + p for h in range(DIFF_HEADS) for p in head_perm])
    w_in_p = jnp.concatenate([w_in[:, :width][:, qk_perm], w_in[:, width:2 * width][:, qk_perm],
                              w_in[:, 2 * width:]], -1).astype(BF16)
    tables = _rope_tables(geom, hd, ["e", "o", "e", "o"])
    tab_spec = pl.BlockSpec((ROW_TILE, LANE), lambda b, t: (t, 0))
    out = jax.ShapeDtypeStruct((geom.batch, geom.nt, width), BF16)
    q, k, v = pl.pallas_call(
        _diff_proj_kernel,
        grid=(geom.batch, geom.tiles),
        in_specs=[geom.row_spec(d), geom.mod_spec(0), geom.mod_spec(1), _const_spec(w_in_p.shape),
                  tab_spec, tab_spec, tab_spec],
        out_specs=[geom.row_spec(width)] * 3,
        out_shape=[out] * 3,
        compiler_params=_cparams("parallel", "parallel"),
        name="diff_proj",
    )(hx, mod, mod, w_in_p, *tables)
    lam_p = jnp.stack([lam_q1, lam_k1, lam_q2, lam_k2]).astype(F32)
    kern = functools.partial(_diff_attn_kernel, lambda_init=lambda_init)
    return _attention(kern, geom, q, k, v, [lam_p, subln.reshape(1, -1)], DIFF_HEADS,
                      LANE, LANE, LANE, LANE, "diff_attn")


def _chain_block(geom, rev):
    tiles, lat = geom.tiles, geom.lat_tiles
    if rev:
        return lambda j: tiles - 1 - j
    ctx_tiles = tiles - lat
    return lambda j: jnp.where(j < ctx_tiles, j + lat, j - ctx_tiles)


def _tri_mask(n, rev, strict):
    r = lax.broadcasted_iota(jnp.int32, (n, n), 0)
    c = lax.broadcasted_iota(jnp.int32, (n, n), 1)
    if rev:
        r, c = c, r
    return r > c if strict else r >= c


def _gla_proj_kernel(h_ref, sh_ref, sc_ref, win_ref, q_ref, k_ref, v_ref, g_ref, r_ref):
    u = h_ref[...] * (1.0 + sc_ref[...]) + sh_ref[...]
    t = _dot(u, win_ref[...])
    kd, vd = GLA_KEY_DIM, GLA_VALUE_DIM
    q_ref[...] = t[:, :kd] * GLA_DK ** -0.5
    k_ref[...] = t[:, kd:2 * kd]
    v_ref[...] = t[:, 2 * kd:2 * kd + vd].astype(v_ref.dtype)
    g_ref[...] = t[:, 2 * kd + vd:2 * kd + 2 * vd]
    r_ref[...] = t[:, 2 * kd + 2 * vd:]


def _gla_scan_kernel(q_ref, k_ref, v_ref, r_ref, gw_ref, gb_ref, o_ref, st_ref, *, rev):
    @pl.when(pl.program_id(1) == 0)
    def _():
        st_ref[...] = jnp.zeros_like(st_ref)

    la = _dot(r_ref[...], gw_ref[...]) + gb_ref[...]
    la = -_softplus(-la) / GLA_GATE_NORM
    tri = _tri_mask(CHUNK, rev, False)
    tri_f = tri.astype(F32)
    n_chunks = ROW_TILE // CHUNK
    edge = 0 if rev else CHUNK - 1
    for ci in (range(n_chunks - 1, -1, -1) if rev else range(n_chunks)):
        rows = slice(ci * CHUNK, (ci + 1) * CHUNK)
        for hd in range(GLA_HEADS):
            ksl = slice(hd * GLA_DK, (hd + 1) * GLA_DK)
            vsl = slice(hd * GLA_DV, (hd + 1) * GLA_DV)
            cum = _dot_exact(tri_f, la[rows, ksl])
            last = cum[edge:edge + 1]
            q, k, v = q_ref[rows, ksl], k_ref[rows, ksl], v_ref[rows, vsl]
            q_dec = q * jnp.exp(cum)
            k_inv = k * jnp.exp(-cum)
            k_end = k * jnp.exp(last - cum)
            a = jnp.where(tri, _dot_nt(q_dec, k_inv), 0.0)
            st = st_ref[hd]
            o_ref[rows, vsl] = _dot(a, v) + _dot_nt(q_dec, st)
            st_ref[hd] = st * jnp.exp(last) + _dot(v.astype(F32).T, k_end)


def _gla_layer(geom, hx, mod, w_in, gate_w_fwd, gate_b_fwd, gate_w_bwd, gate_b_bwd):
    d = D_MODEL
    kd, vd, rk = GLA_KEY_DIM, GLA_VALUE_DIM, GLA_GATE_RANK
    main = 2 * kd + 2 * vd
    w_in_p = jnp.concatenate([w_in, jnp.zeros((d, LANE - 2 * rk), F32)], -1).astype(BF16)
    b, nt = geom.batch, geom.nt
    q, k, v, g, r = pl.pallas_call(
        _gla_proj_kernel,
        grid=(b, geom.tiles),
        in_specs=[geom.row_spec(d), geom.mod_spec(0), geom.mod_spec(1), _const_spec(w_in_p.shape)],
        out_specs=[geom.row_spec(kd), geom.row_spec(kd), geom.row_spec(vd), geom.row_spec(vd),
                   geom.row_spec(LANE)],
        out_shape=[jax.ShapeDtypeStruct((b, nt, kd), F32), jax.ShapeDtypeStruct((b, nt, kd), F32),
                   jax.ShapeDtypeStruct((b, nt, vd), BF16), jax.ShapeDtypeStruct((b, nt, vd), F32),
                   jax.ShapeDtypeStruct((b, nt, LANE), F32)],
        compiler_params=_cparams("parallel", "parallel"),
        name="gla_proj",
    )(hx, mod, mod, w_in_p)
    assert w_in_p.shape[1] == main + LANE

    def scan(rev, gate_w, gate_b, lane_off):
        gw = jnp.zeros((LANE, kd), F32).at[lane_off:lane_off + rk].set(gate_w).astype(BF16)
        blk = _chain_block(geom, rev)
        row = lambda w: pl.BlockSpec((None, ROW_TILE, w), lambda bi, j: (bi, blk(j), 0))
        return pl.pallas_call(
            functools.partial(_gla_scan_kernel, rev=rev),
            grid=(b, geom.tiles),
            in_specs=[row(kd), row(kd), row(vd), row(LANE), _const_spec(gw.shape),
                      _const_spec((1, kd))],
            out_specs=row(vd),
            out_shape=jax.ShapeDtypeStruct((b, nt, vd), F32),
            scratch_shapes=[pltpu.VMEM((GLA_HEADS, GLA_DV, GLA_DK), F32)],
            compiler_params=_cparams("parallel", "arbitrary"),
            name="gla_scan_bwd" if rev else "gla_scan_fwd",
        )(q, k, v, r, gw, gate_b.reshape(1, kd))

    return scan(False, gate_w_fwd, gate_b_fwd, 0), scan(True, gate_w_bwd, gate_b_bwd, rk), g


def _gdn_proj_kernel(hp_ref, h_ref, hn_ref, sh_ref, sc_ref, wc_ref, wz_ref, ws_ref, wst_ref,
                     cw_ref, q_ref, k_ref, v_ref, z_ref, s_ref, st_ref, *, seq, nt):
    t_idx = pl.program_id(1)
    sc, sh = 1.0 + sc_ref[...], sh_ref[...]
    u = h_ref[...] * sc + sh
    row0 = t_idx * ROW_TILE
    has_prev = jnp.logical_and(row0 != 0, row0 != seq)
    has_next = jnp.logical_and(row0 + ROW_TILE != seq, row0 + ROW_TILE != nt)
    u_prev = jnp.where(has_prev, hp_ref[...] * sc + sh, 0.0)
    u_next = jnp.where(has_next, hn_ref[...] * sc + sh, 0.0)
    u_ext = jnp.concatenate([u_prev, u, u_next], 0).astype(BF16)
    u = u.astype(BF16)
    ext = ROW_TILE + 2 * HALO
    blk = 512
    pad = GDN_CONV // 2
    for cb in range(GDN_CONV_CH // blk):
        cols = slice(cb * blk, (cb + 1) * blk)
        te = jnp.dot(u_ext, wc_ref[:, cols], preferred_element_type=F32)
        acc = None
        for j in range(GDN_CONV):
            shifted = te if j == pad else pltpu.roll(te, (pad - j) % ext, 0)
            term = shifted[HALO:HALO + ROW_TILE] * cw_ref[j:j + 1, cols]
            acc = term if acc is None else acc + term
        y = _silu(acc)
        if cols.start < 2 * GDN_KEY_DIM:
            dst, scale, off = ((q_ref, GDN_HEAD_DIM ** -0.5, 0) if cols.start < GDN_KEY_DIM
                               else (k_ref, 1.0, GDN_KEY_DIM))
            for hh in range(blk // LANE):
                yh = y[:, hh * LANE:(hh + 1) * LANE]
                yh = yh * lax.rsqrt(jnp.sum(jnp.square(yh), -1, keepdims=True) + NORM_EPS)
                c0 = cols.start - off + hh * LANE
                dst[:, c0:c0 + LANE] = yh * scale
        else:
            c0 = cols.start - 2 * GDN_KEY_DIM
            v_ref[:, c0:c0 + blk] = y
    z_ref[...] = jnp.dot(u, wz_ref[...], preferred_element_type=F32)
    s_ref[...] = jnp.dot(u, ws_ref[...], preferred_element_type=F32)
    st_ref[...] = lax.dot_general(wst_ref[...], u, (((1,), (1,)), ((), ())),
                                  preferred_element_type=F32)


def _tri_inverse(lmats):
    n = lmats[0].shape[0]
    r = lax.broadcasted_iota(jnp.int32, (n, n), 0)
    c = lax.broadcasted_iota(jnp.int32, (n, n), 1)
    same16 = (r >> 4) == (c >> 4)
    same32 = (r >> 5) == (c >> 5)
    mid32 = jnp.logical_and(same32, jnp.logical_not(same16))
    eye = (r == c).astype(F32)
    d1 = [jnp.where(same16, l, 0.0) for l in lmats]
    xs = [eye - a for a in d1]
    d1 = [a.astype(BF16) for a in d1]
    d2 = [_dot(a, a).astype(BF16) for a in d1]
    d4 = [_dot(a, a).astype(BF16) for a in d2]
    d8 = [_dot(a, a).astype(BF16) for a in d4]
    for powers in (d2, d4, d8):
        xs = [x + _dot(x, p) for x, p in zip(xs, powers)]
    for off in ([jnp.where(mid32, l, 0.0) for l in lmats],
                [jnp.where(same32, 0.0, l) for l in lmats]):
        xb = [x.astype(BF16) for x in xs]
        ys = [_dot(x, o) for x, o in zip(xb, off)]
        xs = [x - _dot(y, x16) for x, y, x16 in zip(xs, ys, xb)]
    return xs


def _gdn_scan_kernel(q_ref, k_ref, v_ref, s_ref, st_ref, alog_ref, dtb_ref, alogc_ref, dtbc_ref,
                     o_ref, state_ref, *, rev, lane_off, heads):
    @pl.when(pl.program_id(2) == 0)
    def _():
        state_ref[...] = jnp.zeros_like(state_ref)

    grp = pl.program_id(1)
    hv = GDN_VALUE_HEADS
    n = ROW_TILE
    r = lax.broadcasted_iota(jnp.int32, (n, n), 0)
    c = lax.broadcasted_iota(jnp.int32, (n, n), 1)
    same = (r >> 6) == (c >> 6)
    incl = jnp.logical_and(same, (c >= r) if rev else (r >= c))
    incl_f = incl.astype(F32)
    same_f = same.astype(F32)
    off_diag = r != c

    small = s_ref[...]
    small_t = st_ref[...]
    beta_all = jax.nn.sigmoid(small)
    g_all = -jnp.exp(alog_ref[...]) * _softplus(small + dtb_ref[...])
    a_rows = small_t[2 * hv + lane_off:2 * hv + lane_off + hv]
    g_t = -jnp.exp(alogc_ref[...]) * _softplus(a_rows + dtbc_ref[...])
    gc_all = _dot_exact(incl_f, g_all)
    tot_all = _dot_exact(same_f, g_all)
    gct_all = _dot_nt_exact(g_t, incl_f)
    tott_all = _dot_nt_exact(g_t, same_f)
    lane = lax.broadcasted_iota(jnp.int32, (n, LANE), 1)
    sub = lax.broadcasted_iota(jnp.int32, (hv, n), 0)
    hs = range(heads)

    def column(x, idx):
        return jnp.sum(jnp.where(lane == idx, x, 0.0), -1, keepdims=True)

    def row(x, idx):
        return jnp.sum(jnp.where(sub == idx, x, 0.0), 0, keepdims=True)

    head = [grp * heads + hh for hh in hs]
    beta = [column(beta_all, head[hh] + lane_off) for hh in hs]
    gc = [column(gc_all, head[hh] + 2 * hv + lane_off) for hh in hs]
    tot = [column(tot_all, head[hh] + 2 * hv + lane_off) for hh in hs]
    gc_row = [row(gct_all, head[hh]) for hh in hs]
    d_last = [jnp.exp(row(tott_all, head[hh])) for hh in hs]
    ksl = [slice((hh // 2) * LANE, (hh // 2 + 1) * LANE) for hh in hs]
    vsl = [slice(hh * LANE, (hh + 1) * LANE) for hh in hs]
    e_gc = [jnp.exp(g) for g in gc]

    gamma = [jnp.where(incl, jnp.exp(gc[hh] - gc_row[hh]), 0.0) for hh in hs]
    kb = [k_ref[:, ksl[hh]] * beta[hh] for hh in hs]
    lmat = [jnp.where(off_diag, _dot_nt(kb[hh], k_ref[:, ksl[hh]]) * gamma[hh], 0.0) for hh in hs]
    a_intra = [(_dot_nt(q_ref[:, ksl[hh]], k_ref[:, ksl[hh]]) * gamma[hh]).astype(BF16) for hh in hs]
    t_inv = _tri_inverse(lmat)
    uw = [_dot(t_inv[hh], jnp.concatenate([v_ref[:, vsl[hh]] * beta[hh], kb[hh] * e_gc[hh]], -1))
          for hh in hs]
    q_dec = [(q_ref[:, ksl[hh]] * e_gc[hh]).astype(BF16) for hh in hs]
    k_dec = [(k_ref[:, ksl[hh]] * jnp.exp(tot[hh] - gc[hh])).astype(BF16) for hh in hs]

    n_chunks = ROW_TILE // CHUNK
    st = [state_ref[hh] for hh in hs]
    zeros = jnp.zeros((CHUNK, LANE), F32)
    for ci in (range(n_chunks - 1, -1, -1) if rev else range(n_chunks)):
        rows = slice(ci * CHUNK, (ci + 1) * CHUNK)
        pair = slice(ci // 2 * LANE, (ci // 2 + 1) * LANE)
        st16 = [s.astype(BF16) for s in st]
        v_new = [uw[hh][rows, :LANE] - _dot_nt(uw[hh][rows, LANE:], st16[hh]) for hh in hs]
        for hh in hs:
            v_pair = jnp.concatenate([v_new[hh], zeros] if ci % 2 == 0 else [zeros, v_new[hh]], 0)
            o_ref[rows, vsl[hh]] = (_dot_nt(q_dec[hh][rows], st16[hh])
                                    + _dot(a_intra[hh][rows, pair], v_pair))
        st = [st[hh] * d_last[hh][:, ci * CHUNK:ci * CHUNK + 1] + _dot(v_new[hh].T, k_dec[hh][rows])
              for hh in hs]
    for hh in hs:
        state_ref[hh] = st[hh]


def _gdn_layer(geom, hx, mod, w_in, conv_w, a_log_fwd, dt_bias_fwd, a_log_bwd, dt_bias_bwd):
    d = D_MODEL
    b, nt = geom.batch, geom.nt
    hv = GDN_VALUE_HEADS
    wc = w_in[:, :GDN_CONV_CH].astype(BF16)
    wz = w_in[:, GDN_CONV_CH:GDN_CONV_CH + GDN_VALUE_DIM].astype(BF16)
    ws_t = w_in[:, GDN_CONV_CH + GDN_VALUE_DIM:].T
    ws = jnp.concatenate([ws_t.T, jnp.zeros((d, LANE - 4 * hv), F32)], -1).astype(BF16)
    ws_t = ws_t.astype(BF16)
    halo_per_tile = ROW_TILE // HALO
    last_halo = nt // HALO - 1
    prev_spec = pl.BlockSpec((None, HALO, d),
                             lambda bi, t: (bi, jnp.maximum(t * halo_per_tile - 1, 0), 0))
    next_spec = pl.BlockSpec((None, HALO, d),
                             lambda bi, t: (bi, jnp.minimum((t + 1) * halo_per_tile, last_halo), 0))
    q, k, v, z, small, small_t = pl.pallas_call(
        functools.partial(_gdn_proj_kernel, seq=geom.seq, nt=nt),
        grid=(b, geom.tiles),
        in_specs=[prev_spec, geom.row_spec(d), next_spec, geom.mod_spec(0), geom.mod_spec(1),
                  _const_spec(wc.shape), _const_spec(wz.shape), _const_spec(ws.shape),
                  _const_spec(ws_t.shape), _const_spec(conv_w.shape)],
        out_specs=[geom.row_spec(GDN_KEY_DIM), geom.row_spec(GDN_KEY_DIM),
                   geom.row_spec(GDN_VALUE_DIM), geom.row_spec(GDN_VALUE_DIM), geom.row_spec(LANE),
                   pl.BlockSpec((None, 4 * hv, ROW_TILE), lambda bi, t: (bi, 0, t))],
        out_shape=[jax.ShapeDtypeStruct((b, nt, GDN_KEY_DIM), F32),
                   jax.ShapeDtypeStruct((b, nt, GDN_KEY_DIM), F32),
                   jax.ShapeDtypeStruct((b, nt, GDN_VALUE_DIM), F32),
                   jax.ShapeDtypeStruct((b, nt, GDN_VALUE_DIM), F32),
                   jax.ShapeDtypeStruct((b, nt, LANE), F32),
                   jax.ShapeDtypeStruct((b, 4 * hv, nt), F32)],
        compiler_params=_cparams("parallel", "parallel"),
        name="gdn_proj",
    )(hx, hx, hx, mod, mod, wc, wz, ws, ws_t, conv_w)

    heads = 16
    groups = hv // heads

    def scan(rev, a_log, dt_bias, lane_off):
        blk = _chain_block(geom, rev)
        place = lambda p: jnp.zeros((1, LANE), F32).at[0, 2 * hv + lane_off:3 * hv + lane_off].set(p)
        return pl.pallas_call(
            functools.partial(_gdn_scan_kernel, rev=rev, lane_off=lane_off, heads=heads),
            grid=(b, groups, geom.tiles),
            in_specs=[
                pl.BlockSpec((None, ROW_TILE, heads // 2 * LANE), lambda bi, g, j: (bi, blk(j), g)),
                pl.BlockSpec((None, ROW_TILE, heads // 2 * LANE), lambda bi, g, j: (bi, blk(j), g)),
                pl.BlockSpec((None, ROW_TILE, heads * LANE), lambda bi, g, j: (bi, blk(j), g)),
                pl.BlockSpec((None, ROW_TILE, LANE), lambda bi, g, j: (bi, blk(j), 0)),
                pl.BlockSpec((None, 4 * hv, ROW_TILE), lambda bi, g, j: (bi, 0, blk(j))),
                _const_spec((1, LANE)), _const_spec((1, LANE)),
                _const_spec((hv, 1)), _const_spec((hv, 1))],
            out_specs=pl.BlockSpec((None, ROW_TILE, heads * LANE),
                                   lambda bi, g, j: (bi, blk(j), g)),
            out_shape=jax.ShapeDtypeStruct((b, nt, GDN_VALUE_DIM), F32),
            scratch_shapes=[pltpu.VMEM((heads, GDN_HEAD_DIM, GDN_HEAD_DIM), F32)],
            compiler_params=_cparams("parallel", "parallel", "arbitrary"),
            name="gdn_scan_bwd" if rev else "gdn_scan_fwd",
        )(q, k, v, small, small_t, place(a_log), place(dt_bias),
          a_log.reshape(hv, 1), dt_bias.reshape(hv, 1))

    return (scan(False, a_log_fwd, dt_bias_fwd, 0), scan(True, a_log_bwd, dt_bias_bwd, hv), z)


def _residual_norm(h, gate, y, g, b):
    return _layer_norm(ALPHA * h + gate * y, g, b)


def _attn_out_kernel(o_ref, h_ref, gate_ref, w_ref, g_ref, b_ref, out_ref):
    y = jnp.dot(o_ref[...], w_ref[...], preferred_element_type=F32)
    out_ref[...] = _residual_norm(h_ref[...], gate_ref[...], y, g_ref[...], b_ref[...])


def _gated_out_kernel(of_ref, ob_ref, z_ref, h_ref, gate_ref, ng_ref, w_ref, g_ref, b_ref,
                      out_ref, *, head_dim):
    o = of_ref[...] + ob_ref[...]
    ng = ng_ref[...]
    parts = [_rms(o[:, c:c + head_dim], ng) for c in range(0, o.shape[1], head_dim)]
    x = jnp.concatenate(parts, -1) * _silu(z_ref[...])
    y = _dot(x, w_ref[...])
    out_ref[...] = _residual_norm(h_ref[...], gate_ref[...], y, g_ref[...], b_ref[...])


def _out_call(kernel, geom, tiles, acts, hx, mod, consts, name):
    d = D_MODEL
    in_specs = ([geom.row_spec(a.shape[-1]) for a in acts] + [geom.row_spec(d), geom.mod_spec(2)]
                + [_const_spec(c.shape) for c in consts])
    return pl.pallas_call(
        kernel,
        grid=(geom.batch, tiles),
        in_specs=in_specs,
        out_specs=geom.row_spec(d),
        out_shape=jax.ShapeDtypeStruct((geom.batch, tiles * ROW_TILE, d), F32),
        compiler_params=_cparams("parallel", "parallel"),
        name=name,
    )(*acts, hx, mod, *consts)


def _mlp_kernel(h_ref, sh_ref, sc_ref, gate_ref, w1_ref, w2_ref, g_ref, b_ref, out_ref):
    h = h_ref[...]
    u = (h * (1.0 + sc_ref[...]) + sh_ref[...]).astype(BF16)
    blk = 1024
    y = None
    for c in range(0, D_FF, blk):
        a = jnp.square(jnp.maximum(jnp.dot(u, w1_ref[:, c:c + blk], preferred_element_type=F32), 0.0))
        part = jnp.dot(a.astype(BF16), w2_ref[c:c + blk, :], preferred_element_type=F32)
        y = part if y is None else y + part
    out_ref[...] = _residual_norm(h, gate_ref[...], y, g_ref[...], b_ref[...])


def _mlp(geom, tiles, hx, mod, w1, w2, g, b):
    d = D_MODEL
    return pl.pallas_call(
        _mlp_kernel,
        grid=(geom.batch, tiles),
        in_specs=[geom.row_spec(d), geom.mod_spec(3), geom.mod_spec(4), geom.mod_spec(5),
                  _const_spec(w1.shape), _const_spec(w2.shape), _const_spec((1, d)),
                  _const_spec((1, d))],
        out_specs=geom.row_spec(d),
        out_shape=jax.ShapeDtypeStruct((geom.batch, tiles * ROW_TILE, d), F32),
        compiler_params=_cparams("parallel", "parallel"),
        name="mlp",
    )(hx, mod, mod, mod, w1, w2, g.reshape(1, d), b.reshape(1, d))


def kernel(x, c, ctx, c_ctx, ada_w, ada_b, ln1_g, ln1_b, ln2_g, ln2_b, mlp_w1, mlp_w2, mla_w_in, mla_q_norm, mla_kv_norm, mla_w_qb, mla_w_kvb, mla_w_out, diff_w_in, diff_lambda_q1, diff_lambda_k1, diff_lambda_q2, diff_lambda_k2, diff_subln, diff_w_out, gla_w_in, gla_gate_w_fwd, gla_gate_b_fwd, gla_gate_w_bwd, gla_gate_b_bwd, gla_norm, gla_w_out, gdn_w_in, gdn_conv_w, gdn_a_log_fwd, gdn_dt_bias_fwd, gdn_a_log_bwd, gdn_dt_bias_bwd, gdn_norm, gdn_w_out):
    batch, seq, d = x.shape
    geom = _Geom(batch, seq, ctx.shape[1])
    depth = ada_w.shape[0]
    cond_rows = -(-(batch + 1) // 8) * 8
    s = jnp.concatenate([c, c_ctx[None], jnp.zeros((cond_rows - batch - 1, d), F32)], 0)
    mods = _adaln(s, ada_w, ada_b)
    hx = jnp.concatenate([x, ctx], 1)
    for i in range(depth):
        last = i == depth - 1
        tiles = geom.lat_tiles if last else geom.tiles
        kind, j = i % 4, i // 4
        mod = mods[i].reshape(cond_rows, 1, 6 * d)
        ln = [ln1_g[i].reshape(1, d), ln1_b[i].reshape(1, d)]
        if kind == 0:
            o = _mla_layer(geom, hx, mod, mla_w_in[j], mla_q_norm[j], mla_kv_norm[j], mla_w_qb[j],
                           mla_w_kvb[j])
            hx = _out_call(_attn_out_kernel, geom, tiles, [o], hx, mod,
                           [mla_w_out[j].astype(BF16)] + ln, "mla_out")
        elif kind == 1:
            lambda_init = 0.8 - 0.6 * math.exp(-0.3 * i)
            o = _diff_layer(geom, hx, mod, diff_w_in[j], diff_lambda_q1[j], diff_lambda_k1[j],
                            diff_lambda_q2[j], diff_lambda_k2[j], diff_subln[j], lambda_init)
            hx = _out_call(_attn_out_kernel, geom, tiles, [o], hx, mod,
                           [diff_w_out[j].astype(BF16)] + ln, "diff_out")
        elif kind == 2:
            o_f, o_b, gate = _gla_layer(geom, hx, mod, gla_w_in[j], gla_gate_w_fwd[j],
                                        gla_gate_b_fwd[j], gla_gate_w_bwd[j], gla_gate_b_bwd[j])
            hx = _out_call(functools.partial(_gated_out_kernel, head_dim=GLA_DV), geom, tiles,
                           [o_f, o_b, gate], hx, mod,
                           [gla_norm[j].reshape(1, -1), gla_w_out[j].astype(BF16)] + ln, "gla_out")
        else:
            o_f, o_b, gate = _gdn_layer(geom, hx, mod, gdn_w_in[j], gdn_conv_w[j], gdn_a_log_fwd[j],
                                        gdn_dt_bias_fwd[j], gdn_a_log_bwd[j], gdn_dt_bias_bwd[j])
            hx = _out_call(functools.partial(_gated_out_kernel, head_dim=GDN_HEAD_DIM), geom, tiles,
                           [o_f, o_b, gate], hx, mod,
                           [gdn_norm[j].reshape(1, -1), gdn_w_out[j].astype(BF16)] + ln, "gdn_out")
        hx = _mlp(geom, tiles, hx, mod, mlp_w1[i].astype(BF16), mlp_w2[i].astype(BF16),
                  ln2_g[i], ln2_b[i])
    return hx[:, :seq]
```

```python
import functools
import math

import jax
import jax.numpy as jnp
from jax import lax
from jax.experimental import pallas as pl
from jax.experimental.pallas import tpu as pltpu

F32 = jnp.float32
BF16 = jnp.bfloat16

D_MODEL = 1024
DEPTH = 4
GRID_W = 64
D_FF = 4 * D_MODEL
ALPHA = (2 * DEPTH) ** 0.25
NORM_EPS = 1e-6
ROPE_BASE = 10000.0
CHUNK = 64

MLA_HEADS = 16
MLA_NOPE = 64
MLA_ROPE = 32
MLA_V = 64
MLA_KV_LORA = 256
MLA_Q_LORA = 768

DIFF_HEAD_DIM = 64
DIFF_HEADS = 8

GLA_HEADS = 4
GLA_KEY_DIM = 512
GLA_VALUE_DIM = 1024
GLA_DK = 128
GLA_DV = 256
GLA_GATE_RANK = 16
GLA_GATE_NORM = 16.0

GDN_HEAD_DIM = 128
GDN_KEY_HEADS = 8
GDN_VALUE_HEADS = 16
GDN_KEY_DIM = 1024
GDN_VALUE_DIM = 2048
GDN_CONV = 5
GDN_CONV_CH = 4096

LANE = 128
ROW_TILE = 256
HALO = 8
V_PAD = 16
KEY_BLOCK = 256
LOG2E = math.log2(math.e)
VMEM_LIMIT = 56 * 1024 * 1024


def _cparams(*sem):
    return pltpu.CompilerParams(dimension_semantics=sem, vmem_limit_bytes=VMEM_LIMIT)


def _const_spec(shape):
    nd = len(shape)
    return pl.BlockSpec(shape, lambda *_: (0,) * nd, pipeline_mode=pl.Buffered(1))


def _dot(a, b):
    return jnp.dot(a.astype(BF16), b.astype(BF16), preferred_element_type=F32)


def _dot_nt(a, b):
    return lax.dot_general(a.astype(BF16), b.astype(BF16), (((1,), (1,)), ((), ())),
                           preferred_element_type=F32)


def _dot_exact(a, b):
    return jnp.dot(a, b, preferred_element_type=F32, precision=lax.Precision.HIGHEST)


def _dot_nt_exact(a, b):
    return lax.dot_general(a, b, (((1,), (1,)), ((), ())), preferred_element_type=F32,
                           precision=lax.Precision.HIGHEST)


def _rms(x, g):
    return x * lax.rsqrt(jnp.mean(jnp.square(x), -1, keepdims=True) + NORM_EPS) * g


def _layer_norm(z, g, b):
    mu = jnp.mean(z, -1, keepdims=True)
    zc = z - mu
    var = jnp.mean(jnp.square(zc), -1, keepdims=True)
    return zc * lax.rsqrt(var + NORM_EPS) * g + b


def _softplus(x):
    return jnp.maximum(x, 0.0) + jnp.log1p(jnp.exp(-jnp.abs(x)))


def _silu(x):
    return x * jax.nn.sigmoid(x)


def _adaln_kernel(s_ref, w_ref, b_ref, o_ref):
    o_ref[...] = _dot(_silu(s_ref[...]), w_ref[...]) + b_ref[...]


def _adaln(s, ada_w, ada_b):
    depth, d, n = ada_w.shape
    rows = s.shape[0]
    tn = 1536
    return pl.pallas_call(
        _adaln_kernel,
        grid=(depth, n // tn),
        in_specs=[pl.BlockSpec((rows, d), lambda l, j: (0, 0)),
                  pl.BlockSpec((None, d, tn), lambda l, j: (l, 0, j)),
                  pl.BlockSpec((None, 1, tn), lambda l, j: (l, 0, j))],
        out_specs=pl.BlockSpec((None, rows, tn), lambda l, j: (l, 0, j)),
        out_shape=jax.ShapeDtypeStruct((depth, rows, n), F32),
        compiler_params=_cparams("parallel", "parallel"),
        name="adaln",
    )(s, ada_w, ada_b.reshape(depth, 1, n))


class _Geom:
    def __init__(self, batch, seq, ctx):
        assert seq % ROW_TILE == 0 and ctx % ROW_TILE == 0
        self.batch, self.seq, self.ctx = batch, seq, ctx
        self.nt = seq + ctx
        self.lat_tiles = seq // ROW_TILE
        self.tiles = self.nt // ROW_TILE

    def mod_spec(self, k):
        lat_tiles, batch = self.lat_tiles, self.batch
        return pl.BlockSpec((None, 1, D_MODEL),
                            lambda b, t: (jnp.where(t < lat_tiles, b, batch), 0, k))

    def row_spec(self, width, tile=ROW_TILE):
        return pl.BlockSpec((None, tile, width), lambda b, t: (b, t, 0))


def _rope_angles(n, dim):
    rows = n // GRID_W
    row = jnp.repeat(jnp.arange(rows, dtype=F32), GRID_W)
    col = jnp.tile(jnp.arange(GRID_W, dtype=F32), rows)
    n_freq = dim // 4
    inv_freq = ROPE_BASE ** (-jnp.arange(n_freq, dtype=F32) / n_freq)
    ang = jnp.concatenate([row[:, None] * inv_freq, col[:, None] * inv_freq], -1)
    return jnp.cos(ang), jnp.sin(ang)


def _rope_tables(geom, dim, layout):
    cos, sin = _rope_angles(geom.seq, dim)
    one, zero = jnp.ones_like(cos), jnp.zeros_like(cos)
    c = jnp.concatenate([one if g == "-" else cos for g in layout], -1)
    sa = jnp.concatenate([-sin if g == "e" else zero for g in layout], -1)
    sb = jnp.concatenate([sin if g == "o" else zero for g in layout], -1)
    pad = lambda t, v: jnp.concatenate([t, jnp.full((geom.ctx, LANE), v, F32)], 0)
    return pad(c, 1.0), pad(sa, 0.0), pad(sb, 0.0)


def _rope(x, c, sa, sb, w):
    return x * c + pltpu.roll(x, LANE - w, 1) * sa + pltpu.roll(x, w, 1) * sb


def _deinterleave(n):
    return list(range(0, n, 2)) + list(range(1, n, 2))


def _store_values_t(vt_ref, vt, heads, dv):
    sub = lax.broadcasted_iota(jnp.int32, (V_PAD, vt.shape[1]), 0)
    ones = (sub == 0).astype(vt_ref.dtype)
    for hd in range(heads):
        r0 = hd * (dv + V_PAD)
        vt_ref[r0:r0 + dv, :] = vt[hd * dv:(hd + 1) * dv].astype(vt_ref.dtype)
        vt_ref[r0 + dv:r0 + dv + V_PAD, :] = ones


def _mla_proj_kernel(h_ref, sh_ref, sc_ref, win_ref, qn_ref, kvn_ref, wqb_ref, wk_ref, wvt_ref,
                     c_ref, sa_ref, sb_ref, q_ref, k_ref, vt_ref):
    u = h_ref[...] * (1.0 + sc_ref[...]) + sh_ref[...]
    t = _dot(u, win_ref[...])
    cq, ckv, kr = t[:, :MLA_Q_LORA], t[:, MLA_Q_LORA:MLA_Q_LORA + MLA_KV_LORA], t[:, -LANE:]
    q = _dot(_rms(cq, qn_ref[...]), wqb_ref[...])
    kvn = _rms(ckv, kvn_ref[...]).astype(BF16)
    k_nope = jnp.dot(kvn, wk_ref[...], preferred_element_type=F32)
    c, sa, sb = c_ref[...], sa_ref[...], sb_ref[...]
    half = MLA_ROPE // 2
    scale = (MLA_NOPE + MLA_ROPE) ** -0.5 * LOG2E
    kr = _rope(pltpu.roll(kr, MLA_NOPE, 1), c, sa, sb, half)
    for hd in range(MLA_HEADS):
        sl = slice(hd * LANE, (hd + 1) * LANE)
        q_ref[:, sl] = (_rope(q[:, sl], c, sa, sb, half) * scale).astype(q_ref.dtype)
        k_ref[:, sl] = (k_nope[:, sl] + kr).astype(k_ref.dtype)
    _store_values_t(vt_ref, _dot_nt(wvt_ref[...], kvn), MLA_HEADS, MLA_V)


def _softmax_t(keys, qs, k_blocks, vt_blocks):
    hs = range(len(qs))
    m, acc = [None] * len(qs), [None] * len(qs)
    blocks = [slice(k0, k0 + KEY_BLOCK) for k0 in range(keys.start, keys.stop, KEY_BLOCK)]
    scores = lambda ks: [_dot_nt(k_blocks[i](ks), qs[i]) for i in hs]
    st_next = scores(blocks[0])
    for bi, ks in enumerate(blocks):
        k0 = ks.start
        st, st_next = st_next, (scores(blocks[bi + 1]) if bi + 1 < len(blocks) else None)
        top = [jnp.max(s, 0, keepdims=True) for s in st]
        if k0 == keys.start:
            m = top
            acc = [jnp.dot(vt_blocks[i](ks), jnp.exp2(st[i] - m[i]).astype(BF16),
                           preferred_element_type=F32) for i in hs]
        else:
            m_new = [jnp.maximum(m[i], top[i]) for i in hs]
            acc = [acc[i] * jnp.exp2(m[i] - m_new[i])
                   + jnp.dot(vt_blocks[i](ks), jnp.exp2(st[i] - m_new[i]).astype(BF16),
                             preferred_element_type=F32) for i in hs]
            m = m_new
    dv = acc[0].shape[0] - V_PAD
    return [a[:dv] / a[dv:dv + 1] for a in acc]


def _key_ranges(body, lat_tiles, seq, nt):
    t = pl.program_id(2)
    pl.when(t < lat_tiles)(lambda: body(slice(0, nt)))
    pl.when(t >= lat_tiles)(lambda: body(slice(seq, nt)))


def _mla_attn_kernel(q_ref, k_ref, vt_ref, o_ref, *, lat_tiles, seq):
    rows = MLA_V + V_PAD
    heads = q_ref.shape[1] // LANE
    lanes = [slice(j * LANE, (j + 1) * LANE) for j in range(heads)]

    def attend(keys):
        outs = _softmax_t(keys, [q_ref[:, sl] for sl in lanes],
                          [lambda ks, sl=sl: k_ref[ks, sl] for sl in lanes],
                          [lambda ks, j=j: vt_ref[j * rows:(j + 1) * rows, ks] for j in range(heads)])
        for j in range(0, heads, 2):
            o_ref[:, j // 2 * LANE:(j // 2 + 1) * LANE] = jnp.concatenate(
                outs[j:j + 2], 0).T.astype(o_ref.dtype)

    _key_ranges(attend, lat_tiles, seq, k_ref.shape[0])


def _attention(kernel, geom, q, k, vt, extra, head_groups, qkw, vrows, ow, name):
    extra_specs = [_const_spec(e.shape) for e in extra]
    return pl.pallas_call(
        functools.partial(kernel, lat_tiles=geom.lat_tiles, seq=geom.seq),
        grid=(geom.batch, head_groups, geom.tiles),
        in_specs=extra_specs + [
            pl.BlockSpec((None, ROW_TILE, qkw), lambda b, g, t: (b, t, g)),
            pl.BlockSpec((None, geom.nt, qkw), lambda b, g, t: (b, 0, g)),
            pl.BlockSpec((None, vrows, geom.nt), lambda b, g, t: (b, g, 0))],
        out_specs=pl.BlockSpec((None, ROW_TILE, ow), lambda b, g, t: (b, t, g)),
        out_shape=jax.ShapeDtypeStruct((geom.batch, geom.nt, head_groups * ow), BF16),
        compiler_params=_cparams("parallel", "parallel", "arbitrary"),
        name=name,
    )(*extra, q, k, vt)


def _mla_layer(geom, hx, mod, w_in, q_norm, kv_norm, w_qb, w_kvb):
    d = D_MODEL
    perm = jnp.array(_deinterleave(MLA_ROPE))
    w_in_p = jnp.concatenate(
        [w_in[:, :MLA_Q_LORA + MLA_KV_LORA], w_in[:, MLA_Q_LORA + MLA_KV_LORA:][:, perm],
         jnp.zeros((d, LANE - MLA_ROPE), F32)], -1).astype(BF16)
    wq = w_qb.reshape(MLA_Q_LORA, MLA_HEADS, MLA_NOPE + MLA_ROPE)
    wq = jnp.concatenate([wq[..., :MLA_NOPE], wq[..., MLA_NOPE:][..., perm],
                          jnp.zeros((MLA_Q_LORA, MLA_HEADS, LANE - MLA_NOPE - MLA_ROPE), F32)], -1)
    wq = wq.reshape(MLA_Q_LORA, MLA_HEADS * LANE).astype(BF16)
    wkv = w_kvb.reshape(MLA_KV_LORA, MLA_HEADS, MLA_NOPE + MLA_V)
    wk = jnp.concatenate([wkv[..., :MLA_NOPE],
                          jnp.zeros((MLA_KV_LORA, MLA_HEADS, LANE - MLA_NOPE), F32)], -1)
    wk = wk.reshape(MLA_KV_LORA, MLA_HEADS * LANE).astype(BF16)
    wvt = wkv[..., MLA_NOPE:].reshape(MLA_KV_LORA, MLA_HEADS * MLA_V).T.astype(BF16)
    tables = _rope_tables(geom, MLA_ROPE, ["-"] * 4 + ["e", "o"] + ["-"] * 2)
    tab_spec = pl.BlockSpec((ROW_TILE, LANE), lambda b, t: (t, 0))
    qk_w = MLA_HEADS * LANE
    vrows = MLA_HEADS * (MLA_V + V_PAD)
    q, k, vt = pl.pallas_call(
        _mla_proj_kernel,
        grid=(geom.batch, geom.tiles),
        in_specs=[geom.row_spec(d), geom.mod_spec(0), geom.mod_spec(1),
                  _const_spec(w_in_p.shape), _const_spec((1, MLA_Q_LORA)),
                  _const_spec((1, MLA_KV_LORA)), _const_spec(wq.shape), _const_spec(wk.shape),
                  _const_spec(wvt.shape), tab_spec, tab_spec, tab_spec],
        out_specs=[geom.row_spec(qk_w), geom.row_spec(qk_w),
                   pl.BlockSpec((None, vrows, ROW_TILE), lambda b, t: (b, 0, t))],
        out_shape=[jax.ShapeDtypeStruct((geom.batch, geom.nt, qk_w), BF16),
                   jax.ShapeDtypeStruct((geom.batch, geom.nt, qk_w), BF16),
                   jax.ShapeDtypeStruct((geom.batch, vrows, geom.nt), BF16)],
        compiler_params=_cparams("parallel", "parallel"),
        name="mla_proj",
    )(hx, mod, mod, w_in_p, q_norm.reshape(1, -1), kv_norm.reshape(1, -1), wq, wk, wvt, *tables)
    hps = 4
    return _attention(_mla_attn_kernel, geom, q, k, vt, [], MLA_HEADS // hps, hps * LANE,
                      hps * (MLA_V + V_PAD), hps * MLA_V, "mla_attn")


def _diff_proj_kernel(h_ref, sh_ref, sc_ref, wqk_ref, wvt_ref, c_ref, sa_ref, sb_ref,
                      q_ref, k_ref, vt_ref):
    u = (h_ref[...] * (1.0 + sc_ref[...]) + sh_ref[...]).astype(BF16)
    t = jnp.dot(u, wqk_ref[...], preferred_element_type=F32)
    c, sa, sb = c_ref[...], sa_ref[...], sb_ref[...]
    half = DIFF_HEAD_DIM // 2
    width = DIFF_HEADS * LANE
    scale = DIFF_HEAD_DIM ** -0.5 * LOG2E
    for hd in range(DIFF_HEADS):
        sl = slice(hd * LANE, (hd + 1) * LANE)
        q_ref[:, sl] = (_rope(t[:, sl], c, sa, sb, half) * scale).astype(q_ref.dtype)
        k_ref[:, sl] = _rope(t[:, width + hd * LANE:width + (hd + 1) * LANE],
                             c, sa, sb, half).astype(k_ref.dtype)
    _store_values_t(vt_ref, _dot_nt(wvt_ref[...], u), DIFF_HEADS, 2 * DIFF_HEAD_DIM)


def _diff_attn_kernel(lam_ref, subln_ref, q_ref, k_ref, vt_ref, o_ref, *, lambda_init,
                      lat_tiles, seq):
    lam_p = lam_ref[...]
    lam = (jnp.exp(jnp.sum(lam_p[0:1] * lam_p[1:2], -1, keepdims=True))
           - jnp.exp(jnp.sum(lam_p[2:3] * lam_p[3:4], -1, keepdims=True)) + lambda_init)
    heads = q_ref.shape[1] // LANE
    rows = 2 * DIFF_HEAD_DIM + V_PAD
    lanes = [slice(j * LANE, (j + 1) * LANE) for j in range(heads)]
    lane = lax.broadcasted_iota(jnp.int32, (q_ref.shape[0], LANE), 1)
    qs = []
    for sl in lanes:
        q = q_ref[:, sl]
        zero = jnp.zeros_like(q)
        qs += [jnp.where(lane < DIFF_HEAD_DIM, q, zero), jnp.where(lane < DIFF_HEAD_DIM, zero, q)]

    def attend(keys):
        outs = _softmax_t(
            keys, qs, [lambda ks, sl=sl: k_ref[ks, sl] for sl in lanes for _ in range(2)],
            [lambda ks, j=j: vt_ref[j * rows:(j + 1) * rows, ks] for j in range(heads)
             for _ in range(2)])
        for j, sl in enumerate(lanes):
            o = outs[2 * j] - lam * outs[2 * j + 1]
            o = o * lax.rsqrt(jnp.mean(jnp.square(o), 0, keepdims=True) + NORM_EPS) * subln_ref[...]
            o_ref[:, sl] = (o * (1.0 - lambda_init)).T.astype(o_ref.dtype)

    _key_ranges(attend, lat_tiles, seq, k_ref.shape[0])


def _diff_layer(geom, hx, mod, w_in, lam_q1, lam_k1, lam_q2, lam_k2, subln, lambda_init):
    d = D_MODEL
    hd = DIFF_HEAD_DIM
    half_perm = _deinterleave(hd)
    head_perm = half_perm + [hd + p for p in half_perm]
    width = DIFF_HEADS * LANE
    qk_perm = jnp.array([h * LANE + p for h in range(DIFF_HEADS) for p in head_perm])
    w_qk = jnp.concatenate([w_in[:, :width][:, qk_perm], w_in[:, width:2 * width][:, qk_perm]],
                           -1).astype(BF16)
    w_vt = w_in[:, 2 * width:].T.astype(BF16)
    tables = _rope_tables(geom, hd, ["e", "o", "e", "o"])
    tab_spec = pl.BlockSpec((ROW_TILE, LANE), lambda b, t: (t, 0))
    out = jax.ShapeDtypeStruct((geom.batch, geom.nt, width), BF16)
    vrows = DIFF_HEADS * (2 * hd + V_PAD)
    q, k, vt = pl.pallas_call(
        _diff_proj_kernel,
        grid=(geom.batch, geom.tiles),
        in_specs=[geom.row_spec(d), geom.mod_spec(0), geom.mod_spec(1), _const_spec(w_qk.shape),
                  _const_spec(w_vt.shape), tab_spec, tab_spec, tab_spec],
        out_specs=[geom.row_spec(width), geom.row_spec(width),
                   pl.BlockSpec((None, vrows, ROW_TILE), lambda b, t: (b, 0, t))],
        out_shape=[out, out, jax.ShapeDtypeStruct((geom.batch, vrows, geom.nt), BF16)],
        compiler_params=_cparams("parallel", "parallel"),
        name="diff_proj",
    )(hx, mod, mod, w_qk, w_vt, *tables)
    lam_p = jnp.stack([lam_q1, lam_k1, lam_q2, lam_k2]).astype(F32)
    kern = functools.partial(_diff_attn_kernel, lambda_init=lambda_init)
    hps = 2
    return _attention(kern, geom, q, k, vt, [lam_p, subln.reshape(-1, 1)], DIFF_HEADS // hps,
                      hps * LANE, hps * (2 * hd + V_PAD), hps * LANE, "diff_attn")


def _chain_block(geom, rev):
    tiles, lat = geom.tiles, geom.lat_tiles
    if rev:
        return lambda j: tiles - 1 - j
    ctx_tiles = tiles - lat
    return lambda j: jnp.where(j < ctx_tiles, j + lat, j - ctx_tiles)


def _gla_proj_kernel(h_ref, sh_ref, sc_ref, win_ref, q_ref, k_ref, v_ref, g_ref, r_ref):
    u = h_ref[...] * (1.0 + sc_ref[...]) + sh_ref[...]
    t = _dot(u, win_ref[...])
    kd, vd = GLA_KEY_DIM, GLA_VALUE_DIM
    q_ref[...] = t[:, :kd] * GLA_DK ** -0.5
    k_ref[...] = t[:, kd:2 * kd]
    v_ref[...] = t[:, 2 * kd:2 * kd + vd].astype(v_ref.dtype)
    g_ref[...] = t[:, 2 * kd + vd:2 * kd + 2 * vd]
    r_ref[...] = t[:, 2 * kd + 2 * vd:]


def _gla_scan_kernel(q_ref, k_ref, v_ref, r_ref, gw_ref, gb_ref, o_ref, st_ref, *, rev):
    @pl.when(pl.program_id(1) == 0)
    def _():
        st_ref[...] = jnp.zeros_like(st_ref)

    n = ROW_TILE
    r = lax.broadcasted_iota(jnp.int32, (n, n), 0)
    c = lax.broadcasted_iota(jnp.int32, (n, n), 1)
    same = (r >> 6) == (c >> 6)
    incl = jnp.logical_and(same, (c >= r) if rev else (r >= c))
    la = _dot(r_ref[...], gw_ref[...]) + gb_ref[...]
    la = -_softplus(-la) / GLA_GATE_NORM
    cum = _dot_exact(incl.astype(F32), la)
    tot = _dot_exact(same.astype(F32), la)
    hs = range(GLA_HEADS)
    ksl = [slice(hd * GLA_DK, (hd + 1) * GLA_DK) for hd in hs]
    vsl = [slice(hd * GLA_DV, (hd + 1) * GLA_DV) for hd in hs]

    q_dec = [(q_ref[:, ksl[hd]] * jnp.exp(cum[:, ksl[hd]])).astype(BF16) for hd in hs]
    k_inv = [(k_ref[:, ksl[hd]] * jnp.exp(-cum[:, ksl[hd]])).astype(BF16) for hd in hs]
    k_end = [(k_ref[:, ksl[hd]] * jnp.exp(tot[:, ksl[hd]] - cum[:, ksl[hd]])).astype(BF16)
             for hd in hs]
    a = [jnp.where(incl, _dot_nt(q_dec[hd], k_inv[hd]), 0.0).astype(BF16) for hd in hs]
    o_intra = [jnp.dot(a[hd], v_ref[:, vsl[hd]], preferred_element_type=F32) for hd in hs]

    n_chunks = ROW_TILE // CHUNK
    st = [st_ref[hd] for hd in hs]
    for ci in (range(n_chunks - 1, -1, -1) if rev else range(n_chunks)):
        rows = slice(ci * CHUNK, (ci + 1) * CHUNK)
        for hd in hs:
            o_ref[rows, vsl[hd]] = o_intra[hd][rows] + _dot_nt(q_dec[hd][rows], st[hd])
        st = [st[hd] * jnp.exp(tot[ci * CHUNK:ci * CHUNK + 1, ksl[hd]])
              + _dot(v_ref[rows, vsl[hd]].astype(F32).T, k_end[hd][rows]) for hd in hs]
    for hd in hs:
        st_ref[hd] = st[hd]


def _gla_layer(geom, hx, mod, w_in, gate_w_fwd, gate_b_fwd, gate_w_bwd, gate_b_bwd):
    d = D_MODEL
    kd, vd, rk = GLA_KEY_DIM, GLA_VALUE_DIM, GLA_GATE_RANK
    main = 2 * kd + 2 * vd
    w_in_p = jnp.concatenate([w_in, jnp.zeros((d, LANE - 2 * rk), F32)], -1).astype(BF16)
    b, nt = geom.batch, geom.nt
    q, k, v, g, r = pl.pallas_call(
        _gla_proj_kernel,
        grid=(b, geom.tiles),
        in_specs=[geom.row_spec(d), geom.mod_spec(0), geom.mod_spec(1), _const_spec(w_in_p.shape)],
        out_specs=[geom.row_spec(kd), geom.row_spec(kd), geom.row_spec(vd), geom.row_spec(vd),
                   geom.row_spec(LANE)],
        out_shape=[jax.ShapeDtypeStruct((b, nt, kd), F32), jax.ShapeDtypeStruct((b, nt, kd), F32),
                   jax.ShapeDtypeStruct((b, nt, vd), BF16), jax.ShapeDtypeStruct((b, nt, vd), F32),
                   jax.ShapeDtypeStruct((b, nt, LANE), F32)],
        compiler_params=_cparams("parallel", "parallel"),
        name="gla_proj",
    )(hx, mod, mod, w_in_p)
    assert w_in_p.shape[1] == main + LANE

    def scan(rev, gate_w, gate_b, lane_off):
        gw = jnp.zeros((LANE, kd), F32).at[lane_off:lane_off + rk].set(gate_w).astype(BF16)
        blk = _chain_block(geom, rev)
        row = lambda w: pl.BlockSpec((None, ROW_TILE, w), lambda bi, j: (bi, blk(j), 0))
        return pl.pallas_call(
            functools.partial(_gla_scan_kernel, rev=rev),
            grid=(b, geom.tiles),
            in_specs=[row(kd), row(kd), row(vd), row(LANE), _const_spec(gw.shape),
                      _const_spec((1, kd))],
            out_specs=row(vd),
            out_shape=jax.ShapeDtypeStruct((b, nt, vd), F32),
            scratch_shapes=[pltpu.VMEM((GLA_HEADS, GLA_DV, GLA_DK), F32)],
            compiler_params=_cparams("parallel", "arbitrary"),
            name="gla_scan_bwd" if rev else "gla_scan_fwd",
        )(q, k, v, r, gw, gate_b.reshape(1, kd))

    return scan(False, gate_w_fwd, gate_b_fwd, 0), scan(True, gate_w_bwd, gate_b_bwd, rk), g


def _gdn_proj_kernel(hp_ref, h_ref, hn_ref, sh_ref, sc_ref, wc_ref, wz_ref, ws_ref, wst_ref,
                     cw_ref, q_ref, k_ref, v_ref, z_ref, s_ref, st_ref, *, seq, nt):
    t_idx = pl.program_id(1)
    sc, sh = 1.0 + sc_ref[...], sh_ref[...]
    u = h_ref[...] * sc + sh
    row0 = t_idx * ROW_TILE
    has_prev = jnp.logical_and(row0 != 0, row0 != seq)
    has_next = jnp.logical_and(row0 + ROW_TILE != seq, row0 + ROW_TILE != nt)
    u_prev = jnp.where(has_prev, hp_ref[...] * sc + sh, 0.0)
    u_next = jnp.where(has_next, hn_ref[...] * sc + sh, 0.0)
    u_ext = jnp.concatenate([u_prev, u, u_next], 0).astype(BF16)
    u = u.astype(BF16)
    ext = ROW_TILE + 2 * HALO
    blk = 512
    pad = GDN_CONV // 2
    for cb in range(GDN_CONV_CH // blk):
        cols = slice(cb * blk, (cb + 1) * blk)
        te = jnp.dot(u_ext, wc_ref[:, cols], preferred_element_type=F32)
        acc = None
        for j in range(GDN_CONV):
            shifted = te if j == pad else pltpu.roll(te, (pad - j) % ext, 0)
            term = shifted[HALO:HALO + ROW_TILE] * cw_ref[j:j + 1, cols]
            acc = term if acc is None else acc + term
        y = _silu(acc)
        if cols.start < 2 * GDN_KEY_DIM:
            dst, scale, off = ((q_ref, GDN_HEAD_DIM ** -0.5, 0) if cols.start < GDN_KEY_DIM
                               else (k_ref, 1.0, GDN_KEY_DIM))
            for hh in range(blk // LANE):
                yh = y[:, hh * LANE:(hh + 1) * LANE]
                yh = yh * lax.rsqrt(jnp.sum(jnp.square(yh), -1, keepdims=True) + NORM_EPS)
                c0 = cols.start - off + hh * LANE
                dst[:, c0:c0 + LANE] = yh * scale
        else:
            c0 = cols.start - 2 * GDN_KEY_DIM
            v_ref[:, c0:c0 + blk] = y
    z_ref[...] = jnp.dot(u, wz_ref[...], preferred_element_type=F32)
    s_ref[...] = jnp.dot(u, ws_ref[...], preferred_element_type=F32)
    st_ref[...] = lax.dot_general(wst_ref[...], u, (((1,), (1,)), ((), ())),
                                  preferred_element_type=F32)


def _tri_inverse(lmats):
    n = lmats[0].shape[0]
    r = lax.broadcasted_iota(jnp.int32, (n, n), 0)
    c = lax.broadcasted_iota(jnp.int32, (n, n), 1)
    same16 = (r >> 4) == (c >> 4)
    same32 = (r >> 5) == (c >> 5)
    mid32 = jnp.logical_and(same32, jnp.logical_not(same16))
    eye = (r == c).astype(F32)
    d1 = [jnp.where(same16, l, 0.0) for l in lmats]
    xs = [eye - a for a in d1]
    d1 = [a.astype(BF16) for a in d1]
    d2 = [_dot(a, a).astype(BF16) for a in d1]
    d4 = [_dot(a, a).astype(BF16) for a in d2]
    d8 = [_dot(a, a).astype(BF16) for a in d4]
    for powers in (d2, d4, d8):
        xs = [x + _dot(x, p) for x, p in zip(xs, powers)]
    for off in ([jnp.where(mid32, l, 0.0) for l in lmats],
                [jnp.where(same32, 0.0, l) for l in lmats]):
        xb = [x.astype(BF16) for x in xs]
        ys = [_dot(x, o) for x, o in zip(xb, off)]
        xs = [x - _dot(y, x16) for x, y, x16 in zip(xs, ys, xb)]
    return xs


def _gdn_scan_kernel(q_ref, k_ref, v_ref, s_ref, st_ref, alog_ref, dtb_ref, alogc_ref, dtbc_ref,
                     o_ref, state_ref, *, rev, lane_off, heads):
    @pl.when(pl.program_id(2) == 0)
    def _():
        state_ref[...] = jnp.zeros_like(state_ref)

    grp = pl.program_id(1)
    hv = GDN_VALUE_HEADS
    n = ROW_TILE
    r = lax.broadcasted_iota(jnp.int32, (n, n), 0)
    c = lax.broadcasted_iota(jnp.int32, (n, n), 1)
    same = (r >> 6) == (c >> 6)
    incl = jnp.logical_and(same, (c >= r) if rev else (r >= c))
    incl_f = incl.astype(F32)
    same_f = same.astype(F32)
    off_diag = r != c

    small = s_ref[...]
    small_t = st_ref[...]
    beta_all = jax.nn.sigmoid(small)
    g_all = -jnp.exp(alog_ref[...]) * _softplus(small + dtb_ref[...])
    a_rows = small_t[2 * hv + lane_off:2 * hv + lane_off + hv]
    g_t = -jnp.exp(alogc_ref[...]) * _softplus(a_rows + dtbc_ref[...])
    gc_all = _dot_exact(incl_f, g_all)
    tot_all = _dot_exact(same_f, g_all)
    gct_all = _dot_nt_exact(g_t, incl_f)
    tott_all = _dot_nt_exact(g_t, same_f)
    lane = lax.broadcasted_iota(jnp.int32, (n, LANE), 1)
    sub = lax.broadcasted_iota(jnp.int32, (hv, n), 0)
    hs = range(heads)

    def column(x, idx):
        return jnp.sum(jnp.where(lane == idx, x, 0.0), -1, keepdims=True)

    def row(x, idx):
        return jnp.sum(jnp.where(sub == idx, x, 0.0), 0, keepdims=True)

    head = [grp * heads + hh for hh in hs]
    beta = [column(beta_all, head[hh] + lane_off) for hh in hs]
    gc = [column(gc_all, head[hh] + 2 * hv + lane_off) for hh in hs]
    tot = [column(tot_all, head[hh] + 2 * hv + lane_off) for hh in hs]
    gc_row = [row(gct_all, head[hh]) for hh in hs]
    d_last = [jnp.exp(row(tott_all, head[hh])) for hh in hs]
    ksl = [slice((hh // 2) * LANE, (hh // 2 + 1) * LANE) for hh in hs]
    vsl = [slice(hh * LANE, (hh + 1) * LANE) for hh in hs]
    e_gc = [jnp.exp(g) for g in gc]

    gamma = [jnp.where(incl, jnp.exp(gc[hh] - gc_row[hh]), 0.0) for hh in hs]
    kb = [k_ref[:, ksl[hh]] * beta[hh] for hh in hs]
    lmat = [jnp.where(off_diag, _dot_nt(kb[hh], k_ref[:, ksl[hh]]) * gamma[hh], 0.0) for hh in hs]
    a_intra = [(_dot_nt(q_ref[:, ksl[hh]], k_ref[:, ksl[hh]]) * gamma[hh]).astype(BF16) for hh in hs]
    t_inv = _tri_inverse(lmat)
    uw = [_dot(t_inv[hh], jnp.concatenate([v_ref[:, vsl[hh]] * beta[hh], kb[hh] * e_gc[hh]], -1))
          for hh in hs]
    q_dec = [(q_ref[:, ksl[hh]] * e_gc[hh]).astype(BF16) for hh in hs]
    k_dec = [(k_ref[:, ksl[hh]] * jnp.exp(tot[hh] - gc[hh])).astype(BF16) for hh in hs]

    n_chunks = ROW_TILE // CHUNK
    st = [state_ref[hh] for hh in hs]
    zeros = jnp.zeros((CHUNK, LANE), F32)
    for ci in (range(n_chunks - 1, -1, -1) if rev else range(n_chunks)):
        rows = slice(ci * CHUNK, (ci + 1) * CHUNK)
        pair = slice(ci // 2 * LANE, (ci // 2 + 1) * LANE)
        st16 = [s.astype(BF16) for s in st]
        v_new = [uw[hh][rows, :LANE] - _dot_nt(uw[hh][rows, LANE:], st16[hh]) for hh in hs]
        for hh in hs:
            v_pair = jnp.concatenate([v_new[hh], zeros] if ci % 2 == 0 else [zeros, v_new[hh]], 0)
            o_ref[rows, vsl[hh]] = (_dot_nt(q_dec[hh][rows], st16[hh])
                                    + _dot(a_intra[hh][rows, pair], v_pair))
        st = [st[hh] * d_last[hh][:, ci * CHUNK:ci * CHUNK + 1] + _dot(v_new[hh].T, k_dec[hh][rows])
              for hh in hs]
    for hh in hs:
        state_ref[hh] = st[hh]


def _gdn_layer(geom, hx, mod, w_in, conv_w, a_log_fwd, dt_bias_fwd, a_log_bwd, dt_bias_bwd):
    d = D_MODEL
    b, nt = geom.batch, geom.nt
    hv = GDN_VALUE_HEADS
    wc = w_in[:, :GDN_CONV_CH].astype(BF16)
    wz = w_in[:, GDN_CONV_CH:GDN_CONV_CH + GDN_VALUE_DIM].astype(BF16)
    ws_t = w_in[:, GDN_CONV_CH + GDN_VALUE_DIM:].T
    ws = jnp.concatenate([ws_t.T, jnp.zeros((d, LANE - 4 * hv), F32)], -1).astype(BF16)
    ws_t = ws_t.astype(BF16)
    halo_per_tile = ROW_TILE // HALO
    last_halo = nt // HALO - 1
    prev_spec = pl.BlockSpec((None, HALO, d),
                             lambda bi, t: (bi, jnp.maximum(t * halo_per_tile - 1, 0), 0))
    next_spec = pl.BlockSpec((None, HALO, d),
                             lambda bi, t: (bi, jnp.minimum((t + 1) * halo_per_tile, last_halo), 0))
    q, k, v, z, small, small_t = pl.pallas_call(
        functools.partial(_gdn_proj_kernel, seq=geom.seq, nt=nt),
        grid=(b, geom.tiles),
        in_specs=[prev_spec, geom.row_spec(d), next_spec, geom.mod_spec(0), geom.mod_spec(1),
                  _const_spec(wc.shape), _const_spec(wz.shape), _const_spec(ws.shape),
                  _const_spec(ws_t.shape), _const_spec(conv_w.shape)],
        out_specs=[geom.row_spec(GDN_KEY_DIM), geom.row_spec(GDN_KEY_DIM),
                   geom.row_spec(GDN_VALUE_DIM), geom.row_spec(GDN_VALUE_DIM), geom.row_spec(LANE),
                   pl.BlockSpec((None, 4 * hv, ROW_TILE), lambda bi, t: (bi, 0, t))],
        out_shape=[jax.ShapeDtypeStruct((b, nt, GDN_KEY_DIM), F32),
                   jax.ShapeDtypeStruct((b, nt, GDN_KEY_DIM), F32),
                   jax.ShapeDtypeStruct((b, nt, GDN_VALUE_DIM), F32),
                   jax.ShapeDtypeStruct((b, nt, GDN_VALUE_DIM), F32),
                   jax.ShapeDtypeStruct((b, nt, LANE), F32),
                   jax.ShapeDtypeStruct((b, 4 * hv, nt), F32)],
        compiler_params=_cparams("parallel", "parallel"),
        name="gdn_proj",
    )(hx, hx, hx, mod, mod, wc, wz, ws, ws_t, conv_w)

    heads = 16
    groups = hv // heads

    def scan(rev, a_log, dt_bias, lane_off):
        blk = _chain_block(geom, rev)
        place = lambda p: jnp.zeros((1, LANE), F32).at[0, 2 * hv + lane_off:3 * hv + lane_off].set(p)
        return pl.pallas_call(
            functools.partial(_gdn_scan_kernel, rev=rev, lane_off=lane_off, heads=heads),
            grid=(b, groups, geom.tiles),
            in_specs=[
                pl.BlockSpec((None, ROW_TILE, heads // 2 * LANE), lambda bi, g, j: (bi, blk(j), g)),
                pl.BlockSpec((None, ROW_TILE, heads // 2 * LANE), lambda bi, g, j: (bi, blk(j), g)),
                pl.BlockSpec((None, ROW_TILE, heads * LANE), lambda bi, g, j: (bi, blk(j), g)),
                pl.BlockSpec((None, ROW_TILE, LANE), lambda bi, g, j: (bi, blk(j), 0)),
                pl.BlockSpec((None, 4 * hv, ROW_TILE), lambda bi, g, j: (bi, 0, blk(j))),
                _const_spec((1, LANE)), _const_spec((1, LANE)),
                _const_spec((hv, 1)), _const_spec((hv, 1))],
            out_specs=pl.BlockSpec((None, ROW_TILE, heads * LANE),
                                   lambda bi, g, j: (bi, blk(j), g)),
            out_shape=jax.ShapeDtypeStruct((b, nt, GDN_VALUE_DIM), F32),
            scratch_shapes=[pltpu.VMEM((heads, GDN_HEAD_DIM, GDN_HEAD_DIM), F32)],
            compiler_params=_cparams("parallel", "parallel", "arbitrary"),
            name="gdn_scan_bwd" if rev else "gdn_scan_fwd",
        )(q, k, v, small, small_t, place(a_log), place(dt_bias),
          a_log.reshape(hv, 1), dt_bias.reshape(hv, 1))

    return (scan(False, a_log_fwd, dt_bias_fwd, 0), scan(True, a_log_bwd, dt_bias_bwd, hv), z)


def _residual_norm(h, gate, y, g, b):
    return _layer_norm(ALPHA * h + gate * y, g, b)


def _sublayers_tail(y, h_ref, g1_ref, sh2_ref, sc2_ref, g2_ref, ln1g_ref, ln1b_ref, w1_ref, w2_ref,
                    ln2g_ref, ln2b_ref, out_ref):
    h = _residual_norm(h_ref[...], g1_ref[...], y, ln1g_ref[...], ln1b_ref[...])
    u = (h * (1.0 + sc2_ref[...]) + sh2_ref[...]).astype(BF16)
    blk = 1024
    y2 = None
    for c in range(0, D_FF, blk):
        a = jnp.square(jnp.maximum(jnp.dot(u, w1_ref[:, c:c + blk], preferred_element_type=F32), 0.0))
        part = jnp.dot(a.astype(BF16), w2_ref[c:c + blk, :], preferred_element_type=F32)
        y2 = part if y2 is None else y2 + part
    out_ref[...] = _residual_norm(h, g2_ref[...], y2, ln2g_ref[...], ln2b_ref[...])


def _attn_tail_kernel(o_ref, w_ref, *rest):
    _sublayers_tail(jnp.dot(o_ref[...], w_ref[...], preferred_element_type=F32), *rest)


def _gated_tail_kernel(of_ref, ob_ref, z_ref, ng_ref, w_ref, *rest, head_dim):
    o = of_ref[...] + ob_ref[...]
    ng = ng_ref[...]
    parts = [_rms(o[:, c:c + head_dim], ng) for c in range(0, o.shape[1], head_dim)]
    x = jnp.concatenate(parts, -1) * _silu(z_ref[...])
    _sublayers_tail(_dot(x, w_ref[...]), *rest)


def _tail_call(kernel, geom, tiles, acts, mixer_consts, hx, mod, ln1, w1, w2, ln2, name):
    d = D_MODEL
    consts = list(ln1) + [w1, w2] + list(ln2)
    in_specs = ([geom.row_spec(a.shape[-1]) for a in acts]
                + [_const_spec(c.shape) for c in mixer_consts]
                + [geom.row_spec(d)] + [geom.mod_spec(k) for k in (2, 3, 4, 5)]
                + [_const_spec(c.shape) for c in consts])
    return pl.pallas_call(
        kernel,
        grid=(geom.batch, tiles),
        in_specs=in_specs,
        out_specs=geom.row_spec(d),
        out_shape=jax.ShapeDtypeStruct((geom.batch, tiles * ROW_TILE, d), F32),
        compiler_params=_cparams("parallel", "parallel"),
        name=name,
    )(*acts, *mixer_consts, hx, mod, mod, mod, mod, *consts)


def kernel(x, c, ctx, c_ctx, ada_w, ada_b, ln1_g, ln1_b, ln2_g, ln2_b, mlp_w1, mlp_w2, mla_w_in, mla_q_norm, mla_kv_norm, mla_w_qb, mla_w_kvb, mla_w_out, diff_w_in, diff_lambda_q1, diff_lambda_k1, diff_lambda_q2, diff_lambda_k2, diff_subln, diff_w_out, gla_w_in, gla_gate_w_fwd, gla_gate_b_fwd, gla_gate_w_bwd, gla_gate_b_bwd, gla_norm, gla_w_out, gdn_w_in, gdn_conv_w, gdn_a_log_fwd, gdn_dt_bias_fwd, gdn_a_log_bwd, gdn_dt_bias_bwd, gdn_norm, gdn_w_out):
    batch, seq, d = x.shape
    geom = _Geom(batch, seq, ctx.shape[1])
    depth = ada_w.shape[0]
    cond_rows = -(-(batch + 1) // 8) * 8
    s = jnp.concatenate([c, c_ctx[None], jnp.zeros((cond_rows - batch - 1, d), F32)], 0)
    mods = _adaln(s, ada_w, ada_b)
    hx = jnp.concatenate([x, ctx], 1)
    for i in range(depth):
        last = i == depth - 1
        tiles = geom.lat_tiles if last else geom.tiles
        kind, j = i % 4, i // 4
        mod = mods[i].reshape(cond_rows, 1, 6 * d)
        ln1 = [ln1_g[i].reshape(1, d), ln1_b[i].reshape(1, d)]
        ln2 = [ln2_g[i].reshape(1, d), ln2_b[i].reshape(1, d)]
        tail = functools.partial(_tail_call, hx=hx, mod=mod, ln1=ln1, w1=mlp_w1[i].astype(BF16),
                                 w2=mlp_w2[i].astype(BF16), ln2=ln2)
        if kind == 0:
            o = _mla_layer(geom, hx, mod, mla_w_in[j], mla_q_norm[j], mla_kv_norm[j], mla_w_qb[j],
                           mla_w_kvb[j])
            hx = tail(_attn_tail_kernel, geom, tiles, [o], [mla_w_out[j].astype(BF16)],
                      name="mla_tail")
        elif kind == 1:
            lambda_init = 0.8 - 0.6 * math.exp(-0.3 * i)
            o = _diff_layer(geom, hx, mod, diff_w_in[j], diff_lambda_q1[j], diff_lambda_k1[j],
                            diff_lambda_q2[j], diff_lambda_k2[j], diff_subln[j], lambda_init)
            hx = tail(_attn_tail_kernel, geom, tiles, [o], [diff_w_out[j].astype(BF16)],
                      name="diff_tail")
        elif kind == 2:
            o_f, o_b, gate = _gla_layer(geom, hx, mod, gla_w_in[j], gla_gate_w_fwd[j],
                                        gla_gate_b_fwd[j], gla_gate_w_bwd[j], gla_gate_b_bwd[j])
            hx = tail(functools.partial(_gated_tail_kernel, head_dim=GLA_DV), geom, tiles,
                      [o_f, o_b, gate], [gla_norm[j].reshape(1, -1), gla_w_out[j].astype(BF16)],
                      name="gla_tail")
        else:
            o_f, o_b, gate = _gdn_layer(geom, hx, mod, gdn_w_in[j], gdn_conv_w[j], gdn_a_log_fwd[j],
                                        gdn_dt_bias_fwd[j], gdn_a_log_bwd[j], gdn_dt_bias_bwd[j])
            hx = tail(functools.partial(_gated_tail_kernel, head_dim=GDN_HEAD_DIM), geom, tiles,
                      [o_f, o_b, gate], [gdn_norm[j].reshape(1, -1), gdn_w_out[j].astype(BF16)],
                      name="gdn_tail")
    return hx[:, :seq]
```

```python
import functools
import math

import jax
import jax.numpy as jnp
from jax import lax
from jax.experimental import pallas as pl
from jax.experimental.pallas import tpu as pltpu

F32 = jnp.float32
BF16 = jnp.bfloat16

D_MODEL = 1024
DEPTH = 4
GRID_W = 64
D_FF = 4 * D_MODEL
ALPHA = (2 * DEPTH) ** 0.25
NORM_EPS = 1e-6
ROPE_BASE = 10000.0
CHUNK = 64

MLA_HEADS = 16
MLA_NOPE = 64
MLA_ROPE = 32
MLA_V = 64
MLA_KV_LORA = 256
MLA_Q_LORA = 768

DIFF_HEAD_DIM = 64
DIFF_HEADS = 8

GLA_HEADS = 4
GLA_KEY_DIM = 512
GLA_VALUE_DIM = 1024
GLA_DK = 128
GLA_DV = 256
GLA_GATE_RANK = 16
GLA_GATE_NORM = 16.0

GDN_HEAD_DIM = 128
GDN_KEY_HEADS = 8
GDN_VALUE_HEADS = 16
GDN_KEY_DIM = 1024
GDN_VALUE_DIM = 2048
GDN_CONV = 5
GDN_CONV_CH = 4096

LANE = 128
ROW_TILE = 256
HALO = 8
V_PAD = 16
KEY_BLOCK = 256
LOG2E = math.log2(math.e)
VMEM_LIMIT = 56 * 1024 * 1024


def _cparams(*sem):
    return pltpu.CompilerParams(dimension_semantics=sem, vmem_limit_bytes=VMEM_LIMIT)


def _const_spec(shape):
    nd = len(shape)
    return pl.BlockSpec(shape, lambda *_: (0,) * nd, pipeline_mode=pl.Buffered(1))


def _dot(a, b):
    return jnp.dot(a.astype(BF16), b.astype(BF16), preferred_element_type=F32)


def _dot_nt(a, b):
    return lax.dot_general(a.astype(BF16), b.astype(BF16), (((1,), (1,)), ((), ())),
                           preferred_element_type=F32)


def _dot_exact(a, b):
    return jnp.dot(a, b, preferred_element_type=F32, precision=lax.Precision.HIGHEST)


def _dot_nt_exact(a, b):
    return lax.dot_general(a, b, (((1,), (1,)), ((), ())), preferred_element_type=F32,
                           precision=lax.Precision.HIGHEST)


def _rms(x, g):
    return x * lax.rsqrt(jnp.mean(jnp.square(x), -1, keepdims=True) + NORM_EPS) * g


def _layer_norm(z, g, b):
    mu = jnp.mean(z, -1, keepdims=True)
    zc = z - mu
    var = jnp.mean(jnp.square(zc), -1, keepdims=True)
    return zc * lax.rsqrt(var + NORM_EPS) * g + b


def _softplus(x):
    return jnp.maximum(x, 0.0) + jnp.log1p(jnp.exp(-jnp.abs(x)))


def _silu(x):
    return x * jax.nn.sigmoid(x)


def _adaln_kernel(s_ref, w_ref, b_ref, o_ref):
    o_ref[...] = _dot(_silu(s_ref[...]), w_ref[...]) + b_ref[...]


def _adaln(s, ada_w, ada_b):
    depth, d, n = ada_w.shape
    rows = s.shape[0]
    tn = 1536
    return pl.pallas_call(
        _adaln_kernel,
        grid=(depth, n // tn),
        in_specs=[pl.BlockSpec((rows, d), lambda l, j: (0, 0)),
                  pl.BlockSpec((None, d, tn), lambda l, j: (l, 0, j)),
                  pl.BlockSpec((None, 1, tn), lambda l, j: (l, 0, j))],
        out_specs=pl.BlockSpec((None, rows, tn), lambda l, j: (l, 0, j)),
        out_shape=jax.ShapeDtypeStruct((depth, rows, n), F32),
        compiler_params=_cparams("parallel", "parallel"),
        name="adaln",
    )(s, ada_w, ada_b.reshape(depth, 1, n))


class _Geom:
    def __init__(self, batch, seq, ctx):
        assert seq % ROW_TILE == 0 and ctx % ROW_TILE == 0
        self.batch, self.seq, self.ctx = batch, seq, ctx
        self.nt = seq + ctx
        self.lat_tiles = seq // ROW_TILE
        self.tiles = self.nt // ROW_TILE

    def mod_spec(self, k):
        lat_tiles, batch = self.lat_tiles, self.batch
        return pl.BlockSpec((None, 1, D_MODEL),
                            lambda b, t: (jnp.where(t < lat_tiles, b, batch), 0, k))

    def row_spec(self, width, tile=ROW_TILE):
        return pl.BlockSpec((None, tile, width), lambda b, t: (b, t, 0))


def _rope_angles(n, dim):
    rows = n // GRID_W
    row = jnp.repeat(jnp.arange(rows, dtype=F32), GRID_W)
    col = jnp.tile(jnp.arange(GRID_W, dtype=F32), rows)
    n_freq = dim // 4
    inv_freq = ROPE_BASE ** (-jnp.arange(n_freq, dtype=F32) / n_freq)
    ang = jnp.concatenate([row[:, None] * inv_freq, col[:, None] * inv_freq], -1)
    return jnp.cos(ang), jnp.sin(ang)


def _rope_tables(geom, dim, layout):
    cos, sin = _rope_angles(geom.seq, dim)
    one, zero = jnp.ones_like(cos), jnp.zeros_like(cos)
    c = jnp.concatenate([one if g == "-" else cos for g in layout], -1)
    sa = jnp.concatenate([-sin if g == "e" else zero for g in layout], -1)
    sb = jnp.concatenate([sin if g == "o" else zero for g in layout], -1)
    pad = lambda t, v: jnp.concatenate([t, jnp.full((geom.ctx, LANE), v, F32)], 0)
    return pad(c, 1.0), pad(sa, 0.0), pad(sb, 0.0)


def _rope(x, c, sa, sb, w):
    return x * c + pltpu.roll(x, LANE - w, 1) * sa + pltpu.roll(x, w, 1) * sb


def _deinterleave(n):
    return list(range(0, n, 2)) + list(range(1, n, 2))


def _store_values_t(vt_ref, vt, heads, dv):
    sub = lax.broadcasted_iota(jnp.int32, (V_PAD, vt.shape[1]), 0)
    ones = (sub == 0).astype(vt_ref.dtype)
    for hd in range(heads):
        r0 = hd * (dv + V_PAD)
        vt_ref[r0:r0 + dv, :] = vt[hd * dv:(hd + 1) * dv].astype(vt_ref.dtype)
        vt_ref[r0 + dv:r0 + dv + V_PAD, :] = ones


def _mla_proj_kernel(h_ref, sh_ref, sc_ref, win_ref, qn_ref, kvn_ref, wqb_ref, wk_ref, wvt_ref,
                     c_ref, sa_ref, sb_ref, qt_ref, k_ref, vt_ref):
    u = h_ref[...] * (1.0 + sc_ref[...]) + sh_ref[...]
    t = _dot(u, win_ref[...])
    cq, ckv, kr = t[:, :MLA_Q_LORA], t[:, MLA_Q_LORA:MLA_Q_LORA + MLA_KV_LORA], t[:, -LANE:]
    q = _dot(_rms(cq, qn_ref[...]), wqb_ref[...])
    kvn = _rms(ckv, kvn_ref[...]).astype(BF16)
    k_nope = jnp.dot(kvn, wk_ref[...], preferred_element_type=F32)
    c, sa, sb = c_ref[...], sa_ref[...], sb_ref[...]
    half = MLA_ROPE // 2
    scale = (MLA_NOPE + MLA_ROPE) ** -0.5 * LOG2E
    kr = _rope(pltpu.roll(kr, MLA_NOPE, 1), c, sa, sb, half)
    for hd in range(MLA_HEADS):
        sl = slice(hd * LANE, (hd + 1) * LANE)
        qt_ref[sl, :] = (_rope(q[:, sl], c, sa, sb, half) * scale).T.astype(qt_ref.dtype)
        k_ref[:, sl] = (k_nope[:, sl] + kr).astype(k_ref.dtype)
    _store_values_t(vt_ref, _dot_nt(wvt_ref[...], kvn), MLA_HEADS, MLA_V)


def _softmax_t(keys, qs, k_blocks, vt_blocks):
    hs = range(len(qs))
    m, acc = [None] * len(qs), [None] * len(qs)
    blocks = [slice(k0, k0 + KEY_BLOCK) for k0 in range(keys.start, keys.stop, KEY_BLOCK)]
    scores = lambda ks: [jnp.dot(k_blocks[i](ks), qs[i], preferred_element_type=F32)
                         for i in hs]
    st_next = scores(blocks[0])
    for bi, ks in enumerate(blocks):
        k0 = ks.start
        st, st_next = st_next, (scores(blocks[bi + 1]) if bi + 1 < len(blocks) else None)
        top = [jnp.max(s, 0, keepdims=True) for s in st]
        if k0 == keys.start:
            m = top
            acc = [jnp.dot(vt_blocks[i](ks), jnp.exp2(st[i] - m[i]).astype(BF16),
                           preferred_element_type=F32) for i in hs]
        else:
            m_new = [jnp.maximum(m[i], top[i]) for i in hs]
            acc = [acc[i] * jnp.exp2(m[i] - m_new[i])
                   + jnp.dot(vt_blocks[i](ks), jnp.exp2(st[i] - m_new[i]).astype(BF16),
                             preferred_element_type=F32) for i in hs]
            m = m_new
    dv = acc[0].shape[0] - V_PAD
    return [a[:dv] / a[dv:dv + 1] for a in acc]


def _key_ranges(body, lat_tiles, seq, nt):
    t = pl.program_id(2)
    pl.when(t < lat_tiles)(lambda: body(slice(0, nt)))
    pl.when(t >= lat_tiles)(lambda: body(slice(seq, nt)))


def _mla_attn_kernel(qt_ref, k_ref, vt_ref, o_ref, *, lat_tiles, seq):
    rows = MLA_V + V_PAD
    heads = qt_ref.shape[0] // LANE
    lanes = [slice(j * LANE, (j + 1) * LANE) for j in range(heads)]

    def attend(keys):
        outs = _softmax_t(keys, [qt_ref[sl, :] for sl in lanes],
                          [lambda ks, sl=sl: k_ref[ks, sl] for sl in lanes],
                          [lambda ks, j=j: vt_ref[j * rows:(j + 1) * rows, ks] for j in range(heads)])
        for j in range(0, heads, 2):
            o_ref[:, j // 2 * LANE:(j // 2 + 1) * LANE] = jnp.concatenate(
                outs[j:j + 2], 0).T.astype(o_ref.dtype)

    _key_ranges(attend, lat_tiles, seq, k_ref.shape[0])


def _attention(kernel, geom, qt, k, vt, extra, head_groups, qkw, vrows, ow, name):
    extra_specs = [_const_spec(e.shape) for e in extra]
    return pl.pallas_call(
        functools.partial(kernel, lat_tiles=geom.lat_tiles, seq=geom.seq),
        grid=(geom.batch, head_groups, geom.tiles),
        in_specs=extra_specs + [
            pl.BlockSpec((None, qkw, ROW_TILE), lambda b, g, t: (b, g, t)),
            pl.BlockSpec((None, geom.nt, qkw), lambda b, g, t: (b, 0, g)),
            pl.BlockSpec((None, vrows, geom.nt), lambda b, g, t: (b, g, 0))],
        out_specs=pl.BlockSpec((None, ROW_TILE, ow), lambda b, g, t: (b, t, g)),
        out_shape=jax.ShapeDtypeStruct((geom.batch, geom.nt, head_groups * ow), BF16),
        compiler_params=_cparams("parallel", "parallel", "arbitrary"),
        name=name,
    )(*extra, qt, k, vt)


def _mla_layer(geom, hx, mod, w_in, q_norm, kv_norm, w_qb, w_kvb):
    d = D_MODEL
    perm = jnp.array(_deinterleave(MLA_ROPE))
    w_in_p = jnp.concatenate(
        [w_in[:, :MLA_Q_LORA + MLA_KV_LORA], w_in[:, MLA_Q_LORA + MLA_KV_LORA:][:, perm],
         jnp.zeros((d, LANE - MLA_ROPE), F32)], -1).astype(BF16)
    wq = w_qb.reshape(MLA_Q_LORA, MLA_HEADS, MLA_NOPE + MLA_ROPE)
    wq = jnp.concatenate([wq[..., :MLA_NOPE], wq[..., MLA_NOPE:][..., perm],
                          jnp.zeros((MLA_Q_LORA, MLA_HEADS, LANE - MLA_NOPE - MLA_ROPE), F32)], -1)
    wq = wq.reshape(MLA_Q_LORA, MLA_HEADS * LANE).astype(BF16)
    wkv = w_kvb.reshape(MLA_KV_LORA, MLA_HEADS, MLA_NOPE + MLA_V)
    wk = jnp.concatenate([wkv[..., :MLA_NOPE],
                          jnp.zeros((MLA_KV_LORA, MLA_HEADS, LANE - MLA_NOPE), F32)], -1)
    wk = wk.reshape(MLA_KV_LORA, MLA_HEADS * LANE).astype(BF16)
    wvt = wkv[..., MLA_NOPE:].reshape(MLA_KV_LORA, MLA_HEADS * MLA_V).T.astype(BF16)
    tables = _rope_tables(geom, MLA_ROPE, ["-"] * 4 + ["e", "o"] + ["-"] * 2)
    tab_spec = pl.BlockSpec((ROW_TILE, LANE), lambda b, t: (t, 0))
    qk_w = MLA_HEADS * LANE
    vrows = MLA_HEADS * (MLA_V + V_PAD)
    col_spec = lambda rows: pl.BlockSpec((None, rows, ROW_TILE), lambda b, t: (b, 0, t))
    q, k, vt = pl.pallas_call(
        _mla_proj_kernel,
        grid=(geom.batch, geom.tiles),
        in_specs=[geom.row_spec(d), geom.mod_spec(0), geom.mod_spec(1),
                  _const_spec(w_in_p.shape), _const_spec((1, MLA_Q_LORA)),
                  _const_spec((1, MLA_KV_LORA)), _const_spec(wq.shape), _const_spec(wk.shape),
                  _const_spec(wvt.shape), tab_spec, tab_spec, tab_spec],
        out_specs=[col_spec(qk_w), geom.row_spec(qk_w), col_spec(vrows)],
        out_shape=[jax.ShapeDtypeStruct((geom.batch, qk_w, geom.nt), BF16),
                   jax.ShapeDtypeStruct((geom.batch, geom.nt, qk_w), BF16),
                   jax.ShapeDtypeStruct((geom.batch, vrows, geom.nt), BF16)],
        compiler_params=_cparams("parallel", "parallel"),
        name="mla_proj",
    )(hx, mod, mod, w_in_p, q_norm.reshape(1, -1), kv_norm.reshape(1, -1), wq, wk, wvt, *tables)
    hps = 4
    return _attention(_mla_attn_kernel, geom, q, k, vt, [], MLA_HEADS // hps, hps * LANE,
                      hps * (MLA_V + V_PAD), hps * MLA_V, "mla_attn")


def _diff_proj_kernel(h_ref, sh_ref, sc_ref, wqk_ref, wvt_ref, c_ref, sa_ref, sb_ref,
                      qt_ref, k_ref, vt_ref):
    u = (h_ref[...] * (1.0 + sc_ref[...]) + sh_ref[...]).astype(BF16)
    t = jnp.dot(u, wqk_ref[...], preferred_element_type=F32)
    c, sa, sb = c_ref[...], sa_ref[...], sb_ref[...]
    half = DIFF_HEAD_DIM // 2
    width = DIFF_HEADS * LANE
    scale = DIFF_HEAD_DIM ** -0.5 * LOG2E
    for hd in range(DIFF_HEADS):
        sl = slice(hd * LANE, (hd + 1) * LANE)
        qt_ref[sl, :] = (_rope(t[:, sl], c, sa, sb, half) * scale).T.astype(qt_ref.dtype)
        k_ref[:, sl] = _rope(t[:, width + hd * LANE:width + (hd + 1) * LANE],
                             c, sa, sb, half).astype(k_ref.dtype)
    _store_values_t(vt_ref, _dot_nt(wvt_ref[...], u), DIFF_HEADS, 2 * DIFF_HEAD_DIM)


def _diff_attn_kernel(lam_ref, subln_ref, qt_ref, k_ref, vt_ref, o_ref, *, lambda_init,
                      lat_tiles, seq):
    lam_p = lam_ref[...]
    lam = (jnp.exp(jnp.sum(lam_p[0:1] * lam_p[1:2], -1, keepdims=True))
           - jnp.exp(jnp.sum(lam_p[2:3] * lam_p[3:4], -1, keepdims=True)) + lambda_init)
    heads = qt_ref.shape[0] // LANE
    rows = 2 * DIFF_HEAD_DIM + V_PAD
    lanes = [slice(j * LANE, (j + 1) * LANE) for j in range(heads)]
    sub = lax.broadcasted_iota(jnp.int32, (LANE, qt_ref.shape[1]), 0)
    qs = []
    for sl in lanes:
        q = qt_ref[sl, :]
        zero = jnp.zeros_like(q)
        qs += [jnp.where(sub < DIFF_HEAD_DIM, q, zero), jnp.where(sub < DIFF_HEAD_DIM, zero, q)]

    def attend(keys):
        outs = _softmax_t(
            keys, qs, [lambda ks, sl=sl: k_ref[ks, sl] for sl in lanes for _ in range(2)],
            [lambda ks, j=j: vt_ref[j * rows:(j + 1) * rows, ks] for j in range(heads)
             for _ in range(2)])
        for j, sl in enumerate(lanes):
            o = outs[2 * j] - lam * outs[2 * j + 1]
            o = o * lax.rsqrt(jnp.mean(jnp.square(o), 0, keepdims=True) + NORM_EPS) * subln_ref[...]
            o_ref[:, sl] = (o * (1.0 - lambda_init)).T.astype(o_ref.dtype)

    _key_ranges(attend, lat_tiles, seq, k_ref.shape[0])


def _diff_layer(geom, hx, mod, w_in, lam_q1, lam_k1, lam_q2, lam_k2, subln, lambda_init):
    d = D_MODEL
    hd = DIFF_HEAD_DIM
    half_perm = _deinterleave(hd)
    head_perm = half_perm + [hd + p for p in half_perm]
    width = DIFF_HEADS * LANE
    qk_perm = jnp.array([h * LANE + p for h in range(DIFF_HEADS) for p in head_perm])
    w_qk = jnp.concatenate([w_in[:, :width][:, qk_perm], w_in[:, width:2 * width][:, qk_perm]],
                           -1).astype(BF16)
    w_vt = w_in[:, 2 * width:].T.astype(BF16)
    tables = _rope_tables(geom, hd, ["e", "o", "e", "o"])
    tab_spec = pl.BlockSpec((ROW_TILE, LANE), lambda b, t: (t, 0))
    out = jax.ShapeDtypeStruct((geom.batch, geom.nt, width), BF16)
    vrows = DIFF_HEADS * (2 * hd + V_PAD)
    col_spec = lambda rows: pl.BlockSpec((None, rows, ROW_TILE), lambda b, t: (b, 0, t))
    q, k, vt = pl.pallas_call(
        _diff_proj_kernel,
        grid=(geom.batch, geom.tiles),
        in_specs=[geom.row_spec(d), geom.mod_spec(0), geom.mod_spec(1), _const_spec(w_qk.shape),
                  _const_spec(w_vt.shape), tab_spec, tab_spec, tab_spec],
        out_specs=[col_spec(width), geom.row_spec(width), col_spec(vrows)],
        out_shape=[jax.ShapeDtypeStruct((geom.batch, width, geom.nt), BF16), out,
                   jax.ShapeDtypeStruct((geom.batch, vrows, geom.nt), BF16)],
        compiler_params=_cparams("parallel", "parallel"),
        name="diff_proj",
    )(hx, mod, mod, w_qk, w_vt, *tables)
    lam_p = jnp.stack([lam_q1, lam_k1, lam_q2, lam_k2]).astype(F32)
    kern = functools.partial(_diff_attn_kernel, lambda_init=lambda_init)
    hps = 2
    return _attention(kern, geom, q, k, vt, [lam_p, subln.reshape(-1, 1)], DIFF_HEADS // hps,
                      hps * LANE, hps * (2 * hd + V_PAD), hps * LANE, "diff_attn")


def _chain_block(geom, rev):
    tiles, lat = geom.tiles, geom.lat_tiles
    if rev:
        return lambda j: tiles - 1 - j
    ctx_tiles = tiles - lat
    return lambda j: jnp.where(j < ctx_tiles, j + lat, j - ctx_tiles)


def _gla_proj_kernel(h_ref, sh_ref, sc_ref, win_ref, q_ref, k_ref, v_ref, g_ref, r_ref):
    u = h_ref[...] * (1.0 + sc_ref[...]) + sh_ref[...]
    t = _dot(u, win_ref[...])
    kd, vd = GLA_KEY_DIM, GLA_VALUE_DIM
    q_ref[...] = t[:, :kd] * GLA_DK ** -0.5
    k_ref[...] = t[:, kd:2 * kd]
    v_ref[...] = t[:, 2 * kd:2 * kd + vd].astype(v_ref.dtype)
    g_ref[...] = t[:, 2 * kd + vd:2 * kd + 2 * vd]
    r_ref[...] = t[:, 2 * kd + 2 * vd:]


def _gla_scan_kernel(q_ref, k_ref, v_ref, r_ref, gw_ref, gb_ref, o_ref, st_ref, *, rev):
    @pl.when(pl.program_id(1) == 0)
    def _():
        st_ref[...] = jnp.zeros_like(st_ref)

    n = ROW_TILE
    r = lax.broadcasted_iota(jnp.int32, (n, n), 0)
    c = lax.broadcasted_iota(jnp.int32, (n, n), 1)
    same = (r >> 6) == (c >> 6)
    incl = jnp.logical_and(same, (c >= r) if rev else (r >= c))
    la = _dot(r_ref[...], gw_ref[...]) + gb_ref[...]
    la = -_softplus(-la) / GLA_GATE_NORM
    cum = _dot_exact(incl.astype(F32), la)
    tot = _dot_exact(same.astype(F32), la)
    hs = range(GLA_HEADS)
    ksl = [slice(hd * GLA_DK, (hd + 1) * GLA_DK) for hd in hs]
    vsl = [slice(hd * GLA_DV, (hd + 1) * GLA_DV) for hd in hs]

    q_dec = [(q_ref[:, ksl[hd]] * jnp.exp(cum[:, ksl[hd]])).astype(BF16) for hd in hs]
    k_inv = [(k_ref[:, ksl[hd]] * jnp.exp(-cum[:, ksl[hd]])).astype(BF16) for hd in hs]
    k_end = [(k_ref[:, ksl[hd]] * jnp.exp(tot[:, ksl[hd]] - cum[:, ksl[hd]])).astype(BF16)
             for hd in hs]
    a = [jnp.where(incl, _dot_nt(q_dec[hd], k_inv[hd]), 0.0).astype(BF16) for hd in hs]
    o_intra = [jnp.dot(a[hd], v_ref[:, vsl[hd]], preferred_element_type=F32) for hd in hs]

    n_chunks = ROW_TILE // CHUNK
    st = [st_ref[hd] for hd in hs]
    for ci in (range(n_chunks - 1, -1, -1) if rev else range(n_chunks)):
        rows = slice(ci * CHUNK, (ci + 1) * CHUNK)
        for hd in hs:
            o_ref[rows, vsl[hd]] = o_intra[hd][rows] + _dot_nt(q_dec[hd][rows], st[hd])
        st = [st[hd] * jnp.exp(tot[ci * CHUNK:ci * CHUNK + 1, ksl[hd]])
              + _dot(v_ref[rows, vsl[hd]].astype(F32).T, k_end[hd][rows]) for hd in hs]
    for hd in hs:
        st_ref[hd] = st[hd]


def _gla_layer(geom, hx, mod, w_in, gate_w_fwd, gate_b_fwd, gate_w_bwd, gate_b_bwd):
    d = D_MODEL
    kd, vd, rk = GLA_KEY_DIM, GLA_VALUE_DIM, GLA_GATE_RANK
    main = 2 * kd + 2 * vd
    w_in_p = jnp.concatenate([w_in, jnp.zeros((d, LANE - 2 * rk), F32)], -1).astype(BF16)
    b, nt = geom.batch, geom.nt
    q, k, v, g, r = pl.pallas_call(
        _gla_proj_kernel,
        grid=(b, geom.tiles),
        in_specs=[geom.row_spec(d), geom.mod_spec(0), geom.mod_spec(1), _const_spec(w_in_p.shape)],
        out_specs=[geom.row_spec(kd), geom.row_spec(kd), geom.row_spec(vd), geom.row_spec(vd),
                   geom.row_spec(LANE)],
        out_shape=[jax.ShapeDtypeStruct((b, nt, kd), F32), jax.ShapeDtypeStruct((b, nt, kd), F32),
                   jax.ShapeDtypeStruct((b, nt, vd), BF16), jax.ShapeDtypeStruct((b, nt, vd), F32),
                   jax.ShapeDtypeStruct((b, nt, LANE), F32)],
        compiler_params=_cparams("parallel", "parallel"),
        name="gla_proj",
    )(hx, mod, mod, w_in_p)
    assert w_in_p.shape[1] == main + LANE

    def scan(rev, gate_w, gate_b, lane_off):
        gw = jnp.zeros((LANE, kd), F32).at[lane_off:lane_off + rk].set(gate_w).astype(BF16)
        blk = _chain_block(geom, rev)
        row = lambda w: pl.BlockSpec((None, ROW_TILE, w), lambda bi, j: (bi, blk(j), 0))
        return pl.pallas_call(
            functools.partial(_gla_scan_kernel, rev=rev),
            grid=(b, geom.tiles),
            in_specs=[row(kd), row(kd), row(vd), row(LANE), _const_spec(gw.shape),
                      _const_spec((1, kd))],
            out_specs=row(vd),
            out_shape=jax.ShapeDtypeStruct((b, nt, vd), F32),
            scratch_shapes=[pltpu.VMEM((GLA_HEADS, GLA_DV, GLA_DK), F32)],
            compiler_params=_cparams("parallel", "arbitrary"),
            name="gla_scan_bwd" if rev else "gla_scan_fwd",
        )(q, k, v, r, gw, gate_b.reshape(1, kd))

    return scan(False, gate_w_fwd, gate_b_fwd, 0), scan(True, gate_w_bwd, gate_b_bwd, rk), g


def _gdn_proj_kernel(hp_ref, h_ref, hn_ref, sh_ref, sc_ref, wc_ref, wz_ref, ws_ref, wst_ref,
                     cw_ref, q_ref, k_ref, v_ref, z_ref, s_ref, st_ref, *, seq, nt):
    t_idx = pl.program_id(1)
    sc, sh = 1.0 + sc_ref[...], sh_ref[...]
    u = h_ref[...] * sc + sh
    row0 = t_idx * ROW_TILE
    has_prev = jnp.logical_and(row0 != 0, row0 != seq)
    has_next = jnp.logical_and(row0 + ROW_TILE != seq, row0 + ROW_TILE != nt)
    u_prev = jnp.where(has_prev, hp_ref[...] * sc + sh, 0.0)
    u_next = jnp.where(has_next, hn_ref[...] * sc + sh, 0.0)
    u_ext = jnp.concatenate([u_prev, u, u_next], 0).astype(BF16)
    u = u.astype(BF16)
    ext = ROW_TILE + 2 * HALO
    blk = 512
    pad = GDN_CONV // 2
    for cb in range(GDN_CONV_CH // blk):
        cols = slice(cb * blk, (cb + 1) * blk)
        te = jnp.dot(u_ext, wc_ref[:, cols], preferred_element_type=F32)
        acc = None
        for j in range(GDN_CONV):
            shifted = te if j == pad else pltpu.roll(te, (pad - j) % ext, 0)
            term = shifted[HALO:HALO + ROW_TILE] * cw_ref[j:j + 1, cols]
            acc = term if acc is None else acc + term
        y = _silu(acc)
        if cols.start < 2 * GDN_KEY_DIM:
            dst, scale, off = ((q_ref, GDN_HEAD_DIM ** -0.5, 0) if cols.start < GDN_KEY_DIM
                               else (k_ref, 1.0, GDN_KEY_DIM))
            for hh in range(blk // LANE):
                yh = y[:, hh * LANE:(hh + 1) * LANE]
                yh = yh * lax.rsqrt(jnp.sum(jnp.square(yh), -1, keepdims=True) + NORM_EPS)
                c0 = cols.start - off + hh * LANE
                dst[:, c0:c0 + LANE] = yh * scale
        else:
            c0 = cols.start - 2 * GDN_KEY_DIM
            v_ref[:, c0:c0 + blk] = y
    z_ref[...] = jnp.dot(u, wz_ref[...], preferred_element_type=F32)
    s_ref[...] = jnp.dot(u, ws_ref[...], preferred_element_type=F32)
    st_ref[...] = lax.dot_general(wst_ref[...], u, (((1,), (1,)), ((), ())),
                                  preferred_element_type=F32)


def _tri_inverse(lmats, out):
    n = lmats[0].shape[0]
    r = lax.broadcasted_iota(jnp.int32, (n, n), 0)
    c = lax.broadcasted_iota(jnp.int32, (n, n), 1)
    same16 = (r >> 4) == (c >> 4)
    same32 = (r >> 5) == (c >> 5)
    mid32 = jnp.logical_and(same32, jnp.logical_not(same16))
    eye = (r == c).astype(F32)
    d1 = [jnp.where(same16, l, 0.0) for l in lmats]
    xs = [eye - a for a in d1]
    powers = [a.astype(BF16) for a in d1]
    for _ in range(3):
        powers = [_dot(a, a).astype(BF16) for a in powers]
        yield
        xs = [x + _dot(x, p) for x, p in zip(xs, powers)]
        yield
    for off in ([jnp.where(mid32, l, 0.0) for l in lmats],
                [jnp.where(same32, 0.0, l) for l in lmats]):
        xb = [x.astype(BF16) for x in xs]
        ys = [_dot(x, o) for x, o in zip(xb, off)]
        yield
        xs = [x - _dot(y, x16) for x, y, x16 in zip(xs, ys, xb)]
        yield
    out.extend(xs)


def _interleave(*gens):
    live = list(gens)
    while live:
        for g in list(live):
            try:
                next(g)
            except StopIteration:
                live.remove(g)


def _gdn_scan_kernel(q_ref, k_ref, v_ref, s_ref, st_ref, alog_ref, dtb_ref, alogc_ref, dtbc_ref,
                     o_ref, state_ref, *, rev, lane_off, heads):
    @pl.when(pl.program_id(2) == 0)
    def _():
        state_ref[...] = jnp.zeros_like(state_ref)

    grp = pl.program_id(1)
    hv = GDN_VALUE_HEADS
    n = ROW_TILE
    r = lax.broadcasted_iota(jnp.int32, (n, n), 0)
    c = lax.broadcasted_iota(jnp.int32, (n, n), 1)
    same = (r >> 6) == (c >> 6)
    incl = jnp.logical_and(same, (c >= r) if rev else (r >= c))
    incl_f = incl.astype(F32)
    same_f = same.astype(F32)
    off_diag = r != c

    small = s_ref[...]
    small_t = st_ref[...]
    beta_all = jax.nn.sigmoid(small)
    g_all = -jnp.exp(alog_ref[...]) * _softplus(small + dtb_ref[...])
    a_rows = small_t[2 * hv + lane_off:2 * hv + lane_off + hv]
    g_t = -jnp.exp(alogc_ref[...]) * _softplus(a_rows + dtbc_ref[...])
    gc_all = _dot_exact(incl_f, g_all)
    tot_all = _dot_exact(same_f, g_all)
    gct_all = _dot_nt_exact(g_t, incl_f)
    tott_all = _dot_nt_exact(g_t, same_f)
    lane = lax.broadcasted_iota(jnp.int32, (n, LANE), 1)
    sub = lax.broadcasted_iota(jnp.int32, (hv, n), 0)
    hs = range(heads)

    def column(x, idx):
        return jnp.sum(jnp.where(lane == idx, x, 0.0), -1, keepdims=True)

    def row(x, idx):
        return jnp.sum(jnp.where(sub == idx, x, 0.0), 0, keepdims=True)

    ksl = [slice((hh // 2) * LANE, (hh // 2 + 1) * LANE) for hh in hs]
    vsl = [slice(hh * LANE, (hh + 1) * LANE) for hh in hs]
    shared, pre, post = {}, {}, {}

    def prepare(hh):
        kh = hh // 2
        if kh not in shared:
            k_t = k_ref[:, ksl[hh]].T.astype(BF16)
            shared[kh] = (k_t, jnp.dot(q_ref[:, ksl[hh]].astype(BF16), k_t,
                                       preferred_element_type=F32))
        k_t, qk = shared[kh]
        head = grp * heads + hh
        beta = column(beta_all, head + lane_off)
        gc = column(gc_all, head + 2 * hv + lane_off)
        tot = column(tot_all, head + 2 * hv + lane_off)
        gamma = jnp.where(incl, jnp.exp(gc - row(gct_all, head)), 0.0)
        kb = k_ref[:, ksl[hh]] * beta
        e_gc = jnp.exp(gc)
        pre[hh] = dict(
            lmat=jnp.where(off_diag, _dot(kb, k_t) * gamma, 0.0),
            a_intra=(qk * gamma).astype(BF16),
            rhs=jnp.concatenate([v_ref[:, vsl[hh]] * beta, kb * e_gc], -1).astype(BF16),
            q_dec=(q_ref[:, ksl[hh]] * e_gc).astype(BF16),
            k_dec_t=(k_ref[:, ksl[hh]] * jnp.exp(tot - gc)).T.astype(BF16),
            d_last=jnp.exp(row(tott_all, head)))

    def prepare_all(group):
        for hh in group:
            prepare(hh)
            yield

    def invert(group):
        t_inv = []
        yield from _tri_inverse([pre[hh]["lmat"] for hh in group], t_inv)
        for hh, t in zip(group, t_inv):
            post[hh] = _dot(t, pre[hh]["rhs"])
        yield

    def recur(group):
        n_chunks = ROW_TILE // CHUNK
        st = {hh: state_ref[hh] for hh in group}
        zeros = jnp.zeros((CHUNK, LANE), BF16)
        for ci in (range(n_chunks - 1, -1, -1) if rev else range(n_chunks)):
            rows = slice(ci * CHUNK, (ci + 1) * CHUNK)
            pair = slice(ci // 2 * LANE, (ci // 2 + 1) * LANE)
            ws_qs = {hh: jnp.dot(jnp.concatenate([post[hh][rows, LANE:].astype(BF16),
                                                  pre[hh]["q_dec"][rows]], 0),
                                 st[hh].astype(BF16), preferred_element_type=F32) for hh in group}
            yield
            v_new = {hh: (post[hh][rows, :LANE] - ws_qs[hh][:CHUNK]).astype(BF16) for hh in group}
            v_pair = {hh: jnp.concatenate([v_new[hh], zeros] if ci % 2 == 0 else [zeros, v_new[hh]],
                                          0) for hh in group}
            av_kv = {hh: jnp.dot(jnp.concatenate([pre[hh]["a_intra"][rows, pair],
                                                  pre[hh]["k_dec_t"][:, pair]], 0),
                                 v_pair[hh], preferred_element_type=F32) for hh in group}
            yield
            for hh in group:
                o_ref[rows, vsl[hh]] = ws_qs[hh][CHUNK:] + av_kv[hh][:CHUNK]
                st[hh] = (st[hh] * pre[hh]["d_last"][:, ci * CHUNK:ci * CHUNK + 1]
                          + av_kv[hh][CHUNK:])
        for hh in group:
            state_ref[hh] = st[hh]

    _interleave(prepare_all(list(hs)))
    _interleave(invert(list(hs)))
    _interleave(recur(list(hs)))


def _gdn_layer(geom, hx, mod, w_in, conv_w, a_log_fwd, dt_bias_fwd, a_log_bwd, dt_bias_bwd):
    d = D_MODEL
    b, nt = geom.batch, geom.nt
    hv = GDN_VALUE_HEADS
    wc = w_in[:, :GDN_CONV_CH].astype(BF16)
    wz = w_in[:, GDN_CONV_CH:GDN_CONV_CH + GDN_VALUE_DIM].astype(BF16)
    ws_t = w_in[:, GDN_CONV_CH + GDN_VALUE_DIM:].T
    ws = jnp.concatenate([ws_t.T, jnp.zeros((d, LANE - 4 * hv), F32)], -1).astype(BF16)
    ws_t = ws_t.astype(BF16)
    halo_per_tile = ROW_TILE // HALO
    last_halo = nt // HALO - 1
    prev_spec = pl.BlockSpec((None, HALO, d),
                             lambda bi, t: (bi, jnp.maximum(t * halo_per_tile - 1, 0), 0))
    next_spec = pl.BlockSpec((None, HALO, d),
                             lambda bi, t: (bi, jnp.minimum((t + 1) * halo_per_tile, last_halo), 0))
    q, k, v, z, small, small_t = pl.pallas_call(
        functools.partial(_gdn_proj_kernel, seq=geom.seq, nt=nt),
        grid=(b, geom.tiles),
        in_specs=[prev_spec, geom.row_spec(d), next_spec, geom.mod_spec(0), geom.mod_spec(1),
                  _const_spec(wc.shape), _const_spec(wz.shape), _const_spec(ws.shape),
                  _const_spec(ws_t.shape), _const_spec(conv_w.shape)],
        out_specs=[geom.row_spec(GDN_KEY_DIM), geom.row_spec(GDN_KEY_DIM),
                   geom.row_spec(GDN_VALUE_DIM), geom.row_spec(GDN_VALUE_DIM), geom.row_spec(LANE),
                   pl.BlockSpec((None, 4 * hv, ROW_TILE), lambda bi, t: (bi, 0, t))],
        out_shape=[jax.ShapeDtypeStruct((b, nt, GDN_KEY_DIM), F32),
                   jax.ShapeDtypeStruct((b, nt, GDN_KEY_DIM), F32),
                   jax.ShapeDtypeStruct((b, nt, GDN_VALUE_DIM), F32),
                   jax.ShapeDtypeStruct((b, nt, GDN_VALUE_DIM), F32),
                   jax.ShapeDtypeStruct((b, nt, LANE), F32),
                   jax.ShapeDtypeStruct((b, 4 * hv, nt), F32)],
        compiler_params=_cparams("parallel", "parallel"),
        name="gdn_proj",
    )(hx, hx, hx, mod, mod, wc, wz, ws, ws_t, conv_w)

    heads = 16
    groups = hv // heads

    def scan(rev, a_log, dt_bias, lane_off):
        blk = _chain_block(geom, rev)
        place = lambda p: jnp.zeros((1, LANE), F32).at[0, 2 * hv + lane_off:3 * hv + lane_off].set(p)
        return pl.pallas_call(
            functools.partial(_gdn_scan_kernel, rev=rev, lane_off=lane_off, heads=heads),
            grid=(b, groups, geom.tiles),
            in_specs=[
                pl.BlockSpec((None, ROW_TILE, heads // 2 * LANE), lambda bi, g, j: (bi, blk(j), g)),
                pl.BlockSpec((None, ROW_TILE, heads // 2 * LANE), lambda bi, g, j: (bi, blk(j), g)),
                pl.BlockSpec((None, ROW_TILE, heads * LANE), lambda bi, g, j: (bi, blk(j), g)),
                pl.BlockSpec((None, ROW_TILE, LANE), lambda bi, g, j: (bi, blk(j), 0)),
                pl.BlockSpec((None, 4 * hv, ROW_TILE), lambda bi, g, j: (bi, 0, blk(j))),
                _const_spec((1, LANE)), _const_spec((1, LANE)),
                _const_spec((hv, 1)), _const_spec((hv, 1))],
            out_specs=pl.BlockSpec((None, ROW_TILE, heads * LANE),
                                   lambda bi, g, j: (bi, blk(j), g)),
            out_shape=jax.ShapeDtypeStruct((b, nt, GDN_VALUE_DIM), F32),
            scratch_shapes=[pltpu.VMEM((heads, GDN_HEAD_DIM, GDN_HEAD_DIM), F32)],
            compiler_params=_cparams("parallel", "parallel", "arbitrary"),
            name="gdn_scan_bwd" if rev else "gdn_scan_fwd",
        )(q, k, v, small, small_t, place(a_log), place(dt_bias),
          a_log.reshape(hv, 1), dt_bias.reshape(hv, 1))

    return (scan(False, a_log_fwd, dt_bias_fwd, 0), scan(True, a_log_bwd, dt_bias_bwd, hv), z)


def _residual_norm(h, gate, y, g, b):
    return _layer_norm(ALPHA * h + gate * y, g, b)


def _sublayers_tail(y, h_ref, g1_ref, sh2_ref, sc2_ref, g2_ref, ln1g_ref, ln1b_ref, w1_ref, w2_ref,
                    ln2g_ref, ln2b_ref, out_ref):
    h = _residual_norm(h_ref[...], g1_ref[...], y, ln1g_ref[...], ln1b_ref[...])
    u = (h * (1.0 + sc2_ref[...]) + sh2_ref[...]).astype(BF16)
    blk = 1024
    y2 = None
    for c in range(0, D_FF, blk):
        a = jnp.square(jnp.maximum(jnp.dot(u, w1_ref[:, c:c + blk], preferred_element_type=F32), 0.0))
        part = jnp.dot(a.astype(BF16), w2_ref[c:c + blk, :], preferred_element_type=F32)
        y2 = part if y2 is None else y2 + part
    out_ref[...] = _residual_norm(h, g2_ref[...], y2, ln2g_ref[...], ln2b_ref[...])


def _attn_tail_kernel(o_ref, w_ref, *rest):
    _sublayers_tail(jnp.dot(o_ref[...], w_ref[...], preferred_element_type=F32), *rest)


def _gated_tail_kernel(of_ref, ob_ref, z_ref, ng_ref, w_ref, *rest, head_dim):
    o = of_ref[...] + ob_ref[...]
    ng = ng_ref[...]
    parts = [_rms(o[:, c:c + head_dim], ng) for c in range(0, o.shape[1], head_dim)]
    x = jnp.concatenate(parts, -1) * _silu(z_ref[...])
    _sublayers_tail(_dot(x, w_ref[...]), *rest)


def _tail_call(kernel, geom, tiles, acts, mixer_consts, hx, mod, ln1, w1, w2, ln2, name):
    d = D_MODEL
    consts = list(ln1) + [w1, w2] + list(ln2)
    in_specs = ([geom.row_spec(a.shape[-1]) for a in acts]
                + [_const_spec(c.shape) for c in mixer_consts]
                + [geom.row_spec(d)] + [geom.mod_spec(k) for k in (2, 3, 4, 5)]
                + [_const_spec(c.shape) for c in consts])
    return pl.pallas_call(
        kernel,
        grid=(geom.batch, tiles),
        in_specs=in_specs,
        out_specs=geom.row_spec(d),
        out_shape=jax.ShapeDtypeStruct((geom.batch, tiles * ROW_TILE, d), F32),
        compiler_params=_cparams("parallel", "parallel"),
        name=name,
    )(*acts, *mixer_consts, hx, mod, mod, mod, mod, *consts)


def kernel(x, c, ctx, c_ctx, ada_w, ada_b, ln1_g, ln1_b, ln2_g, ln2_b, mlp_w1, mlp_w2, mla_w_in, mla_q_norm, mla_kv_norm, mla_w_qb, mla_w_kvb, mla_w_out, diff_w_in, diff_lambda_q1, diff_lambda_k1, diff_lambda_q2, diff_lambda_k2, diff_subln, diff_w_out, gla_w_in, gla_gate_w_fwd, gla_gate_b_fwd, gla_gate_w_bwd, gla_gate_b_bwd, gla_norm, gla_w_out, gdn_w_in, gdn_conv_w, gdn_a_log_fwd, gdn_dt_bias_fwd, gdn_a_log_bwd, gdn_dt_bias_bwd, gdn_norm, gdn_w_out):
    batch, seq, d = x.shape
    geom = _Geom(batch, seq, ctx.shape[1])
    depth = ada_w.shape[0]
    cond_rows = -(-(batch + 1) // 8) * 8
    s = jnp.concatenate([c, c_ctx[None], jnp.zeros((cond_rows - batch - 1, d), F32)], 0)
    mods = _adaln(s, ada_w, ada_b)
    hx = jnp.concatenate([x, ctx], 1)
    for i in range(depth):
        last = i == depth - 1
        tiles = geom.lat_tiles if last else geom.tiles
        kind, j = i % 4, i // 4
        mod = mods[i].reshape(cond_rows, 1, 6 * d)
        ln1 = [ln1_g[i].reshape(1, d), ln1_b[i].reshape(1, d)]
        ln2 = [ln2_g[i].reshape(1, d), ln2_b[i].reshape(1, d)]
        tail = functools.partial(_tail_call, hx=hx, mod=mod, ln1=ln1, w1=mlp_w1[i].astype(BF16),
                                 w2=mlp_w2[i].astype(BF16), ln2=ln2)
        if kind == 0:
            o = _mla_layer(geom, hx, mod, mla_w_in[j], mla_q_norm[j], mla_kv_norm[j], mla_w_qb[j],
                           mla_w_kvb[j])
            hx = tail(_attn_tail_kernel, geom, tiles, [o], [mla_w_out[j].astype(BF16)],
                      name="mla_tail")
        elif kind == 1:
            lambda_init = 0.8 - 0.6 * math.exp(-0.3 * i)
            o = _diff_layer(geom, hx, mod, diff_w_in[j], diff_lambda_q1[j], diff_lambda_k1[j],
                            diff_lambda_q2[j], diff_lambda_k2[j], diff_subln[j], lambda_init)
            hx = tail(_attn_tail_kernel, geom, tiles, [o], [diff_w_out[j].astype(BF16)],
                      name="diff_tail")
        elif kind == 2:
            o_f, o_b, gate = _gla_layer(geom, hx, mod, gla_w_in[j], gla_gate_w_fwd[j],
                                        gla_gate_b_fwd[j], gla_gate_w_bwd[j], gla_gate_b_bwd[j])
            hx = tail(functools.partial(_gated_tail_kernel, head_dim=GLA_DV), geom, tiles,
                      [o_f, o_b, gate], [gla_norm[j].reshape(1, -1), gla_w_out[j].astype(BF16)],
                      name="gla_tail")
        else:
            o_f, o_b, gate = _gdn_layer(geom, hx, mod, gdn_w_in[j], gdn_conv_w[j], gdn_a_log_fwd[j],
                                        gdn_dt_bias_fwd[j], gdn_a_log_bwd[j], gdn_dt_bias_bwd[j])
            hx = tail(functools.partial(_gated_tail_kernel, head_dim=GDN_HEAD_DIM), geom, tiles,
                      [o_f, o_b, gate], [gdn_norm[j].reshape(1, -1), gdn_w_out[j].astype(BF16)],
                      name="gdn_tail")
    return hx[:, :seq]
```

```python
import functools
import math

import jax
import jax.numpy as jnp
from jax import lax
from jax.experimental import pallas as pl
from jax.experimental.pallas import tpu as pltpu

F32 = jnp.float32
BF16 = jnp.bfloat16

D_MODEL = 1024
DEPTH = 4
GRID_W = 64
D_FF = 4 * D_MODEL
ALPHA = (2 * DEPTH) ** 0.25
NORM_EPS = 1e-6
ROPE_BASE = 10000.0
CHUNK = 64

MLA_HEADS = 16
MLA_NOPE = 64
MLA_ROPE = 32
MLA_V = 64
MLA_KV_LORA = 256
MLA_Q_LORA = 768

DIFF_HEAD_DIM = 64
DIFF_HEADS = 8

GLA_HEADS = 4
GLA_KEY_DIM = 512
GLA_VALUE_DIM = 1024
GLA_DK = 128
GLA_DV = 256
GLA_GATE_RANK = 16
GLA_GATE_NORM = 16.0

GDN_HEAD_DIM = 128
GDN_KEY_HEADS = 8
GDN_VALUE_HEADS = 16
GDN_KEY_DIM = 1024
GDN_VALUE_DIM = 2048
GDN_CONV = 5
GDN_CONV_CH = 4096

LANE = 128
ROW_TILE = 256
HALO = 8
V_PAD = 16
KEY_BLOCK = 256
LOG2E = math.log2(math.e)
VMEM_LIMIT = 56 * 1024 * 1024


def _cparams(*sem):
    return pltpu.CompilerParams(dimension_semantics=sem, vmem_limit_bytes=VMEM_LIMIT)


def _const_spec(shape):
    nd = len(shape)
    return pl.BlockSpec(shape, lambda *_: (0,) * nd, pipeline_mode=pl.Buffered(1))


def _dot(a, b):
    return jnp.dot(a.astype(BF16), b.astype(BF16), preferred_element_type=F32)


def _dot_nt(a, b):
    return lax.dot_general(a.astype(BF16), b.astype(BF16), (((1,), (1,)), ((), ())),
                           preferred_element_type=F32)


def _dot_exact(a, b):
    return jnp.dot(a, b, preferred_element_type=F32, precision=lax.Precision.HIGHEST)


def _dot_nt_exact(a, b):
    return lax.dot_general(a, b, (((1,), (1,)), ((), ())), preferred_element_type=F32,
                           precision=lax.Precision.HIGHEST)


def _rms(x, g):
    return x * lax.rsqrt(jnp.mean(jnp.square(x), -1, keepdims=True) + NORM_EPS) * g


def _layer_norm(z, g, b):
    mu = jnp.mean(z, -1, keepdims=True)
    zc = z - mu
    var = jnp.mean(jnp.square(zc), -1, keepdims=True)
    return zc * lax.rsqrt(var + NORM_EPS) * g + b


def _softplus(x):
    return jnp.maximum(x, 0.0) + jnp.log1p(jnp.exp(-jnp.abs(x)))


def _silu(x):
    return x * jax.nn.sigmoid(x)


def _adaln_kernel(s_ref, w_ref, b_ref, o_ref):
    o_ref[...] = _dot(_silu(s_ref[...]), w_ref[...]) + b_ref[...]


def _adaln(s, ada_w, ada_b):
    depth, d, n = ada_w.shape
    rows = s.shape[0]
    tn = 1536
    return pl.pallas_call(
        _adaln_kernel,
        grid=(depth, n // tn),
        in_specs=[pl.BlockSpec((rows, d), lambda l, j: (0, 0)),
                  pl.BlockSpec((None, d, tn), lambda l, j: (l, 0, j)),
                  pl.BlockSpec((None, 1, tn), lambda l, j: (l, 0, j))],
        out_specs=pl.BlockSpec((None, rows, tn), lambda l, j: (l, 0, j)),
        out_shape=jax.ShapeDtypeStruct((depth, rows, n), F32),
        compiler_params=_cparams("parallel", "parallel"),
        name="adaln",
    )(s, ada_w, ada_b.reshape(depth, 1, n))


class _Geom:
    def __init__(self, batch, seq, ctx):
        assert seq % ROW_TILE == 0 and ctx % ROW_TILE == 0
        self.batch, self.seq, self.ctx = batch, seq, ctx
        self.nt = seq + ctx
        self.lat_tiles = seq // ROW_TILE
        self.tiles = self.nt // ROW_TILE

    def mod_spec(self, k):
        lat_tiles, batch = self.lat_tiles, self.batch
        return pl.BlockSpec((None, 1, D_MODEL),
                            lambda b, t: (jnp.where(t < lat_tiles, b, batch), 0, k))

    def row_spec(self, width, tile=ROW_TILE):
        return pl.BlockSpec((None, tile, width), lambda b, t: (b, t, 0))


def _rope_angles(n, dim):
    rows = n // GRID_W
    row = jnp.repeat(jnp.arange(rows, dtype=F32), GRID_W)
    col = jnp.tile(jnp.arange(GRID_W, dtype=F32), rows)
    n_freq = dim // 4
    inv_freq = ROPE_BASE ** (-jnp.arange(n_freq, dtype=F32) / n_freq)
    ang = jnp.concatenate([row[:, None] * inv_freq, col[:, None] * inv_freq], -1)
    return jnp.cos(ang), jnp.sin(ang)


def _rope_tables(geom, dim, layout):
    cos, sin = _rope_angles(geom.seq, dim)
    one, zero = jnp.ones_like(cos), jnp.zeros_like(cos)
    c = jnp.concatenate([one if g == "-" else cos for g in layout], -1)
    sa = jnp.concatenate([-sin if g == "e" else zero for g in layout], -1)
    sb = jnp.concatenate([sin if g == "o" else zero for g in layout], -1)
    pad = lambda t, v: jnp.concatenate([t, jnp.full((geom.ctx, LANE), v, F32)], 0)
    return pad(c, 1.0), pad(sa, 0.0), pad(sb, 0.0)


def _rope(x, c, sa, sb, w):
    return x * c + pltpu.roll(x, LANE - w, 1) * sa + pltpu.roll(x, w, 1) * sb


def _deinterleave(n):
    return list(range(0, n, 2)) + list(range(1, n, 2))


def _store_values_t(vt_ref, vt, heads, dv):
    sub = lax.broadcasted_iota(jnp.int32, (V_PAD, vt.shape[1]), 0)
    ones = (sub == 0).astype(vt_ref.dtype)
    for hd in range(heads):
        r0 = hd * (dv + V_PAD)
        vt_ref[r0:r0 + dv, :] = vt[hd * dv:(hd + 1) * dv].astype(vt_ref.dtype)
        vt_ref[r0 + dv:r0 + dv + V_PAD, :] = ones


def _mla_proj_kernel(h_ref, sh_ref, sc_ref, win_ref, qn_ref, kvn_ref, wqb_ref, wk_ref, wvt_ref,
                     c_ref, sa_ref, sb_ref, q_ref, k_ref, vt_ref):
    u = h_ref[...] * (1.0 + sc_ref[...]) + sh_ref[...]
    t = _dot(u, win_ref[...])
    cq, ckv, kr = t[:, :MLA_Q_LORA], t[:, MLA_Q_LORA:MLA_Q_LORA + MLA_KV_LORA], t[:, -LANE:]
    q = _dot(_rms(cq, qn_ref[...]), wqb_ref[...])
    kvn = _rms(ckv, kvn_ref[...]).astype(BF16)
    k_nope = jnp.dot(kvn, wk_ref[...], preferred_element_type=F32)
    c, sa, sb = c_ref[...], sa_ref[...], sb_ref[...]
    half = MLA_ROPE // 2
    scale = (MLA_NOPE + MLA_ROPE) ** -0.5 * LOG2E
    kr = _rope(pltpu.roll(kr, MLA_NOPE, 1), c, sa, sb, half)
    for hd in range(MLA_HEADS):
        sl = slice(hd * LANE, (hd + 1) * LANE)
        q_ref[:, sl] = (_rope(q[:, sl], c, sa, sb, half) * scale).astype(q_ref.dtype)
        k_ref[:, sl] = (k_nope[:, sl] + kr).astype(k_ref.dtype)
    _store_values_t(vt_ref, _dot_nt(wvt_ref[...], kvn), MLA_HEADS, MLA_V)


def _softmax_t(keys, qs, k_blocks, vt_blocks):
    hs = range(len(qs))
    m, acc = [None] * len(qs), [None] * len(qs)
    blocks = [slice(k0, k0 + KEY_BLOCK) for k0 in range(keys.start, keys.stop, KEY_BLOCK)]
    scores = lambda ks: [_dot_nt(k_blocks[i](ks), qs[i]) for i in hs]
    st_next = scores(blocks[0])
    for bi, ks in enumerate(blocks):
        k0 = ks.start
        st, st_next = st_next, (scores(blocks[bi + 1]) if bi + 1 < len(blocks) else None)
        top = [jnp.max(s, 0, keepdims=True) for s in st]
        if k0 == keys.start:
            m = top
            acc = [jnp.dot(vt_blocks[i](ks), jnp.exp2(st[i] - m[i]).astype(BF16),
                           preferred_element_type=F32) for i in hs]
        else:
            m_new = [jnp.maximum(m[i], top[i]) for i in hs]
            acc = [acc[i] * jnp.exp2(m[i] - m_new[i])
                   + jnp.dot(vt_blocks[i](ks), jnp.exp2(st[i] - m_new[i]).astype(BF16),
                             preferred_element_type=F32) for i in hs]
            m = m_new
    dv = acc[0].shape[0] - V_PAD
    return [a[:dv] / a[dv:dv + 1] for a in acc]


def _key_ranges(body, lat_tiles, seq, nt):
    t = pl.program_id(2)
    pl.when(t < lat_tiles)(lambda: body(slice(0, nt)))
    pl.when(t >= lat_tiles)(lambda: body(slice(seq, nt)))


def _mla_attn_kernel(q_ref, k_ref, vt_ref, o_ref, *, lat_tiles, seq):
    rows = MLA_V + V_PAD
    heads = q_ref.shape[1] // LANE
    lanes = [slice(j * LANE, (j + 1) * LANE) for j in range(heads)]

    def attend(keys):
        outs = _softmax_t(keys, [q_ref[:, sl] for sl in lanes],
                          [lambda ks, sl=sl: k_ref[ks, sl] for sl in lanes],
                          [lambda ks, j=j: vt_ref[j * rows:(j + 1) * rows, ks] for j in range(heads)])
        for j in range(0, heads, 2):
            o_ref[:, j // 2 * LANE:(j // 2 + 1) * LANE] = jnp.concatenate(
                outs[j:j + 2], 0).T.astype(o_ref.dtype)

    _key_ranges(attend, lat_tiles, seq, k_ref.shape[0])


def _attention(kernel, geom, q, k, vt, extra, head_groups, qkw, vrows, ow, name):
    extra_specs = [_const_spec(e.shape) for e in extra]
    return pl.pallas_call(
        functools.partial(kernel, lat_tiles=geom.lat_tiles, seq=geom.seq),
        grid=(geom.batch, head_groups, geom.tiles),
        in_specs=extra_specs + [
            pl.BlockSpec((None, ROW_TILE, qkw), lambda b, g, t: (b, t, g)),
            pl.BlockSpec((None, geom.nt, qkw), lambda b, g, t: (b, 0, g)),
            pl.BlockSpec((None, vrows, geom.nt), lambda b, g, t: (b, g, 0))],
        out_specs=pl.BlockSpec((None, ROW_TILE, ow), lambda b, g, t: (b, t, g)),
        out_shape=jax.ShapeDtypeStruct((geom.batch, geom.nt, head_groups * ow), BF16),
        compiler_params=_cparams("parallel", "parallel", "arbitrary"),
        name=name,
    )(*extra, q, k, vt)


def _mla_layer(geom, hx, mod, w_in, q_norm, kv_norm, w_qb, w_kvb):
    d = D_MODEL
    perm = jnp.array(_deinterleave(MLA_ROPE))
    w_in_p = jnp.concatenate(
        [w_in[:, :MLA_Q_LORA + MLA_KV_LORA], w_in[:, MLA_Q_LORA + MLA_KV_LORA:][:, perm],
         jnp.zeros((d, LANE - MLA_ROPE), F32)], -1).astype(BF16)
    wq = w_qb.reshape(MLA_Q_LORA, MLA_HEADS, MLA_NOPE + MLA_ROPE)
    wq = jnp.concatenate([wq[..., :MLA_NOPE], wq[..., MLA_NOPE:][..., perm],
                          jnp.zeros((MLA_Q_LORA, MLA_HEADS, LANE - MLA_NOPE - MLA_ROPE), F32)], -1)
    wq = wq.reshape(MLA_Q_LORA, MLA_HEADS * LANE).astype(BF16)
    wkv = w_kvb.reshape(MLA_KV_LORA, MLA_HEADS, MLA_NOPE + MLA_V)
    wk = jnp.concatenate([wkv[..., :MLA_NOPE],
                          jnp.zeros((MLA_KV_LORA, MLA_HEADS, LANE - MLA_NOPE), F32)], -1)
    wk = wk.reshape(MLA_KV_LORA, MLA_HEADS * LANE).astype(BF16)
    wvt = wkv[..., MLA_NOPE:].reshape(MLA_KV_LORA, MLA_HEADS * MLA_V).T.astype(BF16)
    tables = _rope_tables(geom, MLA_ROPE, ["-"] * 4 + ["e", "o"] + ["-"] * 2)
    tab_spec = pl.BlockSpec((ROW_TILE, LANE), lambda b, t: (t, 0))
    qk_w = MLA_HEADS * LANE
    vrows = MLA_HEADS * (MLA_V + V_PAD)
    col_spec = lambda rows: pl.BlockSpec((None, rows, ROW_TILE), lambda b, t: (b, 0, t))
    q, k, vt = pl.pallas_call(
        _mla_proj_kernel,
        grid=(geom.batch, geom.tiles),
        in_specs=[geom.row_spec(d), geom.mod_spec(0), geom.mod_spec(1),
                  _const_spec(w_in_p.shape), _const_spec((1, MLA_Q_LORA)),
                  _const_spec((1, MLA_KV_LORA)), _const_spec(wq.shape), _const_spec(wk.shape),
                  _const_spec(wvt.shape), tab_spec, tab_spec, tab_spec],
        out_specs=[geom.row_spec(qk_w), geom.row_spec(qk_w), col_spec(vrows)],
        out_shape=[jax.ShapeDtypeStruct((geom.batch, geom.nt, qk_w), BF16),
                   jax.ShapeDtypeStruct((geom.batch, geom.nt, qk_w), BF16),
                   jax.ShapeDtypeStruct((geom.batch, vrows, geom.nt), BF16)],
        compiler_params=_cparams("parallel", "parallel"),
        name="mla_proj",
    )(hx, mod, mod, w_in_p, q_norm.reshape(1, -1), kv_norm.reshape(1, -1), wq, wk, wvt, *tables)
    hps = 8
    return _attention(_mla_attn_kernel, geom, q, k, vt, [], MLA_HEADS // hps, hps * LANE,
                      hps * (MLA_V + V_PAD), hps * MLA_V, "mla_attn")


def _diff_proj_kernel(h_ref, sh_ref, sc_ref, wqk_ref, wvt_ref, c_ref, sa_ref, sb_ref,
                      q_ref, k_ref, vt_ref):
    u = (h_ref[...] * (1.0 + sc_ref[...]) + sh_ref[...]).astype(BF16)
    t = jnp.dot(u, wqk_ref[...], preferred_element_type=F32)
    c, sa, sb = c_ref[...], sa_ref[...], sb_ref[...]
    half = DIFF_HEAD_DIM // 2
    width = DIFF_HEADS * LANE
    scale = DIFF_HEAD_DIM ** -0.5 * LOG2E
    for hd in range(DIFF_HEADS):
        sl = slice(hd * LANE, (hd + 1) * LANE)
        q_ref[:, sl] = (_rope(t[:, sl], c, sa, sb, half) * scale).astype(q_ref.dtype)
        k_ref[:, sl] = _rope(t[:, width + hd * LANE:width + (hd + 1) * LANE],
                             c, sa, sb, half).astype(k_ref.dtype)
    _store_values_t(vt_ref, _dot_nt(wvt_ref[...], u), DIFF_HEADS, 2 * DIFF_HEAD_DIM)


def _diff_attn_kernel(lam_ref, subln_ref, q_ref, k_ref, vt_ref, o_ref, *, lambda_init,
                      lat_tiles, seq):
    lam_p = lam_ref[...]
    lam = (jnp.exp(jnp.sum(lam_p[0:1] * lam_p[1:2], -1, keepdims=True))
           - jnp.exp(jnp.sum(lam_p[2:3] * lam_p[3:4], -1, keepdims=True)) + lambda_init)
    heads = q_ref.shape[1] // LANE
    rows = 2 * DIFF_HEAD_DIM + V_PAD
    lanes = [slice(j * LANE, (j + 1) * LANE) for j in range(heads)]
    lane = lax.broadcasted_iota(jnp.int32, (q_ref.shape[0], LANE), 1)
    qs = []
    for sl in lanes:
        q = q_ref[:, sl]
        zero = jnp.zeros_like(q)
        qs += [jnp.where(lane < DIFF_HEAD_DIM, q, zero), jnp.where(lane < DIFF_HEAD_DIM, zero, q)]

    def attend(keys):
        outs = _softmax_t(
            keys, qs, [lambda ks, sl=sl: k_ref[ks, sl] for sl in lanes for _ in range(2)],
            [lambda ks, j=j: vt_ref[j * rows:(j + 1) * rows, ks] for j in range(heads)
             for _ in range(2)])
        for j, sl in enumerate(lanes):
            o = outs[2 * j] - lam * outs[2 * j + 1]
            o = o * lax.rsqrt(jnp.mean(jnp.square(o), 0, keepdims=True) + NORM_EPS) * subln_ref[...]
            o_ref[:, sl] = (o * (1.0 - lambda_init)).T.astype(o_ref.dtype)

    _key_ranges(attend, lat_tiles, seq, k_ref.shape[0])


def _diff_layer(geom, hx, mod, w_in, lam_q1, lam_k1, lam_q2, lam_k2, subln, lambda_init):
    d = D_MODEL
    hd = DIFF_HEAD_DIM
    half_perm = _deinterleave(hd)
    head_perm = half_perm + [hd + p for p in half_perm]
    width = DIFF_HEADS * LANE
    qk_perm = jnp.array([h * LANE + p for h in range(DIFF_HEADS) for p in head_perm])
    w_qk = jnp.concatenate([w_in[:, :width][:, qk_perm], w_in[:, width:2 * width][:, qk_perm]],
                           -1).astype(BF16)
    w_vt = w_in[:, 2 * width:].T.astype(BF16)
    tables = _rope_tables(geom, hd, ["e", "o", "e", "o"])
    tab_spec = pl.BlockSpec((ROW_TILE, LANE), lambda b, t: (t, 0))
    out = jax.ShapeDtypeStruct((geom.batch, geom.nt, width), BF16)
    vrows = DIFF_HEADS * (2 * hd + V_PAD)
    col_spec = lambda rows: pl.BlockSpec((None, rows, ROW_TILE), lambda b, t: (b, 0, t))
    q, k, vt = pl.pallas_call(
        _diff_proj_kernel,
        grid=(geom.batch, geom.tiles),
        in_specs=[geom.row_spec(d), geom.mod_spec(0), geom.mod_spec(1), _const_spec(w_qk.shape),
                  _const_spec(w_vt.shape), tab_spec, tab_spec, tab_spec],
        out_specs=[geom.row_spec(width), geom.row_spec(width), col_spec(vrows)],
        out_shape=[out, out, jax.ShapeDtypeStruct((geom.batch, vrows, geom.nt), BF16)],
        compiler_params=_cparams("parallel", "parallel"),
        name="diff_proj",
    )(hx, mod, mod, w_qk, w_vt, *tables)
    lam_p = jnp.stack([lam_q1, lam_k1, lam_q2, lam_k2]).astype(F32)
    kern = functools.partial(_diff_attn_kernel, lambda_init=lambda_init)
    hps = 4
    return _attention(kern, geom, q, k, vt, [lam_p, subln.reshape(-1, 1)], DIFF_HEADS // hps,
                      hps * LANE, hps * (2 * hd + V_PAD), hps * LANE, "diff_attn")


def _chain_block(geom, rev):
    tiles, lat = geom.tiles, geom.lat_tiles
    if rev:
        return lambda j: tiles - 1 - j
    ctx_tiles = tiles - lat
    return lambda j: jnp.where(j < ctx_tiles, j + lat, j - ctx_tiles)


def _gla_proj_kernel(h_ref, sh_ref, sc_ref, win_ref, q_ref, k_ref, v_ref, g_ref, r_ref):
    u = h_ref[...] * (1.0 + sc_ref[...]) + sh_ref[...]
    t = _dot(u, win_ref[...])
    kd, vd = GLA_KEY_DIM, GLA_VALUE_DIM
    q_ref[...] = t[:, :kd] * GLA_DK ** -0.5
    k_ref[...] = t[:, kd:2 * kd]
    v_ref[...] = t[:, 2 * kd:2 * kd + vd].astype(v_ref.dtype)
    g_ref[...] = t[:, 2 * kd + vd:2 * kd + 2 * vd].astype(g_ref.dtype)
    r_ref[...] = t[:, 2 * kd + 2 * vd:]


def _gla_scan_kernel(q_ref, k_ref, v_ref, r_ref, gw_ref, gb_ref, o_ref, st_ref, *, rev):
    @pl.when(pl.program_id(1) == 0)
    def _():
        st_ref[...] = jnp.zeros_like(st_ref)

    n = ROW_TILE
    r = lax.broadcasted_iota(jnp.int32, (n, n), 0)
    c = lax.broadcasted_iota(jnp.int32, (n, n), 1)
    same = (r >> 6) == (c >> 6)
    incl = jnp.logical_and(same, (c >= r) if rev else (r >= c))
    la = _dot(r_ref[...], gw_ref[...]) + gb_ref[...]
    la = -_softplus(-la) / GLA_GATE_NORM
    cum = _dot_exact(incl.astype(F32), la)
    tot = _dot_exact(same.astype(F32), la)
    hs = range(GLA_HEADS)
    ksl = [slice(hd * GLA_DK, (hd + 1) * GLA_DK) for hd in hs]
    vsl = [slice(hd * GLA_DV, (hd + 1) * GLA_DV) for hd in hs]

    q_dec = [(q_ref[:, ksl[hd]] * jnp.exp(cum[:, ksl[hd]])).astype(BF16) for hd in hs]
    k_inv = [(k_ref[:, ksl[hd]] * jnp.exp(-cum[:, ksl[hd]])).astype(BF16) for hd in hs]
    k_end = [(k_ref[:, ksl[hd]] * jnp.exp(tot[:, ksl[hd]] - cum[:, ksl[hd]])).astype(BF16)
             for hd in hs]
    a = [jnp.where(incl, _dot_nt(q_dec[hd], k_inv[hd]), 0.0).astype(BF16) for hd in hs]
    o_intra = [jnp.dot(a[hd], v_ref[:, vsl[hd]], preferred_element_type=F32) for hd in hs]

    n_chunks = ROW_TILE // CHUNK
    st = [st_ref[hd] for hd in hs]
    for ci in (range(n_chunks - 1, -1, -1) if rev else range(n_chunks)):
        rows = slice(ci * CHUNK, (ci + 1) * CHUNK)
        for hd in hs:
            o_ref[rows, vsl[hd]] = (o_intra[hd][rows]
                                    + _dot_nt(q_dec[hd][rows], st[hd])).astype(o_ref.dtype)
        st = [st[hd] * jnp.exp(tot[ci * CHUNK:ci * CHUNK + 1, ksl[hd]])
              + _dot(v_ref[rows, vsl[hd]].astype(F32).T, k_end[hd][rows]) for hd in hs]
    for hd in hs:
        st_ref[hd] = st[hd]


def _gla_layer(geom, hx, mod, w_in, gate_w_fwd, gate_b_fwd, gate_w_bwd, gate_b_bwd):
    d = D_MODEL
    kd, vd, rk = GLA_KEY_DIM, GLA_VALUE_DIM, GLA_GATE_RANK
    main = 2 * kd + 2 * vd
    w_in_p = jnp.concatenate([w_in, jnp.zeros((d, LANE - 2 * rk), F32)], -1).astype(BF16)
    b, nt = geom.batch, geom.nt
    q, k, v, g, r = pl.pallas_call(
        _gla_proj_kernel,
        grid=(b, geom.tiles),
        in_specs=[geom.row_spec(d), geom.mod_spec(0), geom.mod_spec(1), _const_spec(w_in_p.shape)],
        out_specs=[geom.row_spec(kd), geom.row_spec(kd), geom.row_spec(vd), geom.row_spec(vd),
                   geom.row_spec(LANE)],
        out_shape=[jax.ShapeDtypeStruct((b, nt, kd), F32), jax.ShapeDtypeStruct((b, nt, kd), F32),
                   jax.ShapeDtypeStruct((b, nt, vd), BF16), jax.ShapeDtypeStruct((b, nt, vd), BF16),
                   jax.ShapeDtypeStruct((b, nt, LANE), F32)],
        compiler_params=_cparams("parallel", "parallel"),
        name="gla_proj",
    )(hx, mod, mod, w_in_p)
    assert w_in_p.shape[1] == main + LANE

    def scan(rev, gate_w, gate_b, lane_off):
        gw = jnp.zeros((LANE, kd), F32).at[lane_off:lane_off + rk].set(gate_w).astype(BF16)
        blk = _chain_block(geom, rev)
        row = lambda w: pl.BlockSpec((None, ROW_TILE, w), lambda bi, j: (bi, blk(j), 0))
        return pl.pallas_call(
            functools.partial(_gla_scan_kernel, rev=rev),
            grid=(b, geom.tiles),
            in_specs=[row(kd), row(kd), row(vd), row(LANE), _const_spec(gw.shape),
                      _const_spec((1, kd))],
            out_specs=row(vd),
            out_shape=jax.ShapeDtypeStruct((b, nt, vd), BF16),
            scratch_shapes=[pltpu.VMEM((GLA_HEADS, GLA_DV, GLA_DK), F32)],
            compiler_params=_cparams("parallel", "arbitrary"),
            name="gla_scan_bwd" if rev else "gla_scan_fwd",
        )(q, k, v, r, gw, gate_b.reshape(1, kd))

    return scan(False, gate_w_fwd, gate_b_fwd, 0), scan(True, gate_w_bwd, gate_b_bwd, rk), g


def _gdn_proj_kernel(hp_ref, h_ref, hn_ref, sh_ref, sc_ref, wc_ref, wz_ref, ws_ref, wst_ref,
                     cw_ref, q_ref, k_ref, v_ref, z_ref, s_ref, st_ref, *, seq, nt):
    t_idx = pl.program_id(1)
    sc, sh = 1.0 + sc_ref[...], sh_ref[...]
    u = h_ref[...] * sc + sh
    row0 = t_idx * ROW_TILE
    has_prev = jnp.logical_and(row0 != 0, row0 != seq)
    has_next = jnp.logical_and(row0 + ROW_TILE != seq, row0 + ROW_TILE != nt)
    u_prev = jnp.where(has_prev, hp_ref[...] * sc + sh, 0.0)
    u_next = jnp.where(has_next, hn_ref[...] * sc + sh, 0.0)
    u_ext = jnp.concatenate([u_prev, u, u_next], 0).astype(BF16)
    u = u.astype(BF16)
    ext = ROW_TILE + 2 * HALO
    blk = 512
    pad = GDN_CONV // 2
    for cb in range(GDN_CONV_CH // blk):
        cols = slice(cb * blk, (cb + 1) * blk)
        te = jnp.dot(u_ext, wc_ref[:, cols], preferred_element_type=F32)
        acc = None
        for j in range(GDN_CONV):
            shifted = te if j == pad else pltpu.roll(te, (pad - j) % ext, 0)
            term = shifted[HALO:HALO + ROW_TILE] * cw_ref[j:j + 1, cols]
            acc = term if acc is None else acc + term
        y = _silu(acc)
        if cols.start < 2 * GDN_KEY_DIM:
            dst, scale, off = ((q_ref, GDN_HEAD_DIM ** -0.5, 0) if cols.start < GDN_KEY_DIM
                               else (k_ref, 1.0, GDN_KEY_DIM))
            for hh in range(blk // LANE):
                yh = y[:, hh * LANE:(hh + 1) * LANE]
                yh = yh * lax.rsqrt(jnp.sum(jnp.square(yh), -1, keepdims=True) + NORM_EPS)
                c0 = cols.start - off + hh * LANE
                dst[:, c0:c0 + LANE] = yh * scale
        else:
            c0 = cols.start - 2 * GDN_KEY_DIM
            v_ref[:, c0:c0 + blk] = y
    z_ref[...] = jnp.dot(u, wz_ref[...], preferred_element_type=F32).astype(z_ref.dtype)
    s_ref[...] = jnp.dot(u, ws_ref[...], preferred_element_type=F32)
    st_ref[...] = lax.dot_general(wst_ref[...], u, (((1,), (1,)), ((), ())),
                                  preferred_element_type=F32)


def _tri_inverse(lmats, out):
    n = lmats[0].shape[0]
    r = lax.broadcasted_iota(jnp.int32, (n, n), 0)
    c = lax.broadcasted_iota(jnp.int32, (n, n), 1)
    same16 = (r >> 4) == (c >> 4)
    same32 = (r >> 5) == (c >> 5)
    mid32 = jnp.logical_and(same32, jnp.logical_not(same16))
    eye = (r == c).astype(F32)
    d1 = [jnp.where(same16, l, 0.0) for l in lmats]
    xs = [eye - a for a in d1]
    powers = [a.astype(BF16) for a in d1]
    for _ in range(3):
        powers = [_dot(a, a).astype(BF16) for a in powers]
        yield
        xs = [x + _dot(x, p) for x, p in zip(xs, powers)]
        yield
    for off in ([jnp.where(mid32, l, 0.0) for l in lmats],
                [jnp.where(same32, 0.0, l) for l in lmats]):
        xb = [x.astype(BF16) for x in xs]
        ys = [_dot(x, o) for x, o in zip(xb, off)]
        yield
        xs = [x - _dot(y, x16) for x, y, x16 in zip(xs, ys, xb)]
        yield
    out.extend(xs)


def _interleave(*gens):
    live = list(gens)
    while live:
        for g in list(live):
            try:
                next(g)
            except StopIteration:
                live.remove(g)


def _gdn_scan_kernel(q_ref, k_ref, v_ref, s_ref, st_ref, alog_ref, dtb_ref, alogc_ref, dtbc_ref,
                     o_ref, state_ref, *, rev, lane_off, heads):
    @pl.when(pl.program_id(2) == 0)
    def _():
        state_ref[...] = jnp.zeros_like(state_ref)

    grp = pl.program_id(1)
    hv = GDN_VALUE_HEADS
    n = ROW_TILE
    r = lax.broadcasted_iota(jnp.int32, (n, n), 0)
    c = lax.broadcasted_iota(jnp.int32, (n, n), 1)
    same = (r >> 6) == (c >> 6)
    incl = jnp.logical_and(same, (c >= r) if rev else (r >= c))
    incl_f = incl.astype(F32)
    same_f = same.astype(F32)
    off_diag = r != c

    small = s_ref[...]
    small_t = st_ref[...]
    beta_all = jax.nn.sigmoid(small)
    g_all = -jnp.exp(alog_ref[...]) * _softplus(small + dtb_ref[...])
    a_rows = small_t[2 * hv + lane_off:2 * hv + lane_off + hv]
    g_t = -jnp.exp(alogc_ref[...]) * _softplus(a_rows + dtbc_ref[...])
    gc_all = _dot_exact(incl_f, g_all)
    tot_all = _dot_exact(same_f, g_all)
    gct_all = _dot_nt_exact(g_t, incl_f)
    tott_all = _dot_nt_exact(g_t, same_f)
    lane = lax.broadcasted_iota(jnp.int32, (n, LANE), 1)
    sub = lax.broadcasted_iota(jnp.int32, (hv, n), 0)
    hs = range(heads)

    def column(x, idx):
        return jnp.sum(jnp.where(lane == idx, x, 0.0), -1, keepdims=True)

    def row(x, idx):
        return jnp.sum(jnp.where(sub == idx, x, 0.0), 0, keepdims=True)

    ksl = [slice((hh // 2) * LANE, (hh // 2 + 1) * LANE) for hh in hs]
    vsl = [slice(hh * LANE, (hh + 1) * LANE) for hh in hs]
    shared, pre, post = {}, {}, {}

    def prepare(hh):
        kh = hh // 2
        if kh not in shared:
            k_t = k_ref[:, ksl[hh]].T.astype(BF16)
            shared[kh] = (k_t, jnp.dot(q_ref[:, ksl[hh]].astype(BF16), k_t,
                                       preferred_element_type=F32))
        k_t, qk = shared[kh]
        head = grp * heads + hh
        beta = column(beta_all, head + lane_off)
        gc = column(gc_all, head + 2 * hv + lane_off)
        tot = column(tot_all, head + 2 * hv + lane_off)
        gamma = jnp.where(incl, jnp.exp(gc - row(gct_all, head)), 0.0)
        kb = k_ref[:, ksl[hh]] * beta
        e_gc = jnp.exp(gc)
        pre[hh] = dict(
            lmat=jnp.where(off_diag, _dot(kb, k_t) * gamma, 0.0),
            a_intra=(qk * gamma).astype(BF16),
            rhs=jnp.concatenate([v_ref[:, vsl[hh]] * beta, kb * e_gc], -1).astype(BF16),
            q_dec=(q_ref[:, ksl[hh]] * e_gc).astype(BF16),
            k_dec_t=(k_ref[:, ksl[hh]] * jnp.exp(tot - gc)).T.astype(BF16),
            d_last=jnp.exp(row(tott_all, head)))

    def prepare_all(group):
        for hh in group:
            prepare(hh)
            yield

    def invert(group):
        t_inv = []
        yield from _tri_inverse([pre[hh]["lmat"] for hh in group], t_inv)
        for hh, t in zip(group, t_inv):
            post[hh] = _dot(t, pre[hh]["rhs"])
        yield

    def recur(group):
        n_chunks = ROW_TILE // CHUNK
        st = {hh: state_ref[hh] for hh in group}
        zeros = jnp.zeros((CHUNK, LANE), BF16)
        for ci in (range(n_chunks - 1, -1, -1) if rev else range(n_chunks)):
            rows = slice(ci * CHUNK, (ci + 1) * CHUNK)
            pair = slice(ci // 2 * LANE, (ci // 2 + 1) * LANE)
            ws_qs = {hh: jnp.dot(jnp.concatenate([post[hh][rows, LANE:].astype(BF16),
                                                  pre[hh]["q_dec"][rows]], 0),
                                 st[hh].astype(BF16), preferred_element_type=F32) for hh in group}
            yield
            v_new = {hh: (post[hh][rows, :LANE] - ws_qs[hh][:CHUNK]).astype(BF16) for hh in group}
            v_pair = {hh: jnp.concatenate([v_new[hh], zeros] if ci % 2 == 0 else [zeros, v_new[hh]],
                                          0) for hh in group}
            av_kv = {hh: jnp.dot(jnp.concatenate([pre[hh]["a_intra"][rows, pair],
                                                  pre[hh]["k_dec_t"][:, pair]], 0),
                                 v_pair[hh], preferred_element_type=F32) for hh in group}
            yield
            for hh in group:
                o_ref[rows, vsl[hh]] = (ws_qs[hh][CHUNK:] + av_kv[hh][:CHUNK]).astype(o_ref.dtype)
                st[hh] = (st[hh] * pre[hh]["d_last"][:, ci * CHUNK:ci * CHUNK + 1]
                          + av_kv[hh][CHUNK:])
        for hh in group:
            state_ref[hh] = st[hh]

    _interleave(prepare_all(list(hs)))
    _interleave(invert(list(hs)))
    _interleave(recur(list(hs)))


def _gdn_layer(geom, hx, mod, w_in, conv_w, a_log_fwd, dt_bias_fwd, a_log_bwd, dt_bias_bwd):
    d = D_MODEL
    b, nt = geom.batch, geom.nt
    hv = GDN_VALUE_HEADS
    wc = w_in[:, :GDN_CONV_CH].astype(BF16)
    wz = w_in[:, GDN_CONV_CH:GDN_CONV_CH + GDN_VALUE_DIM].astype(BF16)
    ws_t = w_in[:, GDN_CONV_CH + GDN_VALUE_DIM:].T
    ws = jnp.concatenate([ws_t.T, jnp.zeros((d, LANE - 4 * hv), F32)], -1).astype(BF16)
    ws_t = ws_t.astype(BF16)
    halo_per_tile = ROW_TILE // HALO
    last_halo = nt // HALO - 1
    prev_spec = pl.BlockSpec((None, HALO, d),
                             lambda bi, t: (bi, jnp.maximum(t * halo_per_tile - 1, 0), 0))
    next_spec = pl.BlockSpec((None, HALO, d),
                             lambda bi, t: (bi, jnp.minimum((t + 1) * halo_per_tile, last_halo), 0))
    q, k, v, z, small, small_t = pl.pallas_call(
        functools.partial(_gdn_proj_kernel, seq=geom.seq, nt=nt),
        grid=(b, geom.tiles),
        in_specs=[prev_spec, geom.row_spec(d), next_spec, geom.mod_spec(0), geom.mod_spec(1),
                  _const_spec(wc.shape), _const_spec(wz.shape), _const_spec(ws.shape),
                  _const_spec(ws_t.shape), _const_spec(conv_w.shape)],
        out_specs=[geom.row_spec(GDN_KEY_DIM), geom.row_spec(GDN_KEY_DIM),
                   geom.row_spec(GDN_VALUE_DIM), geom.row_spec(GDN_VALUE_DIM), geom.row_spec(LANE),
                   pl.BlockSpec((None, 4 * hv, ROW_TILE), lambda bi, t: (bi, 0, t))],
        out_shape=[jax.ShapeDtypeStruct((b, nt, GDN_KEY_DIM), F32),
                   jax.ShapeDtypeStruct((b, nt, GDN_KEY_DIM), F32),
                   jax.ShapeDtypeStruct((b, nt, GDN_VALUE_DIM), F32),
                   jax.ShapeDtypeStruct((b, nt, GDN_VALUE_DIM), BF16),
                   jax.ShapeDtypeStruct((b, nt, LANE), F32),
                   jax.ShapeDtypeStruct((b, 4 * hv, nt), F32)],
        compiler_params=_cparams("parallel", "parallel"),
        name="gdn_proj",
    )(hx, hx, hx, mod, mod, wc, wz, ws, ws_t, conv_w)

    heads = 16
    groups = hv // heads

    def scan(rev, a_log, dt_bias, lane_off):
        blk = _chain_block(geom, rev)
        place = lambda p: jnp.zeros((1, LANE), F32).at[0, 2 * hv + lane_off:3 * hv + lane_off].set(p)
        return pl.pallas_call(
            functools.partial(_gdn_scan_kernel, rev=rev, lane_off=lane_off, heads=heads),
            grid=(b, groups, geom.tiles),
            in_specs=[
                pl.BlockSpec((None, ROW_TILE, heads // 2 * LANE), lambda bi, g, j: (bi, blk(j), g)),
                pl.BlockSpec((None, ROW_TILE, heads // 2 * LANE), lambda bi, g, j: (bi, blk(j), g)),
                pl.BlockSpec((None, ROW_TILE, heads * LANE), lambda bi, g, j: (bi, blk(j), g)),
                pl.BlockSpec((None, ROW_TILE, LANE), lambda bi, g, j: (bi, blk(j), 0)),
                pl.BlockSpec((None, 4 * hv, ROW_TILE), lambda bi, g, j: (bi, 0, blk(j))),
                _const_spec((1, LANE)), _const_spec((1, LANE)),
                _const_spec((hv, 1)), _const_spec((hv, 1))],
            out_specs=pl.BlockSpec((None, ROW_TILE, heads * LANE),
                                   lambda bi, g, j: (bi, blk(j), g)),
            out_shape=jax.ShapeDtypeStruct((b, nt, GDN_VALUE_DIM), BF16),
            scratch_shapes=[pltpu.VMEM((heads, GDN_HEAD_DIM, GDN_HEAD_DIM), F32)],
            compiler_params=_cparams("parallel", "parallel", "arbitrary"),
            name="gdn_scan_bwd" if rev else "gdn_scan_fwd",
        )(q, k, v, small, small_t, place(a_log), place(dt_bias),
          a_log.reshape(hv, 1), dt_bias.reshape(hv, 1))

    return (scan(False, a_log_fwd, dt_bias_fwd, 0), scan(True, a_log_bwd, dt_bias_bwd, hv), z)


def _residual_norm(h, gate, y, g, b):
    return _layer_norm(ALPHA * h + gate * y, g, b)


def _sublayers_tail(project, h_ref, lat_ref, ctx_ref, ln1g_ref, ln1b_ref, w1_ref, w2_ref,
                    ln2g_ref, ln2b_ref, out_ref, *, seq):
    d = D_MODEL
    n_sub = h_ref.shape[0] // ROW_TILE
    row0 = pl.program_id(1) * h_ref.shape[0]
    quarter = ROW_TILE // 4
    ff_blk = 1024
    y, h1, y2 = {}, {}, {}

    def mods(s):
        m = jnp.where(row0 + s * ROW_TILE >= seq, ctx_ref[...], lat_ref[...])
        return [m[:, k * d:(k + 1) * d] for k in range(6)]

    def proj(s):
        y[s] = project(slice(s * ROW_TILE, (s + 1) * ROW_TILE))
        yield

    def norm1(s):
        gate = mods(s)[2]
        parts = []
        for qi in range(4):
            rows = slice(s * ROW_TILE + qi * quarter, s * ROW_TILE + (qi + 1) * quarter)
            parts.append(_residual_norm(h_ref[rows, :], gate, y[s][qi * quarter:(qi + 1) * quarter],
                                        ln1g_ref[...], ln1b_ref[...]))
            yield
        h1[s] = jnp.concatenate(parts, 0)

    def mlp(s):
        m = mods(s)
        u = (h1[s] * (1.0 + m[4]) + m[3]).astype(BF16)
        acc = None
        for c in range(0, D_FF, ff_blk):
            a = jnp.square(jnp.maximum(
                jnp.dot(u, w1_ref[:, c:c + ff_blk], preferred_element_type=F32), 0.0))
            part = jnp.dot(a.astype(BF16), w2_ref[c:c + ff_blk, :], preferred_element_type=F32)
            acc = part if acc is None else acc + part
            yield
        y2[s] = acc

    def norm2(s):
        gate = mods(s)[5]
        for qi in range(4):
            sl = slice(qi * quarter, (qi + 1) * quarter)
            rows = slice(s * ROW_TILE + qi * quarter, s * ROW_TILE + (qi + 1) * quarter)
            out_ref[rows, :] = _residual_norm(h1[s][sl], gate, y2[s][sl], ln2g_ref[...],
                                              ln2b_ref[...])
            yield

    stages = [proj, norm1, mlp, norm2]
    for step in range(n_sub + len(stages) - 1):
        _interleave(*[stages[step - s](s) for s in range(n_sub) if 0 <= step - s < len(stages)])


def _attn_tail_kernel(o_ref, w_ref, *rest, seq):
    _sublayers_tail(lambda rows: jnp.dot(o_ref[rows, :], w_ref[...], preferred_element_type=F32),
                    *rest, seq=seq)


def _gated_tail_kernel(of_ref, ob_ref, z_ref, ng_ref, w_ref, *rest, head_dim, seq):
    def project(rows):
        o = of_ref[rows, :].astype(F32) + ob_ref[rows, :].astype(F32)
        ng = ng_ref[...]
        parts = [_rms(o[:, c:c + head_dim], ng) for c in range(0, o.shape[1], head_dim)]
        x = jnp.concatenate(parts, -1) * _silu(z_ref[rows, :].astype(F32))
        return _dot(x, w_ref[...])

    _sublayers_tail(project, *rest, seq=seq)


def _tail_call(kernel, geom, rows, tile, acts, mixer_consts, hx, mod, ln1, w1, w2, ln2, name):
    d = D_MODEL
    consts = list(ln1) + [w1, w2] + list(ln2)
    batch = geom.batch
    mod_lat = pl.BlockSpec((None, 1, 6 * d), lambda b, t: (b, 0, 0))
    mod_ctx = pl.BlockSpec((None, 1, 6 * d), lambda b, t: (batch, 0, 0))
    in_specs = ([geom.row_spec(a.shape[-1], tile) for a in acts]
                + [_const_spec(c.shape) for c in mixer_consts]
                + [geom.row_spec(d, tile), mod_lat, mod_ctx]
                + [_const_spec(c.shape) for c in consts])
    return pl.pallas_call(
        functools.partial(kernel, seq=geom.seq),
        grid=(batch, rows // tile),
        in_specs=in_specs,
        out_specs=geom.row_spec(d, tile),
        out_shape=jax.ShapeDtypeStruct((batch, rows, d), F32),
        compiler_params=_cparams("parallel", "parallel"),
        name=name,
    )(*acts, *mixer_consts, hx, mod, mod, *consts)


def kernel(x, c, ctx, c_ctx, ada_w, ada_b, ln1_g, ln1_b, ln2_g, ln2_b, mlp_w1, mlp_w2, mla_w_in, mla_q_norm, mla_kv_norm, mla_w_qb, mla_w_kvb, mla_w_out, diff_w_in, diff_lambda_q1, diff_lambda_k1, diff_lambda_q2, diff_lambda_k2, diff_subln, diff_w_out, gla_w_in, gla_gate_w_fwd, gla_gate_b_fwd, gla_gate_w_bwd, gla_gate_b_bwd, gla_norm, gla_w_out, gdn_w_in, gdn_conv_w, gdn_a_log_fwd, gdn_dt_bias_fwd, gdn_a_log_bwd, gdn_dt_bias_bwd, gdn_norm, gdn_w_out):
    batch, seq, d = x.shape
    geom = _Geom(batch, seq, ctx.shape[1])
    depth = ada_w.shape[0]
    cond_rows = -(-(batch + 1) // 8) * 8
    s = jnp.concatenate([c, c_ctx[None], jnp.zeros((cond_rows - batch - 1, d), F32)], 0)
    mods = _adaln(s, ada_w, ada_b)
    hx = jnp.concatenate([x, ctx], 1)
    for i in range(depth):
        last = i == depth - 1
        rows = geom.seq if last else geom.nt
        tile = next(n * ROW_TILE for n in (3, 2, 1) if rows % (n * ROW_TILE) == 0)
        kind, j = i % 4, i // 4
        mod = mods[i].reshape(cond_rows, 1, 6 * d)
        ln1 = [ln1_g[i].reshape(1, d), ln1_b[i].reshape(1, d)]
        ln2 = [ln2_g[i].reshape(1, d), ln2_b[i].reshape(1, d)]
        tail = functools.partial(_tail_call, hx=hx, mod=mod, ln1=ln1, w1=mlp_w1[i].astype(BF16),
                                 w2=mlp_w2[i].astype(BF16), ln2=ln2)
        if kind == 0:
            o = _mla_layer(geom, hx, mod, mla_w_in[j], mla_q_norm[j], mla_kv_norm[j], mla_w_qb[j],
                           mla_w_kvb[j])
            hx = tail(_attn_tail_kernel, geom, rows, tile, [o], [mla_w_out[j].astype(BF16)],
                      name="mla_tail")
        elif kind == 1:
            lambda_init = 0.8 - 0.6 * math.exp(-0.3 * i)
            o = _diff_layer(geom, hx, mod, diff_w_in[j], diff_lambda_q1[j], diff_lambda_k1[j],
                            diff_lambda_q2[j], diff_lambda_k2[j], diff_subln[j], lambda_init)
            hx = tail(_attn_tail_kernel, geom, rows, tile, [o], [diff_w_out[j].astype(BF16)],
                      name="diff_tail")
        elif kind == 2:
            o_f, o_b, gate = _gla_layer(geom, hx, mod, gla_w_in[j], gla_gate_w_fwd[j],
                                        gla_gate_b_fwd[j], gla_gate_w_bwd[j], gla_gate_b_bwd[j])
            hx = tail(functools.partial(_gated_tail_kernel, head_dim=GLA_DV), geom, rows, tile,
                      [o_f, o_b, gate], [gla_norm[j].reshape(1, -1), gla_w_out[j].astype(BF16)],
                      name="gla_tail")
        else:
            o_f, o_b, gate = _gdn_layer(geom, hx, mod, gdn_w_in[j], gdn_conv_w[j], gdn_a_log_fwd[j],
                                        gdn_dt_bias_fwd[j], gdn_a_log_bwd[j], gdn_dt_bias_bwd[j])
            hx = tail(functools.partial(_gated_tail_kernel, head_dim=GDN_HEAD_DIM), geom, rows, tile,
                      [o_f, o_b, gate], [gdn_norm[j].reshape(1, -1), gdn_w_out[j].astype(BF16)],
                      name="gdn_tail")
    return hx[:, :seq]
```

```python
import functools
import math

import jax
import jax.numpy as jnp
from jax import lax
from jax.experimental import pallas as pl
from jax.experimental.pallas import tpu as pltpu

F32 = jnp.float32
BF16 = jnp.bfloat16

D_MODEL = 1024
DEPTH = 4
GRID_W = 64
D_FF = 4 * D_MODEL
ALPHA = (2 * DEPTH) ** 0.25
NORM_EPS = 1e-6
ROPE_BASE = 10000.0
CHUNK = 64

MLA_HEADS = 16
MLA_NOPE = 64
MLA_ROPE = 32
MLA_V = 64
MLA_KV_LORA = 256
MLA_Q_LORA = 768

DIFF_HEAD_DIM = 64
DIFF_HEADS = 8

GLA_HEADS = 4
GLA_KEY_DIM = 512
GLA_VALUE_DIM = 1024
GLA_DK = 128
GLA_DV = 256
GLA_GATE_RANK = 16
GLA_GATE_NORM = 16.0

GDN_HEAD_DIM = 128
GDN_KEY_HEADS = 8
GDN_VALUE_HEADS = 16
GDN_KEY_DIM = 1024
GDN_VALUE_DIM = 2048
GDN_CONV = 5
GDN_CONV_CH = 4096

LANE = 128
ROW_TILE = 256
HALO = 8
V_PAD = 16
KEY_BLOCK = 256
LOG2E = math.log2(math.e)
VMEM_LIMIT = 56 * 1024 * 1024


def _cparams(*sem):
    return pltpu.CompilerParams(dimension_semantics=sem, vmem_limit_bytes=VMEM_LIMIT)


def _const_spec(shape):
    nd = len(shape)
    return pl.BlockSpec(shape, lambda *_: (0,) * nd, pipeline_mode=pl.Buffered(1))


def _dot(a, b):
    return jnp.dot(a.astype(BF16), b.astype(BF16), preferred_element_type=F32)


def _dot_nt(a, b):
    return lax.dot_general(a.astype(BF16), b.astype(BF16), (((1,), (1,)), ((), ())),
                           preferred_element_type=F32)


def _dot_exact(a, b):
    return jnp.dot(a, b, preferred_element_type=F32, precision=lax.Precision.HIGHEST)


def _dot_nt_exact(a, b):
    return lax.dot_general(a, b, (((1,), (1,)), ((), ())), preferred_element_type=F32,
                           precision=lax.Precision.HIGHEST)


def _rms(x, g):
    return x * lax.rsqrt(jnp.mean(jnp.square(x), -1, keepdims=True) + NORM_EPS) * g


def _layer_norm(z, g, b):
    mu = jnp.mean(z, -1, keepdims=True)
    zc = z - mu
    var = jnp.mean(jnp.square(zc), -1, keepdims=True)
    return zc * lax.rsqrt(var + NORM_EPS) * g + b


def _softplus(x):
    return jnp.maximum(x, 0.0) + jnp.log1p(jnp.exp(-jnp.abs(x)))


def _silu(x):
    return x * jax.nn.sigmoid(x)


def _adaln_kernel(s_ref, w_ref, b_ref, o_ref):
    o_ref[...] = _dot(_silu(s_ref[...]), w_ref[...]) + b_ref[...]


def _adaln(s, ada_w, ada_b):
    depth, d, n = ada_w.shape
    rows = s.shape[0]
    tn = 1536
    return pl.pallas_call(
        _adaln_kernel,
        grid=(depth, n // tn),
        in_specs=[pl.BlockSpec((rows, d), lambda l, j: (0, 0)),
                  pl.BlockSpec((None, d, tn), lambda l, j: (l, 0, j)),
                  pl.BlockSpec((None, 1, tn), lambda l, j: (l, 0, j))],
        out_specs=pl.BlockSpec((None, rows, tn), lambda l, j: (l, 0, j)),
        out_shape=jax.ShapeDtypeStruct((depth, rows, n), F32),
        compiler_params=_cparams("parallel", "parallel"),
        name="adaln",
    )(s, ada_w, ada_b.reshape(depth, 1, n))


class _Geom:
    def __init__(self, batch, seq, ctx):
        assert seq % ROW_TILE == 0 and ctx % ROW_TILE == 0
        self.batch, self.seq, self.ctx = batch, seq, ctx
        self.nt = seq + ctx
        self.lat_tiles = seq // ROW_TILE
        self.tiles = self.nt // ROW_TILE

    def mod_spec(self, k):
        lat_tiles, batch = self.lat_tiles, self.batch
        return pl.BlockSpec((None, 1, D_MODEL),
                            lambda b, t: (jnp.where(t < lat_tiles, b, batch), 0, k))

    def row_spec(self, width, tile=ROW_TILE):
        return pl.BlockSpec((None, tile, width), lambda b, t: (b, t, 0))


def _rope_angles(n, dim):
    rows = n // GRID_W
    row = jnp.repeat(jnp.arange(rows, dtype=F32), GRID_W)
    col = jnp.tile(jnp.arange(GRID_W, dtype=F32), rows)
    n_freq = dim // 4
    inv_freq = ROPE_BASE ** (-jnp.arange(n_freq, dtype=F32) / n_freq)
    ang = jnp.concatenate([row[:, None] * inv_freq, col[:, None] * inv_freq], -1)
    return jnp.cos(ang), jnp.sin(ang)


def _rope_tables(geom, dim, layout):
    cos, sin = _rope_angles(geom.seq, dim)
    one, zero = jnp.ones_like(cos), jnp.zeros_like(cos)
    c = jnp.concatenate([one if g == "-" else cos for g in layout], -1)
    sa = jnp.concatenate([-sin if g == "e" else zero for g in layout], -1)
    sb = jnp.concatenate([sin if g == "o" else zero for g in layout], -1)
    pad = lambda t, v: jnp.concatenate([t, jnp.full((geom.ctx, LANE), v, F32)], 0)
    return pad(c, 1.0), pad(sa, 0.0), pad(sb, 0.0)


def _rope(x, c, sa, sb, w):
    return x * c + pltpu.roll(x, LANE - w, 1) * sa + pltpu.roll(x, w, 1) * sb


def _deinterleave(n):
    return list(range(0, n, 2)) + list(range(1, n, 2))


def _store_values_t(vt_ref, vt, heads, dv):
    sub = lax.broadcasted_iota(jnp.int32, (V_PAD, vt.shape[1]), 0)
    ones = (sub == 0).astype(vt_ref.dtype)
    for hd in range(heads):
        r0 = hd * (dv + V_PAD)
        vt_ref[r0:r0 + dv, :] = vt[hd * dv:(hd + 1) * dv].astype(vt_ref.dtype)
        vt_ref[r0 + dv:r0 + dv + V_PAD, :] = ones


def _mla_proj_kernel(h_ref, sh_ref, sc_ref, win_ref, qn_ref, kvn_ref, wqb_ref, wk_ref, wvt_ref,
                     c_ref, sa_ref, sb_ref, qt_ref, k_ref, vt_ref):
    u = h_ref[...] * (1.0 + sc_ref[...]) + sh_ref[...]
    t = _dot(u, win_ref[...])
    cq, ckv, kr = t[:, :MLA_Q_LORA], t[:, MLA_Q_LORA:MLA_Q_LORA + MLA_KV_LORA], t[:, -LANE:]
    q = _dot(_rms(cq, qn_ref[...]), wqb_ref[...])
    kvn = _rms(ckv, kvn_ref[...]).astype(BF16)
    k_nope = jnp.dot(kvn, wk_ref[...], preferred_element_type=F32)
    c, sa, sb = c_ref[...], sa_ref[...], sb_ref[...]
    half = MLA_ROPE // 2
    scale = (MLA_NOPE + MLA_ROPE) ** -0.5 * LOG2E
    kr = _rope(pltpu.roll(kr, MLA_NOPE, 1), c, sa, sb, half)
    for hd in range(MLA_HEADS):
        sl = slice(hd * LANE, (hd + 1) * LANE)
        qt_ref[sl, :] = (_rope(q[:, sl], c, sa, sb, half) * scale).T.astype(qt_ref.dtype)
        k_ref[:, sl] = (k_nope[:, sl] + kr).astype(k_ref.dtype)
    _store_values_t(vt_ref, _dot_nt(wvt_ref[...], kvn), MLA_HEADS, MLA_V)


def _mla_proj_joining_kernel(x_ref, ctx_ref, *rest, lat_tiles):
    hx_ref = rest[-1]
    hx_ref[...] = jnp.where(pl.program_id(1) < lat_tiles, x_ref[...], ctx_ref[...])
    _mla_proj_kernel(hx_ref, *rest[:-1])


def _softmax_t(keys, qs, k_blocks, vt_blocks, q_transposed=False):
    hs = range(len(qs))
    m, acc = [None] * len(qs), [None] * len(qs)
    blocks = [slice(k0, k0 + KEY_BLOCK) for k0 in range(keys.start, keys.stop, KEY_BLOCK)]
    score = ((lambda k, q: jnp.dot(k, q, preferred_element_type=F32)) if q_transposed
             else _dot_nt)
    scores = lambda ks: [score(k_blocks[i](ks), qs[i]) for i in hs]
    st_next = scores(blocks[0])
    for bi, ks in enumerate(blocks):
        k0 = ks.start
        st, st_next = st_next, (scores(blocks[bi + 1]) if bi + 1 < len(blocks) else None)
        top = [jnp.max(s, 0, keepdims=True) for s in st]
        if k0 == keys.start:
            m = top
            acc = [jnp.dot(vt_blocks[i](ks), jnp.exp2(st[i] - m[i]).astype(BF16),
                           preferred_element_type=F32) for i in hs]
        else:
            m_new = [jnp.maximum(m[i], top[i]) for i in hs]
            acc = [acc[i] * jnp.exp2(m[i] - m_new[i])
                   + jnp.dot(vt_blocks[i](ks), jnp.exp2(st[i] - m_new[i]).astype(BF16),
                             preferred_element_type=F32) for i in hs]
            m = m_new
    dv = acc[0].shape[0] - V_PAD
    return [a[:dv] / a[dv:dv + 1] for a in acc]


def _key_ranges(body, lat_tiles, seq, nt):
    t = pl.program_id(2)
    pl.when(t < lat_tiles)(lambda: body(slice(0, nt)))
    pl.when(t >= lat_tiles)(lambda: body(slice(seq, nt)))


def _mla_attn_kernel(qt_ref, k_ref, vt_ref, o_ref, *, lat_tiles, seq):
    rows = MLA_V + V_PAD
    heads = qt_ref.shape[0] // LANE
    lanes = [slice(j * LANE, (j + 1) * LANE) for j in range(heads)]

    def attend(keys):
        outs = _softmax_t(keys, [qt_ref[sl, :] for sl in lanes],
                          [lambda ks, sl=sl: k_ref[ks, sl] for sl in lanes],
                          [lambda ks, j=j: vt_ref[j * rows:(j + 1) * rows, ks] for j in range(heads)],
                          q_transposed=True)
        for j in range(0, heads, 2):
            o_ref[:, j // 2 * LANE:(j // 2 + 1) * LANE] = jnp.concatenate(
                outs[j:j + 2], 0).T.astype(o_ref.dtype)

    _key_ranges(attend, lat_tiles, seq, k_ref.shape[0])


def _attention(kernel, geom, q, k, vt, extra, head_groups, qkw, vrows, ow, name,
               q_transposed=False):
    extra_specs = [_const_spec(e.shape) for e in extra]
    q_spec = (pl.BlockSpec((None, qkw, ROW_TILE), lambda b, g, t: (b, g, t)) if q_transposed
              else pl.BlockSpec((None, ROW_TILE, qkw), lambda b, g, t: (b, t, g)))
    return pl.pallas_call(
        functools.partial(kernel, lat_tiles=geom.lat_tiles, seq=geom.seq),
        grid=(geom.batch, head_groups, geom.tiles),
        in_specs=extra_specs + [
            q_spec,
            pl.BlockSpec((None, geom.nt, qkw), lambda b, g, t: (b, 0, g)),
            pl.BlockSpec((None, vrows, geom.nt), lambda b, g, t: (b, g, 0))],
        out_specs=pl.BlockSpec((None, ROW_TILE, ow), lambda b, g, t: (b, t, g)),
        out_shape=jax.ShapeDtypeStruct((geom.batch, geom.nt, head_groups * ow), BF16),
        compiler_params=_cparams("parallel", "parallel", "arbitrary"),
        name=name,
    )(*extra, q, k, vt)


def _mla_layer(geom, hx, mod, w_in, q_norm, kv_norm, w_qb, w_kvb):
    d = D_MODEL
    perm = jnp.array(_deinterleave(MLA_ROPE))
    w_in_p = jnp.concatenate(
        [w_in[:, :MLA_Q_LORA + MLA_KV_LORA], w_in[:, MLA_Q_LORA + MLA_KV_LORA:][:, perm],
         jnp.zeros((d, LANE - MLA_ROPE), F32)], -1).astype(BF16)
    wq = w_qb.reshape(MLA_Q_LORA, MLA_HEADS, MLA_NOPE + MLA_ROPE)
    wq = jnp.concatenate([wq[..., :MLA_NOPE], wq[..., MLA_NOPE:][..., perm],
                          jnp.zeros((MLA_Q_LORA, MLA_HEADS, LANE - MLA_NOPE - MLA_ROPE), F32)], -1)
    wq = wq.reshape(MLA_Q_LORA, MLA_HEADS * LANE).astype(BF16)
    wkv = w_kvb.reshape(MLA_KV_LORA, MLA_HEADS, MLA_NOPE + MLA_V)
    wk = jnp.concatenate([wkv[..., :MLA_NOPE],
                          jnp.zeros((MLA_KV_LORA, MLA_HEADS, LANE - MLA_NOPE), F32)], -1)
    wk = wk.reshape(MLA_KV_LORA, MLA_HEADS * LANE).astype(BF16)
    wvt = wkv[..., MLA_NOPE:].reshape(MLA_KV_LORA, MLA_HEADS * MLA_V).T.astype(BF16)
    tables = _rope_tables(geom, MLA_ROPE, ["-"] * 4 + ["e", "o"] + ["-"] * 2)
    tab_spec = pl.BlockSpec((ROW_TILE, LANE), lambda b, t: (t, 0))
    qk_w = MLA_HEADS * LANE
    vrows = MLA_HEADS * (MLA_V + V_PAD)
    col_spec = lambda rows: pl.BlockSpec((None, rows, ROW_TILE), lambda b, t: (b, 0, t))
    lat = geom.lat_tiles
    joining = isinstance(hx, tuple)
    body = functools.partial(_mla_proj_joining_kernel, lat_tiles=lat) if joining else _mla_proj_kernel
    h_specs = ([pl.BlockSpec((None, ROW_TILE, d), lambda b, t: (b, jnp.minimum(t, lat - 1), 0)),
                pl.BlockSpec((None, ROW_TILE, d), lambda b, t: (b, jnp.maximum(t - lat, 0), 0))]
               if joining else [geom.row_spec(d)])
    h_args = list(hx) if joining else [hx]
    outs = pl.pallas_call(
        body,
        grid=(geom.batch, geom.tiles),
        in_specs=h_specs + [geom.mod_spec(0), geom.mod_spec(1),
                            _const_spec(w_in_p.shape), _const_spec((1, MLA_Q_LORA)),
                            _const_spec((1, MLA_KV_LORA)), _const_spec(wq.shape),
                            _const_spec(wk.shape), _const_spec(wvt.shape),
                            tab_spec, tab_spec, tab_spec],
        out_specs=([col_spec(qk_w), geom.row_spec(qk_w), col_spec(vrows)]
                   + ([geom.row_spec(d)] if joining else [])),
        out_shape=([jax.ShapeDtypeStruct((geom.batch, qk_w, geom.nt), BF16),
                    jax.ShapeDtypeStruct((geom.batch, geom.nt, qk_w), BF16),
                    jax.ShapeDtypeStruct((geom.batch, vrows, geom.nt), BF16)]
                   + ([jax.ShapeDtypeStruct((geom.batch, geom.nt, d), F32)] if joining else [])),
        compiler_params=_cparams("parallel", "parallel"),
        name="mla_proj",
    )(*h_args, mod, mod, w_in_p, q_norm.reshape(1, -1), kv_norm.reshape(1, -1), wq, wk, wvt,
      *tables)
    q, k, vt = outs[:3]
    hps = 8
    o = _attention(_mla_attn_kernel, geom, q, k, vt, [], MLA_HEADS // hps, hps * LANE,
                   hps * (MLA_V + V_PAD), hps * MLA_V, "mla_attn", q_transposed=True)
    return o, (outs[3] if joining else hx)


def _diff_proj_kernel(h_ref, sh_ref, sc_ref, wqk_ref, wvt_ref, c_ref, sa_ref, sb_ref,
                      q_ref, k_ref, vt_ref):
    u = (h_ref[...] * (1.0 + sc_ref[...]) + sh_ref[...]).astype(BF16)
    t = jnp.dot(u, wqk_ref[...], preferred_element_type=F32)
    c, sa, sb = c_ref[...], sa_ref[...], sb_ref[...]
    half = DIFF_HEAD_DIM // 2
    width = DIFF_HEADS * LANE
    scale = DIFF_HEAD_DIM ** -0.5 * LOG2E
    for hd in range(DIFF_HEADS):
        sl = slice(hd * LANE, (hd + 1) * LANE)
        q_ref[:, sl] = (_rope(t[:, sl], c, sa, sb, half) * scale).astype(q_ref.dtype)
        k_ref[:, sl] = _rope(t[:, width + hd * LANE:width + (hd + 1) * LANE],
                             c, sa, sb, half).astype(k_ref.dtype)
    _store_values_t(vt_ref, _dot_nt(wvt_ref[...], u), DIFF_HEADS, 2 * DIFF_HEAD_DIM)


def _diff_attn_kernel(lam_ref, subln_ref, q_ref, k_ref, vt_ref, o_ref, *, lambda_init,
                      lat_tiles, seq):
    lam_p = lam_ref[...]
    lam = (jnp.exp(jnp.sum(lam_p[0:1] * lam_p[1:2], -1, keepdims=True))
           - jnp.exp(jnp.sum(lam_p[2:3] * lam_p[3:4], -1, keepdims=True)) + lambda_init)
    heads = q_ref.shape[1] // LANE
    rows = 2 * DIFF_HEAD_DIM + V_PAD
    lanes = [slice(j * LANE, (j + 1) * LANE) for j in range(heads)]
    lane = lax.broadcasted_iota(jnp.int32, (q_ref.shape[0], LANE), 1)
    qs = []
    for sl in lanes:
        q = q_ref[:, sl]
        zero = jnp.zeros_like(q)
        qs += [jnp.where(lane < DIFF_HEAD_DIM, q, zero), jnp.where(lane < DIFF_HEAD_DIM, zero, q)]

    def attend(keys):
        outs = _softmax_t(
            keys, qs, [lambda ks, sl=sl: k_ref[ks, sl] for sl in lanes for _ in range(2)],
            [lambda ks, j=j: vt_ref[j * rows:(j + 1) * rows, ks] for j in range(heads)
             for _ in range(2)])
        for j, sl in enumerate(lanes):
            o = outs[2 * j] - lam * outs[2 * j + 1]
            o = o * lax.rsqrt(jnp.mean(jnp.square(o), 0, keepdims=True) + NORM_EPS) * subln_ref[...]
            o_ref[:, sl] = (o * (1.0 - lambda_init)).T.astype(o_ref.dtype)

    _key_ranges(attend, lat_tiles, seq, k_ref.shape[0])


def _diff_layer(geom, hx, mod, w_in, lam_q1, lam_k1, lam_q2, lam_k2, subln, lambda_init):
    d = D_MODEL
    hd = DIFF_HEAD_DIM
    half_perm = _deinterleave(hd)
    head_perm = half_perm + [hd + p for p in half_perm]
    width = DIFF_HEADS * LANE
    qk_perm = jnp.array([h * LANE + p for h in range(DIFF_HEADS) for p in head_perm])
    w_qk = jnp.concatenate([w_in[:, :width][:, qk_perm], w_in[:, width:2 * width][:, qk_perm]],
                           -1).astype(BF16)
    w_vt = w_in[:, 2 * width:].T.astype(BF16)
    tables = _rope_tables(geom, hd, ["e", "o", "e", "o"])
    tab_spec = pl.BlockSpec((ROW_TILE, LANE), lambda b, t: (t, 0))
    out = jax.ShapeDtypeStruct((geom.batch, geom.nt, width), BF16)
    vrows = DIFF_HEADS * (2 * hd + V_PAD)
    col_spec = lambda rows: pl.BlockSpec((None, rows, ROW_TILE), lambda b, t: (b, 0, t))
    q, k, vt = pl.pallas_call(
        _diff_proj_kernel,
        grid=(geom.batch, geom.tiles),
        in_specs=[geom.row_spec(d), geom.mod_spec(0), geom.mod_spec(1), _const_spec(w_qk.shape),
                  _const_spec(w_vt.shape), tab_spec, tab_spec, tab_spec],
        out_specs=[geom.row_spec(width), geom.row_spec(width), col_spec(vrows)],
        out_shape=[out, out, jax.ShapeDtypeStruct((geom.batch, vrows, geom.nt), BF16)],
        compiler_params=_cparams("parallel", "parallel"),
        name="diff_proj",
    )(hx, mod, mod, w_qk, w_vt, *tables)
    lam_p = jnp.stack([lam_q1, lam_k1, lam_q2, lam_k2]).astype(F32)
    kern = functools.partial(_diff_attn_kernel, lambda_init=lambda_init)
    hps = 4
    return _attention(kern, geom, q, k, vt, [lam_p, subln.reshape(-1, 1)], DIFF_HEADS // hps,
                      hps * LANE, hps * (2 * hd + V_PAD), hps * LANE, "diff_attn")


def _chain_block(geom, rev):
    tiles, lat = geom.tiles, geom.lat_tiles
    if rev:
        return lambda j: tiles - 1 - j
    ctx_tiles = tiles - lat
    return lambda j: jnp.where(j < ctx_tiles, j + lat, j - ctx_tiles)


def _gla_proj_kernel(h_ref, sh_ref, sc_ref, win_ref, q_ref, k_ref, v_ref, g_ref, r_ref):
    u = h_ref[...] * (1.0 + sc_ref[...]) + sh_ref[...]
    t = _dot(u, win_ref[...])
    kd, vd = GLA_KEY_DIM, GLA_VALUE_DIM
    q_ref[...] = t[:, :kd] * GLA_DK ** -0.5
    k_ref[...] = t[:, kd:2 * kd]
    v_ref[...] = t[:, 2 * kd:2 * kd + vd].astype(v_ref.dtype)
    g_ref[...] = t[:, 2 * kd + vd:2 * kd + 2 * vd].astype(g_ref.dtype)
    r_ref[...] = t[:, 2 * kd + 2 * vd:]


def _gla_scan_kernel(q_ref, k_ref, v_ref, r_ref, gw_ref, gb_ref, o_ref, st_ref, *, rev):
    @pl.when(pl.program_id(1) == 0)
    def _():
        st_ref[...] = jnp.zeros_like(st_ref)

    n = ROW_TILE
    r = lax.broadcasted_iota(jnp.int32, (n, n), 0)
    c = lax.broadcasted_iota(jnp.int32, (n, n), 1)
    same = (r >> 6) == (c >> 6)
    incl = jnp.logical_and(same, (c >= r) if rev else (r >= c))
    la = _dot(r_ref[...], gw_ref[...]) + gb_ref[...]
    la = -_softplus(-la) / GLA_GATE_NORM
    cum = _dot_exact(incl.astype(F32), la)
    tot = _dot_exact(same.astype(F32), la)
    hs = range(GLA_HEADS)
    ksl = [slice(hd * GLA_DK, (hd + 1) * GLA_DK) for hd in hs]
    vsl = [slice(hd * GLA_DV, (hd + 1) * GLA_DV) for hd in hs]

    q_dec = [(q_ref[:, ksl[hd]] * jnp.exp(cum[:, ksl[hd]])).astype(BF16) for hd in hs]
    k_inv = [(k_ref[:, ksl[hd]] * jnp.exp(-cum[:, ksl[hd]])).astype(BF16) for hd in hs]
    k_end = [(k_ref[:, ksl[hd]] * jnp.exp(tot[:, ksl[hd]] - cum[:, ksl[hd]])).astype(BF16)
             for hd in hs]
    a = [jnp.where(incl, _dot_nt(q_dec[hd], k_inv[hd]), 0.0).astype(BF16) for hd in hs]
    o_intra = [jnp.dot(a[hd], v_ref[:, vsl[hd]], preferred_element_type=F32) for hd in hs]

    n_chunks = ROW_TILE // CHUNK
    st = [st_ref[hd] for hd in hs]
    for ci in (range(n_chunks - 1, -1, -1) if rev else range(n_chunks)):
        rows = slice(ci * CHUNK, (ci + 1) * CHUNK)
        for hd in hs:
            o_ref[rows, vsl[hd]] = (o_intra[hd][rows]
                                    + _dot_nt(q_dec[hd][rows], st[hd])).astype(o_ref.dtype)
        st = [st[hd] * jnp.exp(tot[ci * CHUNK:ci * CHUNK + 1, ksl[hd]])
              + _dot(v_ref[rows, vsl[hd]].astype(F32).T, k_end[hd][rows]) for hd in hs]
    for hd in hs:
        st_ref[hd] = st[hd]


def _gla_layer(geom, hx, mod, w_in, gate_w_fwd, gate_b_fwd, gate_w_bwd, gate_b_bwd):
    d = D_MODEL
    kd, vd, rk = GLA_KEY_DIM, GLA_VALUE_DIM, GLA_GATE_RANK
    main = 2 * kd + 2 * vd
    w_in_p = jnp.concatenate([w_in, jnp.zeros((d, LANE - 2 * rk), F32)], -1).astype(BF16)
    b, nt = geom.batch, geom.nt
    q, k, v, g, r = pl.pallas_call(
        _gla_proj_kernel,
        grid=(b, geom.tiles),
        in_specs=[geom.row_spec(d), geom.mod_spec(0), geom.mod_spec(1), _const_spec(w_in_p.shape)],
        out_specs=[geom.row_spec(kd), geom.row_spec(kd), geom.row_spec(vd), geom.row_spec(vd),
                   geom.row_spec(LANE)],
        out_shape=[jax.ShapeDtypeStruct((b, nt, kd), F32), jax.ShapeDtypeStruct((b, nt, kd), F32),
                   jax.ShapeDtypeStruct((b, nt, vd), BF16), jax.ShapeDtypeStruct((b, nt, vd), BF16),
                   jax.ShapeDtypeStruct((b, nt, LANE), F32)],
        compiler_params=_cparams("parallel", "parallel"),
        name="gla_proj",
    )(hx, mod, mod, w_in_p)
    assert w_in_p.shape[1] == main + LANE

    def scan(rev, gate_w, gate_b, lane_off):
        gw = jnp.zeros((LANE, kd), F32).at[lane_off:lane_off + rk].set(gate_w).astype(BF16)
        blk = _chain_block(geom, rev)
        row = lambda w: pl.BlockSpec((None, ROW_TILE, w), lambda bi, j: (bi, blk(j), 0))
        return pl.pallas_call(
            functools.partial(_gla_scan_kernel, rev=rev),
            grid=(b, geom.tiles),
            in_specs=[row(kd), row(kd), row(vd), row(LANE), _const_spec(gw.shape),
                      _const_spec((1, kd))],
            out_specs=row(vd),
            out_shape=jax.ShapeDtypeStruct((b, nt, vd), BF16),
            scratch_shapes=[pltpu.VMEM((GLA_HEADS, GLA_DV, GLA_DK), F32)],
            compiler_params=_cparams("parallel", "arbitrary"),
            name="gla_scan_bwd" if rev else "gla_scan_fwd",
        )(q, k, v, r, gw, gate_b.reshape(1, kd))

    return scan(False, gate_w_fwd, gate_b_fwd, 0), scan(True, gate_w_bwd, gate_b_bwd, rk), g


def _gdn_proj_kernel(hp_ref, h_ref, hn_ref, sh_ref, sc_ref, wc_ref, wz_ref, ws_ref, wst_ref,
                     cw_ref, q_ref, k_ref, v_ref, z_ref, s_ref, st_ref, *, seq, nt):
    t_idx = pl.program_id(1)
    sc, sh = 1.0 + sc_ref[...], sh_ref[...]
    u = h_ref[...] * sc + sh
    row0 = t_idx * ROW_TILE
    has_prev = jnp.logical_and(row0 != 0, row0 != seq)
    has_next = jnp.logical_and(row0 + ROW_TILE != seq, row0 + ROW_TILE != nt)
    u_prev = jnp.where(has_prev, hp_ref[...] * sc + sh, 0.0)
    u_next = jnp.where(has_next, hn_ref[...] * sc + sh, 0.0)
    u_ext = jnp.concatenate([u_prev, u, u_next], 0).astype(BF16)
    u = u.astype(BF16)
    ext = ROW_TILE + 2 * HALO
    blk = 512
    pad = GDN_CONV // 2
    for cb in range(GDN_CONV_CH // blk):
        cols = slice(cb * blk, (cb + 1) * blk)
        te = jnp.dot(u_ext, wc_ref[:, cols], preferred_element_type=F32)
        acc = None
        for j in range(GDN_CONV):
            shifted = te if j == pad else pltpu.roll(te, (pad - j) % ext, 0)
            term = shifted[HALO:HALO + ROW_TILE] * cw_ref[j:j + 1, cols]
            acc = term if acc is None else acc + term
        y = _silu(acc)
        if cols.start < 2 * GDN_KEY_DIM:
            dst, scale, off = ((q_ref, GDN_HEAD_DIM ** -0.5, 0) if cols.start < GDN_KEY_DIM
                               else (k_ref, 1.0, GDN_KEY_DIM))
            for hh in range(blk // LANE):
                yh = y[:, hh * LANE:(hh + 1) * LANE]
                yh = yh * lax.rsqrt(jnp.sum(jnp.square(yh), -1, keepdims=True) + NORM_EPS)
                c0 = cols.start - off + hh * LANE
                dst[:, c0:c0 + LANE] = yh * scale
        else:
            c0 = cols.start - 2 * GDN_KEY_DIM
            v_ref[:, c0:c0 + blk] = y
    z_ref[...] = jnp.dot(u, wz_ref[...], preferred_element_type=F32).astype(z_ref.dtype)
    s_ref[...] = jnp.dot(u, ws_ref[...], preferred_element_type=F32)
    st_ref[...] = lax.dot_general(wst_ref[...], u, (((1,), (1,)), ((), ())),
                                  preferred_element_type=F32)


def _tri_inverse(lmats, out):
    n = lmats[0].shape[0]
    r = lax.broadcasted_iota(jnp.int32, (n, n), 0)
    c = lax.broadcasted_iota(jnp.int32, (n, n), 1)
    same16 = (r >> 4) == (c >> 4)
    same32 = (r >> 5) == (c >> 5)
    mid32 = jnp.logical_and(same32, jnp.logical_not(same16))
    eye = (r == c).astype(F32)
    d1 = [jnp.where(same16, l, 0.0) for l in lmats]
    xs = [eye - a for a in d1]
    powers = [a.astype(BF16) for a in d1]
    for _ in range(3):
        powers = [_dot(a, a).astype(BF16) for a in powers]
        yield
        xs = [x + _dot(x, p) for x, p in zip(xs, powers)]
        yield
    for off in ([jnp.where(mid32, l, 0.0) for l in lmats],
                [jnp.where(same32, 0.0, l) for l in lmats]):
        xb = [x.astype(BF16) for x in xs]
        ys = [_dot(x, o) for x, o in zip(xb, off)]
        yield
        xs = [x - _dot(y, x16) for x, y, x16 in zip(xs, ys, xb)]
        yield
    out.extend(xs)


def _interleave(*gens):
    live = list(gens)
    while live:
        for g in list(live):
            try:
                next(g)
            except StopIteration:
                live.remove(g)


def _gdn_scan_kernel(q_ref, k_ref, v_ref, s_ref, st_ref, alog_ref, dtb_ref, alogc_ref, dtbc_ref,
                     o_ref, state_ref, *, rev, lane_off, heads):
    @pl.when(pl.program_id(2) == 0)
    def _():
        state_ref[...] = jnp.zeros_like(state_ref)

    grp = pl.program_id(1)
    hv = GDN_VALUE_HEADS
    n = ROW_TILE
    r = lax.broadcasted_iota(jnp.int32, (n, n), 0)
    c = lax.broadcasted_iota(jnp.int32, (n, n), 1)
    same = (r >> 6) == (c >> 6)
    incl = jnp.logical_and(same, (c >= r) if rev else (r >= c))
    incl_f = incl.astype(F32)
    same_f = same.astype(F32)
    off_diag = r != c

    small = s_ref[...]
    small_t = st_ref[...]
    beta_all = jax.nn.sigmoid(small)
    g_all = -jnp.exp(alog_ref[...]) * _softplus(small + dtb_ref[...])
    a_rows = small_t[2 * hv + lane_off:2 * hv + lane_off + hv]
    g_t = -jnp.exp(alogc_ref[...]) * _softplus(a_rows + dtbc_ref[...])
    gc_all = _dot_exact(incl_f, g_all)
    tot_all = _dot_exact(same_f, g_all)
    gct_all = _dot_nt_exact(g_t, incl_f)
    tott_all = _dot_nt_exact(g_t, same_f)
    lane = lax.broadcasted_iota(jnp.int32, (n, LANE), 1)
    sub = lax.broadcasted_iota(jnp.int32, (hv, n), 0)
    hs = range(heads)

    def column(x, idx):
        return jnp.sum(jnp.where(lane == idx, x, 0.0), -1, keepdims=True)

    def row(x, idx):
        return jnp.sum(jnp.where(sub == idx, x, 0.0), 0, keepdims=True)

    ksl = [slice((hh // 2) * LANE, (hh // 2 + 1) * LANE) for hh in hs]
    vsl = [slice(hh * LANE, (hh + 1) * LANE) for hh in hs]
    shared, pre, post = {}, {}, {}

    def prepare(hh):
        kh = hh // 2
        if kh not in shared:
            k_t = k_ref[:, ksl[hh]].T.astype(BF16)
            kq = jnp.concatenate([k_ref[:, ksl[hh]], q_ref[:, ksl[hh]]], 0).astype(BF16)
            shared[kh] = jnp.dot(kq, k_t, preferred_element_type=F32)
        kk, qk = shared[kh][:ROW_TILE], shared[kh][ROW_TILE:]
        head = grp * heads + hh
        beta = column(beta_all, head + lane_off)
        gc = column(gc_all, head + 2 * hv + lane_off)
        tot = column(tot_all, head + 2 * hv + lane_off)
        gamma = jnp.where(incl, jnp.exp(gc - row(gct_all, head)), 0.0)
        kb = k_ref[:, ksl[hh]] * beta
        e_gc = jnp.exp(gc)
        pre[hh] = dict(
            lmat=jnp.where(off_diag, kk * beta * gamma, 0.0),
            a_intra=(qk * gamma).astype(BF16),
            rhs=jnp.concatenate([v_ref[:, vsl[hh]] * beta, kb * e_gc], -1).astype(BF16),
            q_dec=(q_ref[:, ksl[hh]] * e_gc).astype(BF16),
            k_dec_t=(k_ref[:, ksl[hh]] * jnp.exp(tot - gc)).T.astype(BF16),
            d_last=jnp.exp(row(tott_all, head)))

    def prepare_all(group):
        for hh in group:
            prepare(hh)
            yield

    def invert(group):
        t_inv = []
        yield from _tri_inverse([pre[hh]["lmat"] for hh in group], t_inv)
        for hh, t in zip(group, t_inv):
            post[hh] = _dot(t, pre[hh]["rhs"])
        yield

    def recur(group):
        n_chunks = ROW_TILE // CHUNK
        st = {hh: state_ref[hh] for hh in group}
        zeros = jnp.zeros((CHUNK, LANE), BF16)
        for ci in (range(n_chunks - 1, -1, -1) if rev else range(n_chunks)):
            rows = slice(ci * CHUNK, (ci + 1) * CHUNK)
            pair = slice(ci // 2 * LANE, (ci // 2 + 1) * LANE)
            ws_qs = {hh: jnp.dot(jnp.concatenate([post[hh][rows, LANE:].astype(BF16),
                                                  pre[hh]["q_dec"][rows]], 0),
                                 st[hh].astype(BF16), preferred_element_type=F32) for hh in group}
            yield
            v_new = {hh: (post[hh][rows, :LANE] - ws_qs[hh][:CHUNK]).astype(BF16) for hh in group}
            v_pair = {hh: jnp.concatenate([v_new[hh], zeros] if ci % 2 == 0 else [zeros, v_new[hh]],
                                          0) for hh in group}
            av_kv = {hh: jnp.dot(jnp.concatenate([pre[hh]["a_intra"][rows, pair],
                                                  pre[hh]["k_dec_t"][:, pair]], 0),
                                 v_pair[hh], preferred_element_type=F32) for hh in group}
            yield
            for hh in group:
                o_ref[rows, vsl[hh]] = (ws_qs[hh][CHUNK:] + av_kv[hh][:CHUNK]).astype(o_ref.dtype)
                st[hh] = (st[hh] * pre[hh]["d_last"][:, ci * CHUNK:ci * CHUNK + 1]
                          + av_kv[hh][CHUNK:])
        for hh in group:
            state_ref[hh] = st[hh]

    _interleave(prepare_all(list(hs)))
    _interleave(invert(list(hs)))
    _interleave(recur(list(hs)))


def _gdn_layer(geom, hx, mod, w_in, conv_w, a_log_fwd, dt_bias_fwd, a_log_bwd, dt_bias_bwd):
    d = D_MODEL
    b, nt = geom.batch, geom.nt
    hv = GDN_VALUE_HEADS
    wc = w_in[:, :GDN_CONV_CH].astype(BF16)
    wz = w_in[:, GDN_CONV_CH:GDN_CONV_CH + GDN_VALUE_DIM].astype(BF16)
    ws_t = w_in[:, GDN_CONV_CH + GDN_VALUE_DIM:].T
    ws = jnp.concatenate([ws_t.T, jnp.zeros((d, LANE - 4 * hv), F32)], -1).astype(BF16)
    ws_t = ws_t.astype(BF16)
    halo_per_tile = ROW_TILE // HALO
    last_halo = nt // HALO - 1
    prev_spec = pl.BlockSpec((None, HALO, d),
                             lambda bi, t: (bi, jnp.maximum(t * halo_per_tile - 1, 0), 0))
    next_spec = pl.BlockSpec((None, HALO, d),
                             lambda bi, t: (bi, jnp.minimum((t + 1) * halo_per_tile, last_halo), 0))
    q, k, v, z, small, small_t = pl.pallas_call(
        functools.partial(_gdn_proj_kernel, seq=geom.seq, nt=nt),
        grid=(b, geom.tiles),
        in_specs=[prev_spec, geom.row_spec(d), next_spec, geom.mod_spec(0), geom.mod_spec(1),
                  _const_spec(wc.shape), _const_spec(wz.shape), _const_spec(ws.shape),
                  _const_spec(ws_t.shape), _const_spec(conv_w.shape)],
        out_specs=[geom.row_spec(GDN_KEY_DIM), geom.row_spec(GDN_KEY_DIM),
                   geom.row_spec(GDN_VALUE_DIM), geom.row_spec(GDN_VALUE_DIM), geom.row_spec(LANE),
                   pl.BlockSpec((None, 4 * hv, ROW_TILE), lambda bi, t: (bi, 0, t))],
        out_shape=[jax.ShapeDtypeStruct((b, nt, GDN_KEY_DIM), F32),
                   jax.ShapeDtypeStruct((b, nt, GDN_KEY_DIM), F32),
                   jax.ShapeDtypeStruct((b, nt, GDN_VALUE_DIM), F32),
                   jax.ShapeDtypeStruct((b, nt, GDN_VALUE_DIM), BF16),
                   jax.ShapeDtypeStruct((b, nt, LANE), F32),
                   jax.ShapeDtypeStruct((b, 4 * hv, nt), F32)],
        compiler_params=_cparams("parallel", "parallel"),
        name="gdn_proj",
    )(hx, hx, hx, mod, mod, wc, wz, ws, ws_t, conv_w)

    heads = 16
    groups = hv // heads

    def scan(rev, a_log, dt_bias, lane_off):
        blk = _chain_block(geom, rev)
        place = lambda p: jnp.zeros((1, LANE), F32).at[0, 2 * hv + lane_off:3 * hv + lane_off].set(p)
        return pl.pallas_call(
            functools.partial(_gdn_scan_kernel, rev=rev, lane_off=lane_off, heads=heads),
            grid=(b, groups, geom.tiles),
            in_specs=[
                pl.BlockSpec((None, ROW_TILE, heads // 2 * LANE), lambda bi, g, j: (bi, blk(j), g)),
                pl.BlockSpec((None, ROW_TILE, heads // 2 * LANE), lambda bi, g, j: (bi, blk(j), g)),
                pl.BlockSpec((None, ROW_TILE, heads * LANE), lambda bi, g, j: (bi, blk(j), g)),
                pl.BlockSpec((None, ROW_TILE, LANE), lambda bi, g, j: (bi, blk(j), 0)),
                pl.BlockSpec((None, 4 * hv, ROW_TILE), lambda bi, g, j: (bi, 0, blk(j))),
                _const_spec((1, LANE)), _const_spec((1, LANE)),
                _const_spec((hv, 1)), _const_spec((hv, 1))],
            out_specs=pl.BlockSpec((None, ROW_TILE, heads * LANE),
                                   lambda bi, g, j: (bi, blk(j), g)),
            out_shape=jax.ShapeDtypeStruct((b, nt, GDN_VALUE_DIM), BF16),
            scratch_shapes=[pltpu.VMEM((heads, GDN_HEAD_DIM, GDN_HEAD_DIM), F32)],
            compiler_params=_cparams("parallel", "parallel", "arbitrary"),
            name="gdn_scan_bwd" if rev else "gdn_scan_fwd",
        )(q, k, v, small, small_t, place(a_log), place(dt_bias),
          a_log.reshape(hv, 1), dt_bias.reshape(hv, 1))

    return (scan(False, a_log_fwd, dt_bias_fwd, 0), scan(True, a_log_bwd, dt_bias_bwd, hv), z)


def _residual_norm(h, gate, y, g, b):
    return _layer_norm(ALPHA * h + gate * y, g, b)


def _sublayers_tail(project, h_ref, lat_ref, ctx_ref, ln1g_ref, ln1b_ref, w1_ref, w2_ref,
                    ln2g_ref, ln2b_ref, out_ref, *, seq):
    d = D_MODEL
    n_sub = h_ref.shape[0] // ROW_TILE
    row0 = pl.program_id(1) * h_ref.shape[0]
    quarter = ROW_TILE // 4
    ff_blk = 1024
    y, h1, y2 = {}, {}, {}

    def mods(s):
        m = jnp.where(row0 + s * ROW_TILE >= seq, ctx_ref[...], lat_ref[...])
        return [m[:, k * d:(k + 1) * d] for k in range(6)]

    def proj(s):
        y[s] = project(slice(s * ROW_TILE, (s + 1) * ROW_TILE))
        yield

    def norm1(s):
        gate = mods(s)[2]
        parts = []
        for qi in range(4):
            rows = slice(s * ROW_TILE + qi * quarter, s * ROW_TILE + (qi + 1) * quarter)
            parts.append(_residual_norm(h_ref[rows, :], gate, y[s][qi * quarter:(qi + 1) * quarter],
                                        ln1g_ref[...], ln1b_ref[...]))
            yield
        h1[s] = jnp.concatenate(parts, 0)

    def mlp(s):
        m = mods(s)
        u = (h1[s] * (1.0 + m[4]) + m[3]).astype(BF16)
        acc = None
        for c in range(0, D_FF, ff_blk):
            a = jnp.square(jnp.maximum(
                jnp.dot(u, w1_ref[:, c:c + ff_blk], preferred_element_type=F32), 0.0))
            part = jnp.dot(a.astype(BF16), w2_ref[c:c + ff_blk, :], preferred_element_type=F32)
            acc = part if acc is None else acc + part
            yield
        y2[s] = acc

    def norm2(s):
        gate = mods(s)[5]
        for qi in range(4):
            sl = slice(qi * quarter, (qi + 1) * quarter)
            rows = slice(s * ROW_TILE + qi * quarter, s * ROW_TILE + (qi + 1) * quarter)
            out_ref[rows, :] = _residual_norm(h1[s][sl], gate, y2[s][sl], ln2g_ref[...],
                                              ln2b_ref[...])
            yield

    stages = [proj, norm1, mlp, norm2]
    for step in range(n_sub + len(stages) - 1):
        _interleave(*[stages[step - s](s) for s in range(n_sub) if 0 <= step - s < len(stages)])


def _attn_tail_kernel(o_ref, w_ref, *rest, seq):
    _sublayers_tail(lambda rows: jnp.dot(o_ref[rows, :], w_ref[...], preferred_element_type=F32),
                    *rest, seq=seq)


def _gated_tail_kernel(of_ref, ob_ref, z_ref, ng_ref, w_ref, *rest, head_dim, seq):
    def project(rows):
        o = of_ref[rows, :].astype(F32) + ob_ref[rows, :].astype(F32)
        ng = ng_ref[...]
        parts = [_rms(o[:, c:c + head_dim], ng) for c in range(0, o.shape[1], head_dim)]
        x = jnp.concatenate(parts, -1) * _silu(z_ref[rows, :].astype(F32))
        return _dot(x, w_ref[...])

    _sublayers_tail(project, *rest, seq=seq)


def _tail_call(kernel, geom, rows, tile, acts, mixer_consts, hx, mod, ln1, w1, w2, ln2, name):
    d = D_MODEL
    consts = list(ln1) + [w1, w2] + list(ln2)
    batch = geom.batch
    mod_lat = pl.BlockSpec((None, 1, 6 * d), lambda b, t: (b, 0, 0))
    mod_ctx = pl.BlockSpec((None, 1, 6 * d), lambda b, t: (batch, 0, 0))
    in_specs = ([geom.row_spec(a.shape[-1], tile) for a in acts]
                + [_const_spec(c.shape) for c in mixer_consts]
                + [geom.row_spec(d, tile), mod_lat, mod_ctx]
                + [_const_spec(c.shape) for c in consts])
    return pl.pallas_call(
        functools.partial(kernel, seq=geom.seq),
        grid=(batch, rows // tile),
        in_specs=in_specs,
        out_specs=geom.row_spec(d, tile),
        out_shape=jax.ShapeDtypeStruct((batch, rows, d), F32),
        compiler_params=_cparams("parallel", "parallel"),
        name=name,
    )(*acts, *mixer_consts, hx, mod, mod, *consts)


def kernel(x, c, ctx, c_ctx, ada_w, ada_b, ln1_g, ln1_b, ln2_g, ln2_b, mlp_w1, mlp_w2, mla_w_in, mla_q_norm, mla_kv_norm, mla_w_qb, mla_w_kvb, mla_w_out, diff_w_in, diff_lambda_q1, diff_lambda_k1, diff_lambda_q2, diff_lambda_k2, diff_subln, diff_w_out, gla_w_in, gla_gate_w_fwd, gla_gate_b_fwd, gla_gate_w_bwd, gla_gate_b_bwd, gla_norm, gla_w_out, gdn_w_in, gdn_conv_w, gdn_a_log_fwd, gdn_dt_bias_fwd, gdn_a_log_bwd, gdn_dt_bias_bwd, gdn_norm, gdn_w_out):
    batch, seq, d = x.shape
    geom = _Geom(batch, seq, ctx.shape[1])
    depth = ada_w.shape[0]
    cond_rows = -(-(batch + 1) // 8) * 8
    s = jnp.concatenate([c, c_ctx[None], jnp.zeros((cond_rows - batch - 1, d), F32)], 0)
    mods = _adaln(s, ada_w, ada_b)
    hx = (x, ctx)
    for i in range(depth):
        last = i == depth - 1
        rows = geom.seq if last else geom.nt
        tile = next(n * ROW_TILE for n in (3, 2, 1) if rows % (n * ROW_TILE) == 0)
        kind, j = i % 4, i // 4
        mod = mods[i].reshape(cond_rows, 1, 6 * d)
        ln1 = [ln1_g[i].reshape(1, d), ln1_b[i].reshape(1, d)]
        ln2 = [ln2_g[i].reshape(1, d), ln2_b[i].reshape(1, d)]
        if kind == 0:
            o, hx = _mla_layer(geom, hx, mod, mla_w_in[j], mla_q_norm[j], mla_kv_norm[j],
                               mla_w_qb[j], mla_w_kvb[j])
        elif isinstance(hx, tuple):
            hx = jnp.concatenate(hx, 1)
        tail = functools.partial(_tail_call, hx=hx, mod=mod, ln1=ln1, w1=mlp_w1[i].astype(BF16),
                                 w2=mlp_w2[i].astype(BF16), ln2=ln2)
        if kind == 0:
            hx = tail(_attn_tail_kernel, geom, rows, tile, [o], [mla_w_out[j].astype(BF16)],
                      name="mla_tail")
        elif kind == 1:
            lambda_init = 0.8 - 0.6 * math.exp(-0.3 * i)
            o = _diff_layer(geom, hx, mod, diff_w_in[j], diff_lambda_q1[j], diff_lambda_k1[j],
                            diff_lambda_q2[j], diff_lambda_k2[j], diff_subln[j], lambda_init)
            hx = tail(_attn_tail_kernel, geom, rows, tile, [o], [diff_w_out[j].astype(BF16)],
                      name="diff_tail")
        elif kind == 2:
            o_f, o_b, gate = _gla_layer(geom, hx, mod, gla_w_in[j], gla_gate_w_fwd[j],
                                        gla_gate_b_fwd[j], gla_gate_w_bwd[j], gla_gate_b_bwd[j])
            hx = tail(functools.partial(_gated_tail_kernel, head_dim=GLA_DV), geom, rows, tile,
                      [o_f, o_b, gate], [gla_norm[j].reshape(1, -1), gla_w_out[j].astype(BF16)],
                      name="gla_tail")
        else:
            o_f, o_b, gate = _gdn_layer(geom, hx, mod, gdn_w_in[j], gdn_conv_w[j], gdn_a_log_fwd[j],
                                        gdn_dt_bias_fwd[j], gdn_a_log_bwd[j], gdn_dt_bias_bwd[j])
            hx = tail(functools.partial(_gated_tail_kernel, head_dim=GDN_HEAD_DIM), geom, rows, tile,
                      [o_f, o_b, gate], [gdn_norm[j].reshape(1, -1), gdn_w_out[j].astype(BF16)],
                      name="gdn_tail")
    return hx[:, :seq]
```

```python
import functools
import math

import jax
import jax.numpy as jnp
from jax import lax
from jax.experimental import pallas as pl
from jax.experimental.pallas import tpu as pltpu

F32 = jnp.float32
BF16 = jnp.bfloat16

D_MODEL = 1024
DEPTH = 4
GRID_W = 64
D_FF = 4 * D_MODEL
ALPHA = (2 * DEPTH) ** 0.25
NORM_EPS = 1e-6
ROPE_BASE = 10000.0
CHUNK = 64

MLA_HEADS = 16
MLA_NOPE = 64
MLA_ROPE = 32
MLA_V = 64
MLA_KV_LORA = 256
MLA_Q_LORA = 768

DIFF_HEAD_DIM = 64
DIFF_HEADS = 8

GLA_HEADS = 4
GLA_KEY_DIM = 512
GLA_VALUE_DIM = 1024
GLA_DK = 128
GLA_DV = 256
GLA_GATE_RANK = 16
GLA_GATE_NORM = 16.0

GDN_HEAD_DIM = 128
GDN_KEY_HEADS = 8
GDN_VALUE_HEADS = 16
GDN_KEY_DIM = 1024
GDN_VALUE_DIM = 2048
GDN_CONV = 5
GDN_CONV_CH = 4096

LANE = 128
ROW_TILE = 256
HALO = 8
V_PAD = 16
KEY_BLOCK = 256
LOG2E = math.log2(math.e)
VMEM_LIMIT = 56 * 1024 * 1024


def _cparams(*sem):
    return pltpu.CompilerParams(dimension_semantics=sem, vmem_limit_bytes=VMEM_LIMIT)


def _const_spec(shape):
    nd = len(shape)
    return pl.BlockSpec(shape, lambda *_: (0,) * nd, pipeline_mode=pl.Buffered(1))


def _dot(a, b):
    return jnp.dot(a.astype(BF16), b.astype(BF16), preferred_element_type=F32)


def _dot_nt(a, b):
    return lax.dot_general(a.astype(BF16), b.astype(BF16), (((1,), (1,)), ((), ())),
                           preferred_element_type=F32)


def _dot_exact(a, b):
    return jnp.dot(a, b, preferred_element_type=F32, precision=lax.Precision.HIGHEST)


def _dot_nt_exact(a, b):
    return lax.dot_general(a, b, (((1,), (1,)), ((), ())), preferred_element_type=F32,
                           precision=lax.Precision.HIGHEST)


def _rms(x, g):
    return x * lax.rsqrt(jnp.mean(jnp.square(x), -1, keepdims=True) + NORM_EPS) * g


def _layer_norm(z, g, b):
    mu = jnp.mean(z, -1, keepdims=True)
    zc = z - mu
    var = jnp.mean(jnp.square(zc), -1, keepdims=True)
    return zc * lax.rsqrt(var + NORM_EPS) * g + b


def _softplus(x):
    return jnp.maximum(x, 0.0) + jnp.log1p(jnp.exp(-jnp.abs(x)))


def _silu(x):
    return x * jax.nn.sigmoid(x)


def _adaln_kernel(s_ref, w_ref, b_ref, o_ref):
    o_ref[...] = _dot(_silu(s_ref[...]), w_ref[...]) + b_ref[...]


def _adaln(s, ada_w, ada_b):
    depth, d, n = ada_w.shape
    rows = s.shape[0]
    tn = 1536
    return pl.pallas_call(
        _adaln_kernel,
        grid=(depth, n // tn),
        in_specs=[pl.BlockSpec((rows, d), lambda l, j: (0, 0)),
                  pl.BlockSpec((None, d, tn), lambda l, j: (l, 0, j)),
                  pl.BlockSpec((None, 1, tn), lambda l, j: (l, 0, j))],
        out_specs=pl.BlockSpec((None, rows, tn), lambda l, j: (l, 0, j)),
        out_shape=jax.ShapeDtypeStruct((depth, rows, n), F32),
        compiler_params=_cparams("parallel", "parallel"),
        name="adaln",
    )(s, ada_w, ada_b.reshape(depth, 1, n))


class _Geom:
    def __init__(self, batch, seq, ctx):
        assert seq % ROW_TILE == 0 and ctx % ROW_TILE == 0
        self.batch, self.seq, self.ctx = batch, seq, ctx
        self.nt = seq + ctx
        self.lat_tiles = seq // ROW_TILE
        self.tiles = self.nt // ROW_TILE

    def mod_spec(self, k):
        lat_tiles, batch = self.lat_tiles, self.batch
        return pl.BlockSpec((None, 1, D_MODEL),
                            lambda b, t: (jnp.where(t < lat_tiles, b, batch), 0, k))

    def row_spec(self, width, tile=ROW_TILE):
        return pl.BlockSpec((None, tile, width), lambda b, t: (b, t, 0))

    def mod_pair_specs(self):
        batch = self.batch
        return [pl.BlockSpec((None, 1, 6 * D_MODEL), lambda b, t: (b, 0, 0)),
                pl.BlockSpec((None, 1, 6 * D_MODEL), lambda b, t: (batch, 0, 0))]

    def big_tile(self, rows):
        return next(n * ROW_TILE for n in (3, 2, 1) if rows % (n * ROW_TILE) == 0)


def _slice_mods(lat_ref, ctx_ref, block_rows, s, seq):
    row0 = pl.program_id(1) * block_rows + s * ROW_TILE
    m = jnp.where(row0 >= seq, ctx_ref[...], lat_ref[...])
    return [m[:, k * D_MODEL:(k + 1) * D_MODEL] for k in range(6)]


def _modulate_block(h_ref, lat_ref, ctx_ref, seq):
    n = h_ref.shape[0]
    parts = []
    for s in range(n // ROW_TILE):
        m = _slice_mods(lat_ref, ctx_ref, n, s, seq)
        parts.append(h_ref[s * ROW_TILE:(s + 1) * ROW_TILE, :] * (1.0 + m[1]) + m[0])
    return jnp.concatenate(parts, 0)


def _rope_angles(n, dim):
    rows = n // GRID_W
    row = jnp.repeat(jnp.arange(rows, dtype=F32), GRID_W)
    col = jnp.tile(jnp.arange(GRID_W, dtype=F32), rows)
    n_freq = dim // 4
    inv_freq = ROPE_BASE ** (-jnp.arange(n_freq, dtype=F32) / n_freq)
    ang = jnp.concatenate([row[:, None] * inv_freq, col[:, None] * inv_freq], -1)
    return jnp.cos(ang), jnp.sin(ang)


def _rope_tables(geom, dim, layout):
    cos, sin = _rope_angles(geom.seq, dim)
    one, zero = jnp.ones_like(cos), jnp.zeros_like(cos)
    c = jnp.concatenate([one if g == "-" else cos for g in layout], -1)
    sa = jnp.concatenate([-sin if g == "e" else zero for g in layout], -1)
    sb = jnp.concatenate([sin if g == "o" else zero for g in layout], -1)
    pad = lambda t, v: jnp.concatenate([t, jnp.full((geom.ctx, LANE), v, F32)], 0)
    return pad(c, 1.0), pad(sa, 0.0), pad(sb, 0.0)


def _rope(x, c, sa, sb, w):
    return x * c + pltpu.roll(x, LANE - w, 1) * sa + pltpu.roll(x, w, 1) * sb


def _deinterleave(n):
    return list(range(0, n, 2)) + list(range(1, n, 2))


def _store_values_t(vt_ref, vt, heads, dv):
    sub = lax.broadcasted_iota(jnp.int32, (V_PAD, vt.shape[1]), 0)
    ones = (sub == 0).astype(vt_ref.dtype)
    for hd in range(heads):
        r0 = hd * (dv + V_PAD)
        vt_ref[r0:r0 + dv, :] = vt[hd * dv:(hd + 1) * dv].astype(vt_ref.dtype)
        vt_ref[r0 + dv:r0 + dv + V_PAD, :] = ones


def _mla_proj_kernel(h_ref, sh_ref, sc_ref, win_ref, qn_ref, kvn_ref, wqb_ref, wk_ref, wvt_ref,
                     c_ref, sa_ref, sb_ref, q_ref, k_ref, vt_ref):
    u = h_ref[...] * (1.0 + sc_ref[...]) + sh_ref[...]
    t = _dot(u, win_ref[...])
    cq, ckv, kr = t[:, :MLA_Q_LORA], t[:, MLA_Q_LORA:MLA_Q_LORA + MLA_KV_LORA], t[:, -LANE:]
    q = _dot(_rms(cq, qn_ref[...]), wqb_ref[...])
    kvn = _rms(ckv, kvn_ref[...]).astype(BF16)
    k_nope = jnp.dot(kvn, wk_ref[...], preferred_element_type=F32)
    c, sa, sb = c_ref[...], sa_ref[...], sb_ref[...]
    half = MLA_ROPE // 2
    scale = (MLA_NOPE + MLA_ROPE) ** -0.5 * LOG2E
    kr = _rope(pltpu.roll(kr, MLA_NOPE, 1), c, sa, sb, half)
    for hd in range(MLA_HEADS):
        sl = slice(hd * LANE, (hd + 1) * LANE)
        q_ref[:, sl] = (_rope(q[:, sl], c, sa, sb, half) * scale).astype(q_ref.dtype)
        k_ref[:, sl] = (k_nope[:, sl] + kr).astype(k_ref.dtype)
    _store_values_t(vt_ref, _dot_nt(wvt_ref[...], kvn), MLA_HEADS, MLA_V)


def _mla_proj_joining_kernel(x_ref, ctx_ref, *rest, lat_tiles):
    hx_ref = rest[-1]
    hx_ref[...] = jnp.where(pl.program_id(1) < lat_tiles, x_ref[...], ctx_ref[...])
    _mla_proj_kernel(hx_ref, *rest[:-1])


def _softmax_t(keys, qs, k_blocks, vt_blocks):
    hs = range(len(qs))
    m, acc = [None] * len(qs), [None] * len(qs)
    blocks = [slice(k0, k0 + KEY_BLOCK) for k0 in range(keys.start, keys.stop, KEY_BLOCK)]
    scores = lambda ks: [_dot_nt(k_blocks[i](ks), qs[i]) for i in hs]
    st_next = scores(blocks[0])
    for bi, ks in enumerate(blocks):
        k0 = ks.start
        st, st_next = st_next, (scores(blocks[bi + 1]) if bi + 1 < len(blocks) else None)
        top = [jnp.max(s, 0, keepdims=True) for s in st]
        if k0 == keys.start:
            m = top
            acc = [jnp.dot(vt_blocks[i](ks), jnp.exp2(st[i] - m[i]).astype(BF16),
                           preferred_element_type=F32) for i in hs]
        else:
            m_new = [jnp.maximum(m[i], top[i]) for i in hs]
            acc = [acc[i] * jnp.exp2(m[i] - m_new[i])
                   + jnp.dot(vt_blocks[i](ks), jnp.exp2(st[i] - m_new[i]).astype(BF16),
                             preferred_element_type=F32) for i in hs]
            m = m_new
    dv = acc[0].shape[0] - V_PAD
    return [a[:dv] / a[dv:dv + 1] for a in acc]


def _key_ranges(body, lat_tiles, seq, nt):
    t = pl.program_id(2)
    pl.when(t < lat_tiles)(lambda: body(slice(0, nt)))
    pl.when(t >= lat_tiles)(lambda: body(slice(seq, nt)))


def _mla_attn_kernel(q_ref, k_ref, vt_ref, o_ref, *, lat_tiles, seq):
    rows = MLA_V + V_PAD
    heads = q_ref.shape[1] // LANE
    lanes = [slice(j * LANE, (j + 1) * LANE) for j in range(heads)]

    def attend(keys):
        outs = _softmax_t(keys, [q_ref[:, sl] for sl in lanes],
                          [lambda ks, sl=sl: k_ref[ks, sl] for sl in lanes],
                          [lambda ks, j=j: vt_ref[j * rows:(j + 1) * rows, ks] for j in range(heads)])
        for j in range(0, heads, 2):
            o_ref[:, j // 2 * LANE:(j // 2 + 1) * LANE] = jnp.concatenate(
                outs[j:j + 2], 0).T.astype(o_ref.dtype)

    _key_ranges(attend, lat_tiles, seq, k_ref.shape[0])


def _attention(kernel, geom, q, k, vt, extra, head_groups, qkw, vrows, ow, name):
    extra_specs = [_const_spec(e.shape) for e in extra]
    return pl.pallas_call(
        functools.partial(kernel, lat_tiles=geom.lat_tiles, seq=geom.seq),
        grid=(geom.batch, head_groups, geom.tiles),
        in_specs=extra_specs + [
            pl.BlockSpec((None, ROW_TILE, qkw), lambda b, g, t: (b, t, g)),
            pl.BlockSpec((None, geom.nt, qkw), lambda b, g, t: (b, 0, g)),
            pl.BlockSpec((None, vrows, geom.nt), lambda b, g, t: (b, g, 0))],
        out_specs=pl.BlockSpec((None, ROW_TILE, ow), lambda b, g, t: (b, t, g)),
        out_shape=jax.ShapeDtypeStruct((geom.batch, geom.nt, head_groups * ow), BF16),
        compiler_params=_cparams("parallel", "parallel", "arbitrary"),
        name=name,
    )(*extra, q, k, vt)


def _mla_layer(geom, hx, mod, w_in, q_norm, kv_norm, w_qb, w_kvb):
    d = D_MODEL
    perm = jnp.array(_deinterleave(MLA_ROPE))
    w_in_p = jnp.concatenate(
        [w_in[:, :MLA_Q_LORA + MLA_KV_LORA], w_in[:, MLA_Q_LORA + MLA_KV_LORA:][:, perm],
         jnp.zeros((d, LANE - MLA_ROPE), F32)], -1).astype(BF16)
    wq = w_qb.reshape(MLA_Q_LORA, MLA_HEADS, MLA_NOPE + MLA_ROPE)
    wq = jnp.concatenate([wq[..., :MLA_NOPE], wq[..., MLA_NOPE:][..., perm],
                          jnp.zeros((MLA_Q_LORA, MLA_HEADS, LANE - MLA_NOPE - MLA_ROPE), F32)], -1)
    wq = wq.reshape(MLA_Q_LORA, MLA_HEADS * LANE).astype(BF16)
    wkv = w_kvb.reshape(MLA_KV_LORA, MLA_HEADS, MLA_NOPE + MLA_V)
    wk = jnp.concatenate([wkv[..., :MLA_NOPE],
                          jnp.zeros((MLA_KV_LORA, MLA_HEADS, LANE - MLA_NOPE), F32)], -1)
    wk = wk.reshape(MLA_KV_LORA, MLA_HEADS * LANE).astype(BF16)
    wvt = wkv[..., MLA_NOPE:].reshape(MLA_KV_LORA, MLA_HEADS * MLA_V).T.astype(BF16)
    tables = _rope_tables(geom, MLA_ROPE, ["-"] * 4 + ["e", "o"] + ["-"] * 2)
    tab_spec = pl.BlockSpec((ROW_TILE, LANE), lambda b, t: (t, 0))
    qk_w = MLA_HEADS * LANE
    vrows = MLA_HEADS * (MLA_V + V_PAD)
    col_spec = lambda rows: pl.BlockSpec((None, rows, ROW_TILE), lambda b, t: (b, 0, t))
    lat = geom.lat_tiles
    joining = isinstance(hx, tuple)
    body = functools.partial(_mla_proj_joining_kernel, lat_tiles=lat) if joining else _mla_proj_kernel
    h_specs = ([pl.BlockSpec((None, ROW_TILE, d), lambda b, t: (b, jnp.minimum(t, lat - 1), 0)),
                pl.BlockSpec((None, ROW_TILE, d), lambda b, t: (b, jnp.maximum(t - lat, 0), 0))]
               if joining else [geom.row_spec(d)])
    h_args = list(hx) if joining else [hx]
    outs = pl.pallas_call(
        body,
        grid=(geom.batch, geom.tiles),
        in_specs=h_specs + [geom.mod_spec(0), geom.mod_spec(1),
                            _const_spec(w_in_p.shape), _const_spec((1, MLA_Q_LORA)),
                            _const_spec((1, MLA_KV_LORA)), _const_spec(wq.shape),
                            _const_spec(wk.shape), _const_spec(wvt.shape),
                            tab_spec, tab_spec, tab_spec],
        out_specs=([geom.row_spec(qk_w), geom.row_spec(qk_w), col_spec(vrows)]
                   + ([geom.row_spec(d)] if joining else [])),
        out_shape=([jax.ShapeDtypeStruct((geom.batch, geom.nt, qk_w), BF16),
                    jax.ShapeDtypeStruct((geom.batch, geom.nt, qk_w), BF16),
                    jax.ShapeDtypeStruct((geom.batch, vrows, geom.nt), BF16)]
                   + ([jax.ShapeDtypeStruct((geom.batch, geom.nt, d), F32)] if joining else [])),
        compiler_params=_cparams("parallel", "parallel"),
        name="mla_proj",
    )(*h_args, mod, mod, w_in_p, q_norm.reshape(1, -1), kv_norm.reshape(1, -1), wq, wk, wvt,
      *tables)
    q, k, vt = outs[:3]
    hps = 8
    o = _attention(_mla_attn_kernel, geom, q, k, vt, [], MLA_HEADS // hps, hps * LANE,
                   hps * (MLA_V + V_PAD), hps * MLA_V, "mla_attn")
    return o, (outs[3] if joining else hx)


def _diff_proj_kernel(h_ref, lat_ref, ctx_ref, wqk_ref, wvt_ref, c_ref, sa_ref, sb_ref,
                      q_ref, k_ref, vt_ref, *, seq):
    u = _modulate_block(h_ref, lat_ref, ctx_ref, seq).astype(BF16)
    t = jnp.dot(u, wqk_ref[...], preferred_element_type=F32)
    c, sa, sb = c_ref[...], sa_ref[...], sb_ref[...]
    half = DIFF_HEAD_DIM // 2
    width = DIFF_HEADS * LANE
    scale = DIFF_HEAD_DIM ** -0.5 * LOG2E
    for hd in range(DIFF_HEADS):
        sl = slice(hd * LANE, (hd + 1) * LANE)
        q_ref[:, sl] = (_rope(t[:, sl], c, sa, sb, half) * scale).astype(q_ref.dtype)
        k_ref[:, sl] = _rope(t[:, width + hd * LANE:width + (hd + 1) * LANE],
                             c, sa, sb, half).astype(k_ref.dtype)
    _store_values_t(vt_ref, _dot_nt(wvt_ref[...], u), DIFF_HEADS, 2 * DIFF_HEAD_DIM)


def _diff_attn_kernel(lam_ref, subln_ref, q_ref, k_ref, vt_ref, o_ref, *, lambda_init,
                      lat_tiles, seq):
    lam_p = lam_ref[...]
    lam = (jnp.exp(jnp.sum(lam_p[0:1] * lam_p[1:2], -1, keepdims=True))
           - jnp.exp(jnp.sum(lam_p[2:3] * lam_p[3:4], -1, keepdims=True)) + lambda_init)
    heads = q_ref.shape[1] // LANE
    rows = 2 * DIFF_HEAD_DIM + V_PAD
    lanes = [slice(j * LANE, (j + 1) * LANE) for j in range(heads)]
    lane = lax.broadcasted_iota(jnp.int32, (q_ref.shape[0], LANE), 1)
    qs = []
    for sl in lanes:
        q = q_ref[:, sl]
        zero = jnp.zeros_like(q)
        qs += [jnp.where(lane < DIFF_HEAD_DIM, q, zero), jnp.where(lane < DIFF_HEAD_DIM, zero, q)]

    def attend(keys):
        outs = _softmax_t(
            keys, qs, [lambda ks, sl=sl: k_ref[ks, sl] for sl in lanes for _ in range(2)],
            [lambda ks, j=j: vt_ref[j * rows:(j + 1) * rows, ks] for j in range(heads)
             for _ in range(2)])
        for j, sl in enumerate(lanes):
            o = outs[2 * j] - lam * outs[2 * j + 1]
            o = o * lax.rsqrt(jnp.mean(jnp.square(o), 0, keepdims=True) + NORM_EPS) * subln_ref[...]
            o_ref[:, sl] = (o * (1.0 - lambda_init)).T.astype(o_ref.dtype)

    _key_ranges(attend, lat_tiles, seq, k_ref.shape[0])


def _diff_layer(geom, hx, mod, w_in, lam_q1, lam_k1, lam_q2, lam_k2, subln, lambda_init):
    d = D_MODEL
    hd = DIFF_HEAD_DIM
    half_perm = _deinterleave(hd)
    head_perm = half_perm + [hd + p for p in half_perm]
    width = DIFF_HEADS * LANE
    qk_perm = jnp.array([h * LANE + p for h in range(DIFF_HEADS) for p in head_perm])
    w_qk = jnp.concatenate([w_in[:, :width][:, qk_perm], w_in[:, width:2 * width][:, qk_perm]],
                           -1).astype(BF16)
    w_vt = w_in[:, 2 * width:].T.astype(BF16)
    tables = _rope_tables(geom, hd, ["e", "o", "e", "o"])
    tile = geom.big_tile(geom.nt)
    tab_spec = pl.BlockSpec((tile, LANE), lambda b, t: (t, 0))
    out = jax.ShapeDtypeStruct((geom.batch, geom.nt, width), BF16)
    vrows = DIFF_HEADS * (2 * hd + V_PAD)
    col_spec = lambda rows: pl.BlockSpec((None, rows, tile), lambda b, t: (b, 0, t))
    q, k, vt = pl.pallas_call(
        functools.partial(_diff_proj_kernel, seq=geom.seq),
        grid=(geom.batch, geom.nt // tile),
        in_specs=[geom.row_spec(d, tile), *geom.mod_pair_specs(), _const_spec(w_qk.shape),
                  _const_spec(w_vt.shape), tab_spec, tab_spec, tab_spec],
        out_specs=[geom.row_spec(width, tile), geom.row_spec(width, tile), col_spec(vrows)],
        out_shape=[out, out, jax.ShapeDtypeStruct((geom.batch, vrows, geom.nt), BF16)],
        compiler_params=_cparams("parallel", "parallel"),
        name="diff_proj",
    )(hx, mod, mod, w_qk, w_vt, *tables)
    lam_p = jnp.stack([lam_q1, lam_k1, lam_q2, lam_k2]).astype(F32)
    kern = functools.partial(_diff_attn_kernel, lambda_init=lambda_init)
    hps = 4
    return _attention(kern, geom, q, k, vt, [lam_p, subln.reshape(-1, 1)], DIFF_HEADS // hps,
                      hps * LANE, hps * (2 * hd + V_PAD), hps * LANE, "diff_attn")


def _chain_block(geom, rev):
    tiles, lat = geom.tiles, geom.lat_tiles
    if rev:
        return lambda j: tiles - 1 - j
    ctx_tiles = tiles - lat
    return lambda j: jnp.where(j < ctx_tiles, j + lat, j - ctx_tiles)


def _gla_proj_kernel(h_ref, lat_ref, ctx_ref, win_ref, q_ref, k_ref, v_ref, g_ref, r_ref, *, seq):
    t = _dot(_modulate_block(h_ref, lat_ref, ctx_ref, seq), win_ref[...])
    kd, vd = GLA_KEY_DIM, GLA_VALUE_DIM
    q_ref[...] = t[:, :kd] * GLA_DK ** -0.5
    k_ref[...] = t[:, kd:2 * kd]
    v_ref[...] = t[:, 2 * kd:2 * kd + vd].astype(v_ref.dtype)
    g_ref[...] = t[:, 2 * kd + vd:2 * kd + 2 * vd].astype(g_ref.dtype)
    r_ref[...] = t[:, 2 * kd + 2 * vd:]


def _gla_scan_kernel(q_ref, k_ref, v_ref, r_ref, gw_ref, gb_ref, o_ref, st_ref, *, rev):
    @pl.when(pl.program_id(1) == 0)
    def _():
        st_ref[...] = jnp.zeros_like(st_ref)

    n = ROW_TILE
    r = lax.broadcasted_iota(jnp.int32, (n, n), 0)
    c = lax.broadcasted_iota(jnp.int32, (n, n), 1)
    same = (r >> 6) == (c >> 6)
    incl = jnp.logical_and(same, (c >= r) if rev else (r >= c))
    la = _dot(r_ref[...], gw_ref[...]) + gb_ref[...]
    la = -_softplus(-la) / GLA_GATE_NORM
    cum = _dot_exact(incl.astype(F32), la)
    tot = _dot_exact(same.astype(F32), la)
    hs = range(GLA_HEADS)
    ksl = [slice(hd * GLA_DK, (hd + 1) * GLA_DK) for hd in hs]
    vsl = [slice(hd * GLA_DV, (hd + 1) * GLA_DV) for hd in hs]

    q_dec = [(q_ref[:, ksl[hd]] * jnp.exp(cum[:, ksl[hd]])).astype(BF16) for hd in hs]
    k_inv = [(k_ref[:, ksl[hd]] * jnp.exp(-cum[:, ksl[hd]])).astype(BF16) for hd in hs]
    k_end = [(k_ref[:, ksl[hd]] * jnp.exp(tot[:, ksl[hd]] - cum[:, ksl[hd]])).astype(BF16)
             for hd in hs]
    a = [jnp.where(incl, _dot_nt(q_dec[hd], k_inv[hd]), 0.0).astype(BF16) for hd in hs]
    o_intra = [jnp.dot(a[hd], v_ref[:, vsl[hd]], preferred_element_type=F32) for hd in hs]

    n_chunks = ROW_TILE // CHUNK
    st = [st_ref[hd] for hd in hs]
    for ci in (range(n_chunks - 1, -1, -1) if rev else range(n_chunks)):
        rows = slice(ci * CHUNK, (ci + 1) * CHUNK)
        for hd in hs:
            o_ref[rows, vsl[hd]] = (o_intra[hd][rows]
                                    + _dot_nt(q_dec[hd][rows], st[hd])).astype(o_ref.dtype)
        st = [st[hd] * jnp.exp(tot[ci * CHUNK:ci * CHUNK + 1, ksl[hd]])
              + _dot(v_ref[rows, vsl[hd]].astype(F32).T, k_end[hd][rows]) for hd in hs]
    for hd in hs:
        st_ref[hd] = st[hd]


def _gla_layer(geom, hx, mod, w_in, gate_w_fwd, gate_b_fwd, gate_w_bwd, gate_b_bwd):
    d = D_MODEL
    kd, vd, rk = GLA_KEY_DIM, GLA_VALUE_DIM, GLA_GATE_RANK
    main = 2 * kd + 2 * vd
    w_in_p = jnp.concatenate([w_in, jnp.zeros((d, LANE - 2 * rk), F32)], -1).astype(BF16)
    b, nt = geom.batch, geom.nt
    tile = geom.big_tile(nt)
    q, k, v, g, r = pl.pallas_call(
        functools.partial(_gla_proj_kernel, seq=geom.seq),
        grid=(b, nt // tile),
        in_specs=[geom.row_spec(d, tile), *geom.mod_pair_specs(), _const_spec(w_in_p.shape)],
        out_specs=[geom.row_spec(kd, tile), geom.row_spec(kd, tile), geom.row_spec(vd, tile),
                   geom.row_spec(vd, tile), geom.row_spec(LANE, tile)],
        out_shape=[jax.ShapeDtypeStruct((b, nt, kd), F32), jax.ShapeDtypeStruct((b, nt, kd), F32),
                   jax.ShapeDtypeStruct((b, nt, vd), BF16), jax.ShapeDtypeStruct((b, nt, vd), BF16),
                   jax.ShapeDtypeStruct((b, nt, LANE), F32)],
        compiler_params=_cparams("parallel", "parallel"),
        name="gla_proj",
    )(hx, mod, mod, w_in_p)
    assert w_in_p.shape[1] == main + LANE

    def scan(rev, gate_w, gate_b, lane_off):
        gw = jnp.zeros((LANE, kd), F32).at[lane_off:lane_off + rk].set(gate_w).astype(BF16)
        blk = _chain_block(geom, rev)
        row = lambda w: pl.BlockSpec((None, ROW_TILE, w), lambda bi, j: (bi, blk(j), 0))
        return pl.pallas_call(
            functools.partial(_gla_scan_kernel, rev=rev),
            grid=(b, geom.tiles),
            in_specs=[row(kd), row(kd), row(vd), row(LANE), _const_spec(gw.shape),
                      _const_spec((1, kd))],
            out_specs=row(vd),
            out_shape=jax.ShapeDtypeStruct((b, nt, vd), BF16),
            scratch_shapes=[pltpu.VMEM((GLA_HEADS, GLA_DV, GLA_DK), F32)],
            compiler_params=_cparams("parallel", "arbitrary"),
            name="gla_scan_bwd" if rev else "gla_scan_fwd",
        )(q, k, v, r, gw, gate_b.reshape(1, kd))

    return scan(False, gate_w_fwd, gate_b_fwd, 0), scan(True, gate_w_bwd, gate_b_bwd, rk), g


def _gdn_proj_kernel(hp_ref, h_ref, hn_ref, sh_ref, sc_ref, wc_ref, wz_ref, ws_ref, wst_ref,
                     cw_ref, q_ref, k_ref, v_ref, z_ref, s_ref, st_ref, *, seq, nt):
    t_idx = pl.program_id(1)
    sc, sh = 1.0 + sc_ref[...], sh_ref[...]
    u = h_ref[...] * sc + sh
    row0 = t_idx * ROW_TILE
    has_prev = jnp.logical_and(row0 != 0, row0 != seq)
    has_next = jnp.logical_and(row0 + ROW_TILE != seq, row0 + ROW_TILE != nt)
    u_prev = jnp.where(has_prev, hp_ref[...] * sc + sh, 0.0)
    u_next = jnp.where(has_next, hn_ref[...] * sc + sh, 0.0)
    u_ext = jnp.concatenate([u_prev, u, u_next], 0).astype(BF16)
    u = u.astype(BF16)
    ext = ROW_TILE + 2 * HALO
    blk = 512
    pad = GDN_CONV // 2
    for cb in range(GDN_CONV_CH // blk):
        cols = slice(cb * blk, (cb + 1) * blk)
        te = jnp.dot(u_ext, wc_ref[:, cols], preferred_element_type=F32)
        acc = None
        for j in range(GDN_CONV):
            shifted = te if j == pad else pltpu.roll(te, (pad - j) % ext, 0)
            term = shifted[HALO:HALO + ROW_TILE] * cw_ref[j:j + 1, cols]
            acc = term if acc is None else acc + term
        y = _silu(acc)
        if cols.start < 2 * GDN_KEY_DIM:
            dst, scale, off = ((q_ref, GDN_HEAD_DIM ** -0.5, 0) if cols.start < GDN_KEY_DIM
                               else (k_ref, 1.0, GDN_KEY_DIM))
            for hh in range(blk // LANE):
                yh = y[:, hh * LANE:(hh + 1) * LANE]
                yh = yh * lax.rsqrt(jnp.sum(jnp.square(yh), -1, keepdims=True) + NORM_EPS)
                c0 = cols.start - off + hh * LANE
                dst[:, c0:c0 + LANE] = yh * scale
        else:
            c0 = cols.start - 2 * GDN_KEY_DIM
            v_ref[:, c0:c0 + blk] = y
    z_ref[...] = jnp.dot(u, wz_ref[...], preferred_element_type=F32).astype(z_ref.dtype)
    s_ref[...] = jnp.dot(u, ws_ref[...], preferred_element_type=F32)
    st_ref[...] = lax.dot_general(wst_ref[...], u, (((1,), (1,)), ((), ())),
                                  preferred_element_type=F32)


def _tri_inverse(lmats, out):
    n = lmats[0].shape[0]
    r = lax.broadcasted_iota(jnp.int32, (n, n), 0)
    c = lax.broadcasted_iota(jnp.int32, (n, n), 1)
    same16 = (r >> 4) == (c >> 4)
    same32 = (r >> 5) == (c >> 5)
    mid32 = jnp.logical_and(same32, jnp.logical_not(same16))
    eye = (r == c).astype(F32)
    d1 = [jnp.where(same16, l, 0.0) for l in lmats]
    xs = [eye - a for a in d1]
    powers = [a.astype(BF16) for a in d1]
    for _ in range(3):
        powers = [_dot(a, a).astype(BF16) for a in powers]
        yield
        xs = [x + _dot(x, p) for x, p in zip(xs, powers)]
        yield
    for off in ([jnp.where(mid32, l, 0.0) for l in lmats],
                [jnp.where(same32, 0.0, l) for l in lmats]):
        xb = [x.astype(BF16) for x in xs]
        ys = [_dot(x, o) for x, o in zip(xb, off)]
        yield
        xs = [x - _dot(y, x16) for x, y, x16 in zip(xs, ys, xb)]
        yield
    out.extend(xs)


def _interleave(*gens):
    live = list(gens)
    while live:
        for g in list(live):
            try:
                next(g)
            except StopIteration:
                live.remove(g)


def _gdn_scan_kernel(q_ref, k_ref, v_ref, s_ref, st_ref, alog_ref, dtb_ref, alogc_ref, dtbc_ref,
                     o_ref, state_ref, *, rev, lane_off, heads):
    @pl.when(pl.program_id(2) == 0)
    def _():
        state_ref[...] = jnp.zeros_like(state_ref)

    grp = pl.program_id(1)
    hv = GDN_VALUE_HEADS
    n = ROW_TILE
    r = lax.broadcasted_iota(jnp.int32, (n, n), 0)
    c = lax.broadcasted_iota(jnp.int32, (n, n), 1)
    same = (r >> 6) == (c >> 6)
    incl = jnp.logical_and(same, (c >= r) if rev else (r >= c))
    incl_f = incl.astype(F32)
    same_f = same.astype(F32)
    off_diag = r != c

    small = s_ref[...]
    small_t = st_ref[...]
    beta_all = jax.nn.sigmoid(small)
    g_all = -jnp.exp(alog_ref[...]) * _softplus(small + dtb_ref[...])
    a_rows = small_t[2 * hv + lane_off:2 * hv + lane_off + hv]
    g_t = -jnp.exp(alogc_ref[...]) * _softplus(a_rows + dtbc_ref[...])
    gc_all = _dot_exact(incl_f, g_all)
    tot_all = _dot_exact(same_f, g_all)
    gct_all = _dot_nt_exact(g_t, incl_f)
    tott_all = _dot_nt_exact(g_t, same_f)
    lane = lax.broadcasted_iota(jnp.int32, (n, LANE), 1)
    sub = lax.broadcasted_iota(jnp.int32, (hv, n), 0)
    hs = range(heads)

    def column(x, idx):
        return jnp.sum(jnp.where(lane == idx, x, 0.0), -1, keepdims=True)

    def row(x, idx):
        return jnp.sum(jnp.where(sub == idx, x, 0.0), 0, keepdims=True)

    ksl = [slice((hh // 2) * LANE, (hh // 2 + 1) * LANE) for hh in hs]
    vsl = [slice(hh * LANE, (hh + 1) * LANE) for hh in hs]
    shared, pre, post = {}, {}, {}

    def prepare(hh):
        kh = hh // 2
        if kh not in shared:
            k_t = k_ref[:, ksl[hh]].T.astype(BF16)
            kq = jnp.concatenate([k_ref[:, ksl[hh]], q_ref[:, ksl[hh]]], 0).astype(BF16)
            shared[kh] = jnp.dot(kq, k_t, preferred_element_type=F32)
        kk, qk = shared[kh][:ROW_TILE], shared[kh][ROW_TILE:]
        head = grp * heads + hh
        beta = column(beta_all, head + lane_off)
        gc = column(gc_all, head + 2 * hv + lane_off)
        tot = column(tot_all, head + 2 * hv + lane_off)
        gamma = jnp.where(incl, jnp.exp(gc - row(gct_all, head)), 0.0)
        kb = k_ref[:, ksl[hh]] * beta
        e_gc = jnp.exp(gc)
        pre[hh] = dict(
            lmat=jnp.where(off_diag, kk * beta * gamma, 0.0),
            a_intra=(qk * gamma).astype(BF16),
            rhs=jnp.concatenate([v_ref[:, vsl[hh]] * beta, kb * e_gc], -1).astype(BF16),
            q_dec=(q_ref[:, ksl[hh]] * e_gc).astype(BF16),
            k_dec_t=(k_ref[:, ksl[hh]] * jnp.exp(tot - gc)).T.astype(BF16),
            d_last=jnp.exp(row(tott_all, head)))

    def prepare_all(group):
        for hh in group:
            prepare(hh)
            yield

    def invert(group):
        t_inv = []
        yield from _tri_inverse([pre[hh]["lmat"] for hh in group], t_inv)
        for hh, t in zip(group, t_inv):
            post[hh] = _dot(t, pre[hh]["rhs"])
        yield

    def recur(group):
        n_chunks = ROW_TILE // CHUNK
        st = {hh: state_ref[hh] for hh in group}
        zeros = jnp.zeros((CHUNK, LANE), BF16)
        for ci in (range(n_chunks - 1, -1, -1) if rev else range(n_chunks)):
            rows = slice(ci * CHUNK, (ci + 1) * CHUNK)
            pair = slice(ci // 2 * LANE, (ci // 2 + 1) * LANE)
            ws_qs = {hh: jnp.dot(jnp.concatenate([post[hh][rows, LANE:].astype(BF16),
                                                  pre[hh]["q_dec"][rows]], 0),
                                 st[hh].astype(BF16), preferred_element_type=F32) for hh in group}
            yield
            v_new = {hh: (post[hh][rows, :LANE] - ws_qs[hh][:CHUNK]).astype(BF16) for hh in group}
            v_pair = {hh: jnp.concatenate([v_new[hh], zeros] if ci % 2 == 0 else [zeros, v_new[hh]],
                                          0) for hh in group}
            av_kv = {hh: jnp.dot(jnp.concatenate([pre[hh]["a_intra"][rows, pair],
                                                  pre[hh]["k_dec_t"][:, pair]], 0),
                                 v_pair[hh], preferred_element_type=F32) for hh in group}
            yield
            for hh in group:
                o_ref[rows, vsl[hh]] = (ws_qs[hh][CHUNK:] + av_kv[hh][:CHUNK]).astype(o_ref.dtype)
                st[hh] = (st[hh] * pre[hh]["d_last"][:, ci * CHUNK:ci * CHUNK + 1]
                          + av_kv[hh][CHUNK:])
        for hh in group:
            state_ref[hh] = st[hh]

    _interleave(prepare_all(list(hs)))
    _interleave(invert(list(hs)))
    _interleave(recur(list(hs)))


def _gdn_layer(geom, hx, mod, w_in, conv_w, a_log_fwd, dt_bias_fwd, a_log_bwd, dt_bias_bwd):
    d = D_MODEL
    b, nt = geom.batch, geom.nt
    hv = GDN_VALUE_HEADS
    wc = w_in[:, :GDN_CONV_CH].astype(BF16)
    wz = w_in[:, GDN_CONV_CH:GDN_CONV_CH + GDN_VALUE_DIM].astype(BF16)
    ws_t = w_in[:, GDN_CONV_CH + GDN_VALUE_DIM:].T
    ws = jnp.concatenate([ws_t.T, jnp.zeros((d, LANE - 4 * hv), F32)], -1).astype(BF16)
    ws_t = ws_t.astype(BF16)
    halo_per_tile = ROW_TILE // HALO
    last_halo = nt // HALO - 1
    prev_spec = pl.BlockSpec((None, HALO, d),
                             lambda bi, t: (bi, jnp.maximum(t * halo_per_tile - 1, 0), 0))
    next_spec = pl.BlockSpec((None, HALO, d),
                             lambda bi, t: (bi, jnp.minimum((t + 1) * halo_per_tile, last_halo), 0))
    q, k, v, z, small, small_t = pl.pallas_call(
        functools.partial(_gdn_proj_kernel, seq=geom.seq, nt=nt),
        grid=(b, geom.tiles),
        in_specs=[prev_spec, geom.row_spec(d), next_spec, geom.mod_spec(0), geom.mod_spec(1),
                  _const_spec(wc.shape), _const_spec(wz.shape), _const_spec(ws.shape),
                  _const_spec(ws_t.shape), _const_spec(conv_w.shape)],
        out_specs=[geom.row_spec(GDN_KEY_DIM), geom.row_spec(GDN_KEY_DIM),
                   geom.row_spec(GDN_VALUE_DIM), geom.row_spec(GDN_VALUE_DIM), geom.row_spec(LANE),
                   pl.BlockSpec((None, 4 * hv, ROW_TILE), lambda bi, t: (bi, 0, t))],
        out_shape=[jax.ShapeDtypeStruct((b, nt, GDN_KEY_DIM), F32),
                   jax.ShapeDtypeStruct((b, nt, GDN_KEY_DIM), F32),
                   jax.ShapeDtypeStruct((b, nt, GDN_VALUE_DIM), F32),
                   jax.ShapeDtypeStruct((b, nt, GDN_VALUE_DIM), BF16),
                   jax.ShapeDtypeStruct((b, nt, LANE), F32),
                   jax.ShapeDtypeStruct((b, 4 * hv, nt), F32)],
        compiler_params=_cparams("parallel", "parallel"),
        name="gdn_proj",
    )(hx, hx, hx, mod, mod, wc, wz, ws, ws_t, conv_w)

    heads = 16
    groups = hv // heads

    def scan(rev, a_log, dt_bias, lane_off):
        blk = _chain_block(geom, rev)
        place = lambda p: jnp.zeros((1, LANE), F32).at[0, 2 * hv + lane_off:3 * hv + lane_off].set(p)
        return pl.pallas_call(
            functools.partial(_gdn_scan_kernel, rev=rev, lane_off=lane_off, heads=heads),
            grid=(b, groups, geom.tiles),
            in_specs=[
                pl.BlockSpec((None, ROW_TILE, heads // 2 * LANE), lambda bi, g, j: (bi, blk(j), g)),
                pl.BlockSpec((None, ROW_TILE, heads // 2 * LANE), lambda bi, g, j: (bi, blk(j), g)),
                pl.BlockSpec((None, ROW_TILE, heads * LANE), lambda bi, g, j: (bi, blk(j), g)),
                pl.BlockSpec((None, ROW_TILE, LANE), lambda bi, g, j: (bi, blk(j), 0)),
                pl.BlockSpec((None, 4 * hv, ROW_TILE), lambda bi, g, j: (bi, 0, blk(j))),
                _const_spec((1, LANE)), _const_spec((1, LANE)),
                _const_spec((hv, 1)), _const_spec((hv, 1))],
            out_specs=pl.BlockSpec((None, ROW_TILE, heads * LANE),
                                   lambda bi, g, j: (bi, blk(j), g)),
            out_shape=jax.ShapeDtypeStruct((b, nt, GDN_VALUE_DIM), BF16),
            scratch_shapes=[pltpu.VMEM((heads, GDN_HEAD_DIM, GDN_HEAD_DIM), F32)],
            compiler_params=_cparams("parallel", "parallel", "arbitrary"),
            name="gdn_scan_bwd" if rev else "gdn_scan_fwd",
        )(q, k, v, small, small_t, place(a_log), place(dt_bias),
          a_log.reshape(hv, 1), dt_bias.reshape(hv, 1))

    return (scan(False, a_log_fwd, dt_bias_fwd, 0), scan(True, a_log_bwd, dt_bias_bwd, hv), z)


def _residual_norm(h, gate, y, g, b):
    return _layer_norm(ALPHA * h + gate * y, g, b)


def _sublayers_tail(project, h_ref, lat_ref, ctx_ref, ln1g_ref, ln1b_ref, w1_ref, w2_ref,
                    ln2g_ref, ln2b_ref, out_ref, *, seq):
    n_sub = h_ref.shape[0] // ROW_TILE
    quarter = ROW_TILE // 4
    ff_blk = 1024
    y, h1, y2 = {}, {}, {}
    mods = functools.partial(_slice_mods, lat_ref, ctx_ref, h_ref.shape[0], seq=seq)

    def proj(s):
        y[s] = project(slice(s * ROW_TILE, (s + 1) * ROW_TILE))
        yield

    def norm1(s):
        gate = mods(s)[2]
        parts = []
        for qi in range(4):
            rows = slice(s * ROW_TILE + qi * quarter, s * ROW_TILE + (qi + 1) * quarter)
            parts.append(_residual_norm(h_ref[rows, :], gate, y[s][qi * quarter:(qi + 1) * quarter],
                                        ln1g_ref[...], ln1b_ref[...]))
            yield
        h1[s] = jnp.concatenate(parts, 0)

    def mlp(s):
        m = mods(s)
        u = (h1[s] * (1.0 + m[4]) + m[3]).astype(BF16)
        acc = None
        for c in range(0, D_FF, ff_blk):
            a = jnp.square(jnp.maximum(
                jnp.dot(u, w1_ref[:, c:c + ff_blk], preferred_element_type=F32), 0.0))
            part = jnp.dot(a.astype(BF16), w2_ref[c:c + ff_blk, :], preferred_element_type=F32)
            acc = part if acc is None else acc + part
            yield
        y2[s] = acc

    def norm2(s):
        gate = mods(s)[5]
        for qi in range(4):
            sl = slice(qi * quarter, (qi + 1) * quarter)
            rows = slice(s * ROW_TILE + qi * quarter, s * ROW_TILE + (qi + 1) * quarter)
            out_ref[rows, :] = _residual_norm(h1[s][sl], gate, y2[s][sl], ln2g_ref[...],
                                              ln2b_ref[...])
            yield

    stages = [proj, norm1, mlp, norm2]
    for step in range(n_sub + len(stages) - 1):
        _interleave(*[stages[step - s](s) for s in range(n_sub) if 0 <= step - s < len(stages)])


def _attn_tail_kernel(o_ref, w_ref, *rest, seq):
    _sublayers_tail(lambda rows: jnp.dot(o_ref[rows, :], w_ref[...], preferred_element_type=F32),
                    *rest, seq=seq)


def _gated_tail_kernel(of_ref, ob_ref, z_ref, ng_ref, w_ref, *rest, head_dim, seq):
    def project(rows):
        o = of_ref[rows, :].astype(F32) + ob_ref[rows, :].astype(F32)
        ng = ng_ref[...]
        parts = [_rms(o[:, c:c + head_dim], ng) for c in range(0, o.shape[1], head_dim)]
        x = jnp.concatenate(parts, -1) * _silu(z_ref[rows, :].astype(F32))
        return _dot(x, w_ref[...])

    _sublayers_tail(project, *rest, seq=seq)


def _tail_call(kernel, geom, rows, tile, acts, mixer_consts, hx, mod, ln1, w1, w2, ln2, name):
    d = D_MODEL
    consts = list(ln1) + [w1, w2] + list(ln2)
    batch = geom.batch
    in_specs = ([geom.row_spec(a.shape[-1], tile) for a in acts]
                + [_const_spec(c.shape) for c in mixer_consts]
                + [geom.row_spec(d, tile)] + geom.mod_pair_specs()
                + [_const_spec(c.shape) for c in consts])
    return pl.pallas_call(
        functools.partial(kernel, seq=geom.seq),
        grid=(batch, rows // tile),
        in_specs=in_specs,
        out_specs=geom.row_spec(d, tile),
        out_shape=jax.ShapeDtypeStruct((batch, rows, d), F32),
        compiler_params=_cparams("parallel", "parallel"),
        name=name,
    )(*acts, *mixer_consts, hx, mod, mod, *consts)


def kernel(x, c, ctx, c_ctx, ada_w, ada_b, ln1_g, ln1_b, ln2_g, ln2_b, mlp_w1, mlp_w2, mla_w_in, mla_q_norm, mla_kv_norm, mla_w_qb, mla_w_kvb, mla_w_out, diff_w_in, diff_lambda_q1, diff_lambda_k1, diff_lambda_q2, diff_lambda_k2, diff_subln, diff_w_out, gla_w_in, gla_gate_w_fwd, gla_gate_b_fwd, gla_gate_w_bwd, gla_gate_b_bwd, gla_norm, gla_w_out, gdn_w_in, gdn_conv_w, gdn_a_log_fwd, gdn_dt_bias_fwd, gdn_a_log_bwd, gdn_dt_bias_bwd, gdn_norm, gdn_w_out):
    batch, seq, d = x.shape
    geom = _Geom(batch, seq, ctx.shape[1])
    depth = ada_w.shape[0]
    cond_rows = -(-(batch + 1) // 8) * 8
    s = jnp.concatenate([c, c_ctx[None], jnp.zeros((cond_rows - batch - 1, d), F32)], 0)
    mods = _adaln(s, ada_w, ada_b)
    hx = (x, ctx)
    for i in range(depth):
        last = i == depth - 1
        rows = geom.seq if last else geom.nt
        tile = geom.big_tile(rows)
        kind, j = i % 4, i // 4
        mod = mods[i].reshape(cond_rows, 1, 6 * d)
        ln1 = [ln1_g[i].reshape(1, d), ln1_b[i].reshape(1, d)]
        ln2 = [ln2_g[i].reshape(1, d), ln2_b[i].reshape(1, d)]
        if kind == 0:
            o, hx = _mla_layer(geom, hx, mod, mla_w_in[j], mla_q_norm[j], mla_kv_norm[j],
                               mla_w_qb[j], mla_w_kvb[j])
        elif isinstance(hx, tuple):
            hx = jnp.concatenate(hx, 1)
        tail = functools.partial(_tail_call, hx=hx, mod=mod, ln1=ln1, w1=mlp_w1[i].astype(BF16),
                                 w2=mlp_w2[i].astype(BF16), ln2=ln2)
        if kind == 0:
            hx = tail(_attn_tail_kernel, geom, rows, tile, [o], [mla_w_out[j].astype(BF16)],
                      name="mla_tail")
        elif kind == 1:
            lambda_init = 0.8 - 0.6 * math.exp(-0.3 * i)
            o = _diff_layer(geom, hx, mod, diff_w_in[j], diff_lambda_q1[j], diff_lambda_k1[j],
                            diff_lambda_q2[j], diff_lambda_k2[j], diff_subln[j], lambda_init)
            hx = tail(_attn_tail_kernel, geom, rows, tile, [o], [diff_w_out[j].astype(BF16)],
                      name="diff_tail")
        elif kind == 2:
            o_f, o_b, gate = _gla_layer(geom, hx, mod, gla_w_in[j], gla_gate_w_fwd[j],
                                        gla_gate_b_fwd[j], gla_gate_w_bwd[j], gla_gate_b_bwd[j])
            hx = tail(functools.partial(_gated_tail_kernel, head_dim=GLA_DV), geom, rows, tile,
                      [o_f, o_b, gate], [gla_norm[j].reshape(1, -1), gla_w_out[j].astype(BF16)],
                      name="gla_tail")
        else:
            o_f, o_b, gate = _gdn_layer(geom, hx, mod, gdn_w_in[j], gdn_conv_w[j], gdn_a_log_fwd[j],
                                        gdn_dt_bias_fwd[j], gdn_a_log_bwd[j], gdn_dt_bias_bwd[j])
            hx = tail(functools.partial(_gated_tail_kernel, head_dim=GDN_HEAD_DIM), geom, rows, tile,
                      [o_f, o_b, gate], [gdn_norm[j].reshape(1, -1), gdn_w_out[j].astype(BF16)],
                      name="gdn_tail")
    return hx[:, :seq]
```

```python
import functools
import math

import jax
import jax.numpy as jnp
from jax import lax
from jax.experimental import pallas as pl
from jax.experimental.pallas import tpu as pltpu

F32 = jnp.float32
BF16 = jnp.bfloat16

D_MODEL = 1024
DEPTH = 4
GRID_W = 64
D_FF = 4 * D_MODEL
ALPHA = (2 * DEPTH) ** 0.25
NORM_EPS = 1e-6
ROPE_BASE = 10000.0
CHUNK = 64

MLA_HEADS = 16
MLA_NOPE = 64
MLA_ROPE = 32
MLA_V = 64
MLA_KV_LORA = 256
MLA_Q_LORA = 768

DIFF_HEAD_DIM = 64
DIFF_HEADS = 8

GLA_HEADS = 4
GLA_KEY_DIM = 512
GLA_VALUE_DIM = 1024
GLA_DK = 128
GLA_DV = 256
GLA_GATE_RANK = 16
GLA_GATE_NORM = 16.0

GDN_HEAD_DIM = 128
GDN_KEY_HEADS = 8
GDN_VALUE_HEADS = 16
GDN_KEY_DIM = 1024
GDN_VALUE_DIM = 2048
GDN_CONV = 5
GDN_CONV_CH = 4096

LANE = 128
ROW_TILE = 256
HALO = 8
V_PAD = 16
KEY_BLOCK = 256
LOG2E = math.log2(math.e)
VMEM_LIMIT = 56 * 1024 * 1024


def _cparams(*sem):
    return pltpu.CompilerParams(dimension_semantics=sem, vmem_limit_bytes=VMEM_LIMIT)


def _const_spec(shape):
    nd = len(shape)
    return pl.BlockSpec(shape, lambda *_: (0,) * nd, pipeline_mode=pl.Buffered(1))


def _dot(a, b):
    return jnp.dot(a.astype(BF16), b.astype(BF16), preferred_element_type=F32)


def _dot_nt(a, b):
    return lax.dot_general(a.astype(BF16), b.astype(BF16), (((1,), (1,)), ((), ())),
                           preferred_element_type=F32)


def _dot_exact(a, b):
    return jnp.dot(a, b, preferred_element_type=F32, precision=lax.Precision.HIGHEST)


def _dot_nt_exact(a, b):
    return lax.dot_general(a, b, (((1,), (1,)), ((), ())), preferred_element_type=F32,
                           precision=lax.Precision.HIGHEST)


def _rms(x, g):
    return x * lax.rsqrt(jnp.mean(jnp.square(x), -1, keepdims=True) + NORM_EPS) * g


def _layer_norm(z, g, b):
    mu = jnp.mean(z, -1, keepdims=True)
    zc = z - mu
    var = jnp.mean(jnp.square(zc), -1, keepdims=True)
    return zc * lax.rsqrt(var + NORM_EPS) * g + b


def _softplus(x):
    return jnp.maximum(x, 0.0) + jnp.log1p(jnp.exp(-jnp.abs(x)))


def _silu(x):
    return x * jax.nn.sigmoid(x)


def _adaln_kernel(s_ref, w_ref, b_ref, o_ref):
    o_ref[...] = _dot(_silu(s_ref[...]), w_ref[...]) + b_ref[...]


def _adaln(s, ada_w, ada_b):
    depth, d, n = ada_w.shape
    rows = s.shape[0]
    tn = 1536
    return pl.pallas_call(
        _adaln_kernel,
        grid=(depth, n // tn),
        in_specs=[pl.BlockSpec((rows, d), lambda l, j: (0, 0)),
                  pl.BlockSpec((None, d, tn), lambda l, j: (l, 0, j)),
                  pl.BlockSpec((None, 1, tn), lambda l, j: (l, 0, j))],
        out_specs=pl.BlockSpec((None, rows, tn), lambda l, j: (l, 0, j)),
        out_shape=jax.ShapeDtypeStruct((depth, rows, n), F32),
        compiler_params=_cparams("parallel", "parallel"),
        name="adaln",
    )(s, ada_w, ada_b.reshape(depth, 1, n))


class _Geom:
    def __init__(self, batch, seq, ctx):
        assert seq % ROW_TILE == 0 and ctx % ROW_TILE == 0
        self.batch, self.seq, self.ctx = batch, seq, ctx
        self.nt = seq + ctx
        self.lat_tiles = seq // ROW_TILE
        self.tiles = self.nt // ROW_TILE

    def mod_spec(self, k):
        lat_tiles, batch = self.lat_tiles, self.batch
        return pl.BlockSpec((None, 1, D_MODEL),
                            lambda b, t: (jnp.where(t < lat_tiles, b, batch), 0, k))

    def row_spec(self, width, tile=ROW_TILE):
        return pl.BlockSpec((None, tile, width), lambda b, t: (b, t, 0))

    def mod_pair_specs(self):
        batch = self.batch
        return [pl.BlockSpec((None, 1, 6 * D_MODEL), lambda b, t: (b, 0, 0)),
                pl.BlockSpec((None, 1, 6 * D_MODEL), lambda b, t: (batch, 0, 0))]

    def big_tile(self, rows):
        return next(n * ROW_TILE for n in (3, 2, 1) if rows % (n * ROW_TILE) == 0)


def _slice_mods(lat_ref, ctx_ref, block_rows, s, seq):
    row0 = pl.program_id(1) * block_rows + s * ROW_TILE
    m = jnp.where(row0 >= seq, ctx_ref[...], lat_ref[...])
    return [m[:, k * D_MODEL:(k + 1) * D_MODEL] for k in range(6)]


def _modulate_block(h_ref, lat_ref, ctx_ref, seq):
    n = h_ref.shape[0]
    parts = []
    for s in range(n // ROW_TILE):
        m = _slice_mods(lat_ref, ctx_ref, n, s, seq)
        parts.append(h_ref[s * ROW_TILE:(s + 1) * ROW_TILE, :] * (1.0 + m[1]) + m[0])
    return jnp.concatenate(parts, 0)


def _rope_angles(n, dim):
    rows = n // GRID_W
    row = jnp.repeat(jnp.arange(rows, dtype=F32), GRID_W)
    col = jnp.tile(jnp.arange(GRID_W, dtype=F32), rows)
    n_freq = dim // 4
    inv_freq = ROPE_BASE ** (-jnp.arange(n_freq, dtype=F32) / n_freq)
    ang = jnp.concatenate([row[:, None] * inv_freq, col[:, None] * inv_freq], -1)
    return jnp.cos(ang), jnp.sin(ang)


def _rope_tables(geom, dim, layout):
    cos, sin = _rope_angles(geom.seq, dim)
    one, zero = jnp.ones_like(cos), jnp.zeros_like(cos)
    c = jnp.concatenate([one if g == "-" else cos for g in layout], -1)
    sa = jnp.concatenate([-sin if g == "e" else zero for g in layout], -1)
    sb = jnp.concatenate([sin if g == "o" else zero for g in layout], -1)
    pad = lambda t, v: jnp.concatenate([t, jnp.full((geom.ctx, LANE), v, F32)], 0)
    return pad(c, 1.0), pad(sa, 0.0), pad(sb, 0.0)


def _rope(x, c, sa, sb, w):
    return x * c + pltpu.roll(x, LANE - w, 1) * sa + pltpu.roll(x, w, 1) * sb


def _deinterleave(n):
    return list(range(0, n, 2)) + list(range(1, n, 2))


def _store_values_t(vt_ref, vt, heads, dv):
    sub = lax.broadcasted_iota(jnp.int32, (V_PAD, vt.shape[1]), 0)
    ones = (sub == 0).astype(vt_ref.dtype)
    for hd in range(heads):
        r0 = hd * (dv + V_PAD)
        vt_ref[r0:r0 + dv, :] = vt[hd * dv:(hd + 1) * dv].astype(vt_ref.dtype)
        vt_ref[r0 + dv:r0 + dv + V_PAD, :] = ones


def _mla_proj_kernel(h_ref, sh_ref, sc_ref, win_ref, qn_ref, kvn_ref, wqb_ref, wk_ref, wvt_ref,
                     c_ref, sa_ref, sb_ref, q_ref, k_ref, vt_ref):
    u = h_ref[...] * (1.0 + sc_ref[...]) + sh_ref[...]
    t = _dot(u, win_ref[...])
    cq, ckv, kr = t[:, :MLA_Q_LORA], t[:, MLA_Q_LORA:MLA_Q_LORA + MLA_KV_LORA], t[:, -LANE:]
    q = _dot(_rms(cq, qn_ref[...]), wqb_ref[...])
    kvn = _rms(ckv, kvn_ref[...]).astype(BF16)
    k_nope = jnp.dot(kvn, wk_ref[...], preferred_element_type=F32)
    c, sa, sb = c_ref[...], sa_ref[...], sb_ref[...]
    half = MLA_ROPE // 2
    scale = (MLA_NOPE + MLA_ROPE) ** -0.5 * LOG2E
    kr = _rope(pltpu.roll(kr, MLA_NOPE, 1), c, sa, sb, half)
    for hd in range(MLA_HEADS):
        sl = slice(hd * LANE, (hd + 1) * LANE)
        q_ref[:, sl] = (_rope(q[:, sl], c, sa, sb, half) * scale).astype(q_ref.dtype)
        k_ref[:, sl] = (k_nope[:, sl] + kr).astype(k_ref.dtype)
    _store_values_t(vt_ref, _dot_nt(wvt_ref[...], kvn), MLA_HEADS, MLA_V)


def _mla_proj_joining_kernel(x_ref, ctx_ref, *rest, lat_tiles):
    hx_ref = rest[-1]
    hx_ref[...] = jnp.where(pl.program_id(1) < lat_tiles, x_ref[...], ctx_ref[...])
    _mla_proj_kernel(hx_ref, *rest[:-1])


def _softmax_t(keys, qs, k_blocks, vt_blocks):
    hs = range(len(qs))
    m, acc = [None] * len(qs), [None] * len(qs)
    blocks = [slice(k0, k0 + KEY_BLOCK) for k0 in range(keys.start, keys.stop, KEY_BLOCK)]
    scores = lambda ks: [_dot_nt(k_blocks[i](ks), qs[i]) for i in hs]
    st_next = scores(blocks[0])
    for bi, ks in enumerate(blocks):
        k0 = ks.start
        st, st_next = st_next, (scores(blocks[bi + 1]) if bi + 1 < len(blocks) else None)
        top = [jnp.max(s, 0, keepdims=True) for s in st]
        if k0 == keys.start:
            m = top
            acc = [jnp.dot(vt_blocks[i](ks), jnp.exp2(st[i] - m[i]).astype(BF16),
                           preferred_element_type=F32) for i in hs]
        else:
            m_new = [jnp.maximum(m[i], top[i]) for i in hs]
            acc = [acc[i] * jnp.exp2(m[i] - m_new[i])
                   + jnp.dot(vt_blocks[i](ks), jnp.exp2(st[i] - m_new[i]).astype(BF16),
                             preferred_element_type=F32) for i in hs]
            m = m_new
    dv = acc[0].shape[0] - V_PAD
    return [a[:dv] / a[dv:dv + 1] for a in acc]


def _key_ranges(body, lat_tiles, seq, nt):
    t = pl.program_id(2)
    pl.when(t < lat_tiles)(lambda: body(slice(0, nt)))
    pl.when(t >= lat_tiles)(lambda: body(slice(seq, nt)))


def _mla_attn_kernel(q_ref, k_ref, vt_ref, o_ref, *, lat_tiles, seq):
    rows = MLA_V + V_PAD
    heads = q_ref.shape[1] // LANE
    lanes = [slice(j * LANE, (j + 1) * LANE) for j in range(heads)]

    def attend(keys):
        outs = _softmax_t(keys, [q_ref[:, sl] for sl in lanes],
                          [lambda ks, sl=sl: k_ref[ks, sl] for sl in lanes],
                          [lambda ks, j=j: vt_ref[j * rows:(j + 1) * rows, ks] for j in range(heads)])
        for j in range(0, heads, 2):
            o_ref[:, j // 2 * LANE:(j // 2 + 1) * LANE] = jnp.concatenate(
                outs[j:j + 2], 0).T.astype(o_ref.dtype)

    _key_ranges(attend, lat_tiles, seq, k_ref.shape[0])


def _attention(kernel, geom, q, k, vt, extra, head_groups, qkw, vrows, ow, name):
    extra_specs = [_const_spec(e.shape) for e in extra]
    return pl.pallas_call(
        functools.partial(kernel, lat_tiles=geom.lat_tiles, seq=geom.seq),
        grid=(geom.batch, head_groups, geom.tiles),
        in_specs=extra_specs + [
            pl.BlockSpec((None, ROW_TILE, qkw), lambda b, g, t: (b, t, g)),
            pl.BlockSpec((None, geom.nt, qkw), lambda b, g, t: (b, 0, g)),
            pl.BlockSpec((None, vrows, geom.nt), lambda b, g, t: (b, g, 0))],
        out_specs=pl.BlockSpec((None, ROW_TILE, ow), lambda b, g, t: (b, t, g)),
        out_shape=jax.ShapeDtypeStruct((geom.batch, geom.nt, head_groups * ow), BF16),
        compiler_params=_cparams("parallel", "parallel", "arbitrary"),
        name=name,
    )(*extra, q, k, vt)


def _mla_layer(geom, hx, mod, w_in, q_norm, kv_norm, w_qb, w_kvb):
    d = D_MODEL
    perm = jnp.array(_deinterleave(MLA_ROPE))
    w_in_p = jnp.concatenate(
        [w_in[:, :MLA_Q_LORA + MLA_KV_LORA], w_in[:, MLA_Q_LORA + MLA_KV_LORA:][:, perm],
         jnp.zeros((d, LANE - MLA_ROPE), F32)], -1).astype(BF16)
    wq = w_qb.reshape(MLA_Q_LORA, MLA_HEADS, MLA_NOPE + MLA_ROPE)
    wq = jnp.concatenate([wq[..., :MLA_NOPE], wq[..., MLA_NOPE:][..., perm],
                          jnp.zeros((MLA_Q_LORA, MLA_HEADS, LANE - MLA_NOPE - MLA_ROPE), F32)], -1)
    wq = wq.reshape(MLA_Q_LORA, MLA_HEADS * LANE).astype(BF16)
    wkv = w_kvb.reshape(MLA_KV_LORA, MLA_HEADS, MLA_NOPE + MLA_V)
    wk = jnp.concatenate([wkv[..., :MLA_NOPE],
                          jnp.zeros((MLA_KV_LORA, MLA_HEADS, LANE - MLA_NOPE), F32)], -1)
    wk = wk.reshape(MLA_KV_LORA, MLA_HEADS * LANE).astype(BF16)
    wvt = wkv[..., MLA_NOPE:].reshape(MLA_KV_LORA, MLA_HEADS * MLA_V).T.astype(BF16)
    tables = _rope_tables(geom, MLA_ROPE, ["-"] * 4 + ["e", "o"] + ["-"] * 2)
    tab_spec = pl.BlockSpec((ROW_TILE, LANE), lambda b, t: (t, 0))
    qk_w = MLA_HEADS * LANE
    vrows = MLA_HEADS * (MLA_V + V_PAD)
    col_spec = lambda rows: pl.BlockSpec((None, rows, ROW_TILE), lambda b, t: (b, 0, t))
    lat = geom.lat_tiles
    joining = isinstance(hx, tuple)
    body = functools.partial(_mla_proj_joining_kernel, lat_tiles=lat) if joining else _mla_proj_kernel
    h_specs = ([pl.BlockSpec((None, ROW_TILE, d), lambda b, t: (b, jnp.minimum(t, lat - 1), 0)),
                pl.BlockSpec((None, ROW_TILE, d), lambda b, t: (b, jnp.maximum(t - lat, 0), 0))]
               if joining else [geom.row_spec(d)])
    h_args = list(hx) if joining else [hx]
    outs = pl.pallas_call(
        body,
        grid=(geom.batch, geom.tiles),
        in_specs=h_specs + [geom.mod_spec(0), geom.mod_spec(1),
                            _const_spec(w_in_p.shape), _const_spec((1, MLA_Q_LORA)),
                            _const_spec((1, MLA_KV_LORA)), _const_spec(wq.shape),
                            _const_spec(wk.shape), _const_spec(wvt.shape),
                            tab_spec, tab_spec, tab_spec],
        out_specs=([geom.row_spec(qk_w), geom.row_spec(qk_w), col_spec(vrows)]
                   + ([geom.row_spec(d)] if joining else [])),
        out_shape=([jax.ShapeDtypeStruct((geom.batch, geom.nt, qk_w), BF16),
                    jax.ShapeDtypeStruct((geom.batch, geom.nt, qk_w), BF16),
                    jax.ShapeDtypeStruct((geom.batch, vrows, geom.nt), BF16)]
                   + ([jax.ShapeDtypeStruct((geom.batch, geom.nt, d), F32)] if joining else [])),
        compiler_params=_cparams("parallel", "parallel"),
        name="mla_proj",
    )(*h_args, mod, mod, w_in_p, q_norm.reshape(1, -1), kv_norm.reshape(1, -1), wq, wk, wvt,
      *tables)
    q, k, vt = outs[:3]
    hps = 8
    o = _attention(_mla_attn_kernel, geom, q, k, vt, [], MLA_HEADS // hps, hps * LANE,
                   hps * (MLA_V + V_PAD), hps * MLA_V, "mla_attn")
    return o, (outs[3] if joining else hx)


def _diff_proj_kernel(h_ref, lat_ref, ctx_ref, wqk_ref, wvt_ref, c_ref, sa_ref, sb_ref,
                      q_ref, k_ref, vt_ref, *, seq):
    u = _modulate_block(h_ref, lat_ref, ctx_ref, seq).astype(BF16)
    t = jnp.dot(u, wqk_ref[...], preferred_element_type=F32)
    c, sa, sb = c_ref[...], sa_ref[...], sb_ref[...]
    half = DIFF_HEAD_DIM // 2
    width = DIFF_HEADS * LANE
    scale = DIFF_HEAD_DIM ** -0.5 * LOG2E
    for hd in range(DIFF_HEADS):
        sl = slice(hd * LANE, (hd + 1) * LANE)
        q_ref[:, sl] = (_rope(t[:, sl], c, sa, sb, half) * scale).astype(q_ref.dtype)
        k_ref[:, sl] = _rope(t[:, width + hd * LANE:width + (hd + 1) * LANE],
                             c, sa, sb, half).astype(k_ref.dtype)
    _store_values_t(vt_ref, _dot_nt(wvt_ref[...], u), DIFF_HEADS, 2 * DIFF_HEAD_DIM)


def _diff_attn_kernel(lam_ref, subln_ref, q_ref, k_ref, vt_ref, o_ref, *, lambda_init,
                      lat_tiles, seq):
    lam_p = lam_ref[...]
    lam = (jnp.exp(jnp.sum(lam_p[0:1] * lam_p[1:2], -1, keepdims=True))
           - jnp.exp(jnp.sum(lam_p[2:3] * lam_p[3:4], -1, keepdims=True)) + lambda_init)
    heads = q_ref.shape[1] // LANE
    rows = 2 * DIFF_HEAD_DIM + V_PAD
    lanes = [slice(j * LANE, (j + 1) * LANE) for j in range(heads)]
    lane = lax.broadcasted_iota(jnp.int32, (q_ref.shape[0], LANE), 1)
    qs = []
    for sl in lanes:
        q = q_ref[:, sl]
        zero = jnp.zeros_like(q)
        qs += [jnp.where(lane < DIFF_HEAD_DIM, q, zero), jnp.where(lane < DIFF_HEAD_DIM, zero, q)]

    def attend(keys):
        outs = _softmax_t(
            keys, qs, [lambda ks, sl=sl: k_ref[ks, sl] for sl in lanes for _ in range(2)],
            [lambda ks, j=j: vt_ref[j * rows:(j + 1) * rows, ks] for j in range(heads)
             for _ in range(2)])
        for j, sl in enumerate(lanes):
            o = outs[2 * j] - lam * outs[2 * j + 1]
            o = o * lax.rsqrt(jnp.mean(jnp.square(o), 0, keepdims=True) + NORM_EPS) * subln_ref[...]
            o_ref[:, sl] = (o * (1.0 - lambda_init)).T.astype(o_ref.dtype)

    _key_ranges(attend, lat_tiles, seq, k_ref.shape[0])


def _diff_layer(geom, hx, mod, w_in, lam_q1, lam_k1, lam_q2, lam_k2, subln, lambda_init):
    d = D_MODEL
    hd = DIFF_HEAD_DIM
    half_perm = _deinterleave(hd)
    head_perm = half_perm + [hd + p for p in half_perm]
    width = DIFF_HEADS * LANE
    qk_perm = jnp.array([h * LANE + p for h in range(DIFF_HEADS) for p in head_perm])
    w_qk = jnp.concatenate([w_in[:, :width][:, qk_perm], w_in[:, width:2 * width][:, qk_perm]],
                           -1).astype(BF16)
    w_vt = w_in[:, 2 * width:].T.astype(BF16)
    tables = _rope_tables(geom, hd, ["e", "o", "e", "o"])
    tile = geom.big_tile(geom.nt)
    tab_spec = pl.BlockSpec((tile, LANE), lambda b, t: (t, 0))
    out = jax.ShapeDtypeStruct((geom.batch, geom.nt, width), BF16)
    vrows = DIFF_HEADS * (2 * hd + V_PAD)
    col_spec = lambda rows: pl.BlockSpec((None, rows, tile), lambda b, t: (b, 0, t))
    q, k, vt = pl.pallas_call(
        functools.partial(_diff_proj_kernel, seq=geom.seq),
        grid=(geom.batch, geom.nt // tile),
        in_specs=[geom.row_spec(d, tile), *geom.mod_pair_specs(), _const_spec(w_qk.shape),
                  _const_spec(w_vt.shape), tab_spec, tab_spec, tab_spec],
        out_specs=[geom.row_spec(width, tile), geom.row_spec(width, tile), col_spec(vrows)],
        out_shape=[out, out, jax.ShapeDtypeStruct((geom.batch, vrows, geom.nt), BF16)],
        compiler_params=_cparams("parallel", "parallel"),
        name="diff_proj",
    )(hx, mod, mod, w_qk, w_vt, *tables)
    lam_p = jnp.stack([lam_q1, lam_k1, lam_q2, lam_k2]).astype(F32)
    kern = functools.partial(_diff_attn_kernel, lambda_init=lambda_init)
    hps = 4
    return _attention(kern, geom, q, k, vt, [lam_p, subln.reshape(-1, 1)], DIFF_HEADS // hps,
                      hps * LANE, hps * (2 * hd + V_PAD), hps * LANE, "diff_attn")


def _chain_block(geom, rev):
    tiles, lat = geom.tiles, geom.lat_tiles
    if rev:
        return lambda j: tiles - 1 - j
    ctx_tiles = tiles - lat
    return lambda j: jnp.where(j < ctx_tiles, j + lat, j - ctx_tiles)


def _gla_proj_kernel(h_ref, lat_ref, ctx_ref, win_ref, q_ref, k_ref, v_ref, g_ref, r_ref, *, seq):
    t = _dot(_modulate_block(h_ref, lat_ref, ctx_ref, seq), win_ref[...])
    kd, vd = GLA_KEY_DIM, GLA_VALUE_DIM
    q_ref[...] = t[:, :kd] * GLA_DK ** -0.5
    k_ref[...] = t[:, kd:2 * kd]
    v_ref[...] = t[:, 2 * kd:2 * kd + vd].astype(v_ref.dtype)
    g_ref[...] = t[:, 2 * kd + vd:2 * kd + 2 * vd].astype(g_ref.dtype)
    r_ref[...] = t[:, 2 * kd + 2 * vd:]


def _gla_scan_kernel(q_ref, k_ref, v_ref, r_ref, gw_ref, gb_ref, o_ref, st_ref, *, rev):
    @pl.when(pl.program_id(1) == 0)
    def _():
        st_ref[...] = jnp.zeros_like(st_ref)

    n = ROW_TILE
    r = lax.broadcasted_iota(jnp.int32, (n, n), 0)
    c = lax.broadcasted_iota(jnp.int32, (n, n), 1)
    same = (r >> 6) == (c >> 6)
    incl = jnp.logical_and(same, (c >= r) if rev else (r >= c))
    la = _dot(r_ref[...], gw_ref[...]) + gb_ref[...]
    la = -_softplus(-la) / GLA_GATE_NORM
    cum = _dot_exact(incl.astype(F32), la)
    tot = _dot_exact(same.astype(F32), la)
    hs = range(GLA_HEADS)
    ksl = [slice(hd * GLA_DK, (hd + 1) * GLA_DK) for hd in hs]
    vsl = [slice(hd * GLA_DV, (hd + 1) * GLA_DV) for hd in hs]

    q_dec = [(q_ref[:, ksl[hd]] * jnp.exp(cum[:, ksl[hd]])).astype(BF16) for hd in hs]
    k_inv = [(k_ref[:, ksl[hd]] * jnp.exp(-cum[:, ksl[hd]])).astype(BF16) for hd in hs]
    k_end = [(k_ref[:, ksl[hd]] * jnp.exp(tot[:, ksl[hd]] - cum[:, ksl[hd]])).astype(BF16)
             for hd in hs]
    a = [jnp.where(incl, _dot_nt(q_dec[hd], k_inv[hd]), 0.0).astype(BF16) for hd in hs]
    o_intra = [jnp.dot(a[hd], v_ref[:, vsl[hd]], preferred_element_type=F32) for hd in hs]

    n_chunks = ROW_TILE // CHUNK
    st = [st_ref[hd] for hd in hs]
    for ci in (range(n_chunks - 1, -1, -1) if rev else range(n_chunks)):
        rows = slice(ci * CHUNK, (ci + 1) * CHUNK)
        for hd in hs:
            o_ref[rows, vsl[hd]] = (o_intra[hd][rows]
                                    + _dot_nt(q_dec[hd][rows], st[hd])).astype(o_ref.dtype)
        st = [st[hd] * jnp.exp(tot[ci * CHUNK:ci * CHUNK + 1, ksl[hd]])
              + _dot(v_ref[rows, vsl[hd]].astype(F32).T, k_end[hd][rows]) for hd in hs]
    for hd in hs:
        st_ref[hd] = st[hd]


def _gla_layer(geom, hx, mod, w_in, gate_w_fwd, gate_b_fwd, gate_w_bwd, gate_b_bwd):
    d = D_MODEL
    kd, vd, rk = GLA_KEY_DIM, GLA_VALUE_DIM, GLA_GATE_RANK
    main = 2 * kd + 2 * vd
    w_in_p = jnp.concatenate([w_in, jnp.zeros((d, LANE - 2 * rk), F32)], -1).astype(BF16)
    b, nt = geom.batch, geom.nt
    tile = geom.big_tile(nt)
    q, k, v, g, r = pl.pallas_call(
        functools.partial(_gla_proj_kernel, seq=geom.seq),
        grid=(b, nt // tile),
        in_specs=[geom.row_spec(d, tile), *geom.mod_pair_specs(), _const_spec(w_in_p.shape)],
        out_specs=[geom.row_spec(kd, tile), geom.row_spec(kd, tile), geom.row_spec(vd, tile),
                   geom.row_spec(vd, tile), geom.row_spec(LANE, tile)],
        out_shape=[jax.ShapeDtypeStruct((b, nt, kd), F32), jax.ShapeDtypeStruct((b, nt, kd), F32),
                   jax.ShapeDtypeStruct((b, nt, vd), BF16), jax.ShapeDtypeStruct((b, nt, vd), BF16),
                   jax.ShapeDtypeStruct((b, nt, LANE), F32)],
        compiler_params=_cparams("parallel", "parallel"),
        name="gla_proj",
    )(hx, mod, mod, w_in_p)
    assert w_in_p.shape[1] == main + LANE

    def side(rev, gate_w, gate_b, lane_off):
        gw = jnp.zeros((LANE, kd), F32).at[lane_off:lane_off + rk].set(gate_w).astype(BF16)
        blk = _chain_block(geom, rev)
        row = lambda w: pl.BlockSpec((None, ROW_TILE, w), lambda bi, j: (bi, blk(j), 0))
        specs = [row(kd), row(kd), row(vd), row(LANE), _const_spec(gw.shape), _const_spec((1, kd))]
        return specs, [q, k, v, r, gw, gate_b.reshape(1, kd)], row(vd)

    f_specs, f_args, f_out = side(False, gate_w_fwd, gate_b_fwd, 0)
    b_specs, b_args, b_out = side(True, gate_w_bwd, gate_b_bwd, rk)
    n_in = len(f_specs)

    def both(*refs):
        ins, (of_ref, ob_ref, sf_ref, sb_ref) = refs[:2 * n_in], refs[2 * n_in:]
        _gla_scan_kernel(*ins[:n_in], of_ref, sf_ref, rev=False)
        _gla_scan_kernel(*ins[n_in:], ob_ref, sb_ref, rev=True)

    state = pltpu.VMEM((GLA_HEADS, GLA_DV, GLA_DK), F32)
    o_f, o_b = pl.pallas_call(
        both,
        grid=(b, geom.tiles),
        in_specs=f_specs + b_specs,
        out_specs=[f_out, b_out],
        out_shape=[jax.ShapeDtypeStruct((b, nt, vd), BF16)] * 2,
        scratch_shapes=[state, state],
        compiler_params=_cparams("parallel", "arbitrary"),
        name="gla_scan",
    )(*f_args, *b_args)
    return o_f, o_b, g


def _gdn_proj_kernel(hp_ref, h_ref, hn_ref, sh_ref, sc_ref, wc_ref, wz_ref, ws_ref, wst_ref,
                     cw_ref, q_ref, k_ref, v_ref, z_ref, s_ref, st_ref, *, seq, nt):
    t_idx = pl.program_id(1)
    sc, sh = 1.0 + sc_ref[...], sh_ref[...]
    u = h_ref[...] * sc + sh
    row0 = t_idx * ROW_TILE
    has_prev = jnp.logical_and(row0 != 0, row0 != seq)
    has_next = jnp.logical_and(row0 + ROW_TILE != seq, row0 + ROW_TILE != nt)
    u_prev = jnp.where(has_prev, hp_ref[...] * sc + sh, 0.0)
    u_next = jnp.where(has_next, hn_ref[...] * sc + sh, 0.0)
    u_ext = jnp.concatenate([u_prev, u, u_next], 0).astype(BF16)
    u = u.astype(BF16)
    ext = ROW_TILE + 2 * HALO
    blk = 512
    pad = GDN_CONV // 2
    for cb in range(GDN_CONV_CH // blk):
        cols = slice(cb * blk, (cb + 1) * blk)
        te = jnp.dot(u_ext, wc_ref[:, cols], preferred_element_type=F32)
        acc = None
        for j in range(GDN_CONV):
            shifted = te if j == pad else pltpu.roll(te, (pad - j) % ext, 0)
            term = shifted[HALO:HALO + ROW_TILE] * cw_ref[j:j + 1, cols]
            acc = term if acc is None else acc + term
        y = _silu(acc)
        if cols.start < 2 * GDN_KEY_DIM:
            dst, scale, off = ((q_ref, GDN_HEAD_DIM ** -0.5, 0) if cols.start < GDN_KEY_DIM
                               else (k_ref, 1.0, GDN_KEY_DIM))
            for hh in range(blk // LANE):
                yh = y[:, hh * LANE:(hh + 1) * LANE]
                yh = yh * lax.rsqrt(jnp.sum(jnp.square(yh), -1, keepdims=True) + NORM_EPS)
                c0 = cols.start - off + hh * LANE
                dst[:, c0:c0 + LANE] = yh * scale
        else:
            c0 = cols.start - 2 * GDN_KEY_DIM
            v_ref[:, c0:c0 + blk] = y
    z_ref[...] = jnp.dot(u, wz_ref[...], preferred_element_type=F32).astype(z_ref.dtype)
    s_ref[...] = jnp.dot(u, ws_ref[...], preferred_element_type=F32)
    st_ref[...] = lax.dot_general(wst_ref[...], u, (((1,), (1,)), ((), ())),
                                  preferred_element_type=F32)


def _tri_inverse(lmats, out):
    n = lmats[0].shape[0]
    r = lax.broadcasted_iota(jnp.int32, (n, n), 0)
    c = lax.broadcasted_iota(jnp.int32, (n, n), 1)
    same16 = (r >> 4) == (c >> 4)
    same32 = (r >> 5) == (c >> 5)
    mid32 = jnp.logical_and(same32, jnp.logical_not(same16))
    eye = (r == c).astype(F32)
    d1 = [jnp.where(same16, l, 0.0) for l in lmats]
    xs = [eye - a for a in d1]
    powers = [a.astype(BF16) for a in d1]
    for _ in range(3):
        powers = [_dot(a, a).astype(BF16) for a in powers]
        yield
        xs = [x + _dot(x, p) for x, p in zip(xs, powers)]
        yield
    for off in ([jnp.where(mid32, l, 0.0) for l in lmats],
                [jnp.where(same32, 0.0, l) for l in lmats]):
        xb = [x.astype(BF16) for x in xs]
        ys = [_dot(x, o) for x, o in zip(xb, off)]
        yield
        xs = [x - _dot(y, x16) for x, y, x16 in zip(xs, ys, xb)]
        yield
    out.extend(xs)


def _interleave(*gens):
    live = list(gens)
    while live:
        for g in list(live):
            try:
                next(g)
            except StopIteration:
                live.remove(g)


def _gdn_scan_kernel(q_ref, k_ref, v_ref, s_ref, st_ref, alog_ref, dtb_ref, alogc_ref, dtbc_ref,
                     o_ref, state_ref, *, rev, lane_off, heads):
    @pl.when(pl.program_id(2) == 0)
    def _():
        state_ref[...] = jnp.zeros_like(state_ref)

    grp = pl.program_id(1)
    hv = GDN_VALUE_HEADS
    n = ROW_TILE
    r = lax.broadcasted_iota(jnp.int32, (n, n), 0)
    c = lax.broadcasted_iota(jnp.int32, (n, n), 1)
    same = (r >> 6) == (c >> 6)
    incl = jnp.logical_and(same, (c >= r) if rev else (r >= c))
    incl_f = incl.astype(F32)
    same_f = same.astype(F32)
    off_diag = r != c

    small = s_ref[...]
    small_t = st_ref[...]
    beta_all = jax.nn.sigmoid(small)
    g_all = -jnp.exp(alog_ref[...]) * _softplus(small + dtb_ref[...])
    a_rows = small_t[2 * hv + lane_off:2 * hv + lane_off + hv]
    g_t = -jnp.exp(alogc_ref[...]) * _softplus(a_rows + dtbc_ref[...])
    gc_all = _dot_exact(incl_f, g_all)
    tot_all = _dot_exact(same_f, g_all)
    gct_all = _dot_nt_exact(g_t, incl_f)
    tott_all = _dot_nt_exact(g_t, same_f)
    lane = lax.broadcasted_iota(jnp.int32, (n, LANE), 1)
    sub = lax.broadcasted_iota(jnp.int32, (hv, n), 0)
    hs = range(heads)

    def column(x, idx):
        return jnp.sum(jnp.where(lane == idx, x, 0.0), -1, keepdims=True)

    def row(x, idx):
        return jnp.sum(jnp.where(sub == idx, x, 0.0), 0, keepdims=True)

    ksl = [slice((hh // 2) * LANE, (hh // 2 + 1) * LANE) for hh in hs]
    vsl = [slice(hh * LANE, (hh + 1) * LANE) for hh in hs]
    shared, pre, post = {}, {}, {}

    def prepare(hh):
        kh = hh // 2
        if kh not in shared:
            k_t = k_ref[:, ksl[hh]].T.astype(BF16)
            kq = jnp.concatenate([k_ref[:, ksl[hh]], q_ref[:, ksl[hh]]], 0).astype(BF16)
            shared[kh] = jnp.dot(kq, k_t, preferred_element_type=F32)
        kk, qk = shared[kh][:ROW_TILE], shared[kh][ROW_TILE:]
        head = grp * heads + hh
        beta = column(beta_all, head + lane_off)
        gc = column(gc_all, head + 2 * hv + lane_off)
        tot = column(tot_all, head + 2 * hv + lane_off)
        gamma = jnp.where(incl, jnp.exp(gc - row(gct_all, head)), 0.0)
        kb = k_ref[:, ksl[hh]] * beta
        e_gc = jnp.exp(gc)
        pre[hh] = dict(
            lmat=jnp.where(off_diag, kk * beta * gamma, 0.0),
            a_intra=(qk * gamma).astype(BF16),
            rhs=jnp.concatenate([v_ref[:, vsl[hh]] * beta, kb * e_gc], -1).astype(BF16),
            q_dec=(q_ref[:, ksl[hh]] * e_gc).astype(BF16),
            k_dec_t=(k_ref[:, ksl[hh]] * jnp.exp(tot - gc)).T.astype(BF16),
            d_last=jnp.exp(row(tott_all, head)))

    def prepare_all(group):
        for hh in group:
            prepare(hh)
            yield

    def invert(group):
        t_inv = []
        yield from _tri_inverse([pre[hh]["lmat"] for hh in group], t_inv)
        for hh, t in zip(group, t_inv):
            post[hh] = _dot(t, pre[hh]["rhs"])
        yield

    def recur(group):
        n_chunks = ROW_TILE // CHUNK
        st = {hh: state_ref[hh] for hh in group}
        zeros = jnp.zeros((CHUNK, LANE), BF16)
        for ci in (range(n_chunks - 1, -1, -1) if rev else range(n_chunks)):
            rows = slice(ci * CHUNK, (ci + 1) * CHUNK)
            pair = slice(ci // 2 * LANE, (ci // 2 + 1) * LANE)
            ws_qs = {hh: jnp.dot(jnp.concatenate([post[hh][rows, LANE:].astype(BF16),
                                                  pre[hh]["q_dec"][rows]], 0),
                                 st[hh].astype(BF16), preferred_element_type=F32) for hh in group}
            yield
            v_new = {hh: (post[hh][rows, :LANE] - ws_qs[hh][:CHUNK]).astype(BF16) for hh in group}
            v_pair = {hh: jnp.concatenate([v_new[hh], zeros] if ci % 2 == 0 else [zeros, v_new[hh]],
                                          0) for hh in group}
            av_kv = {hh: jnp.dot(jnp.concatenate([pre[hh]["a_intra"][rows, pair],
                                                  pre[hh]["k_dec_t"][:, pair]], 0),
                                 v_pair[hh], preferred_element_type=F32) for hh in group}
            yield
            for hh in group:
                o_ref[rows, vsl[hh]] = (ws_qs[hh][CHUNK:] + av_kv[hh][:CHUNK]).astype(o_ref.dtype)
                st[hh] = (st[hh] * pre[hh]["d_last"][:, ci * CHUNK:ci * CHUNK + 1]
                          + av_kv[hh][CHUNK:])
        for hh in group:
            state_ref[hh] = st[hh]

    _interleave(prepare_all(list(hs)))
    _interleave(invert(list(hs)))
    _interleave(recur(list(hs)))


def _gdn_layer(geom, hx, mod, w_in, conv_w, a_log_fwd, dt_bias_fwd, a_log_bwd, dt_bias_bwd):
    d = D_MODEL
    b, nt = geom.batch, geom.nt
    hv = GDN_VALUE_HEADS
    wc = w_in[:, :GDN_CONV_CH].astype(BF16)
    wz = w_in[:, GDN_CONV_CH:GDN_CONV_CH + GDN_VALUE_DIM].astype(BF16)
    ws_t = w_in[:, GDN_CONV_CH + GDN_VALUE_DIM:].T
    ws = jnp.concatenate([ws_t.T, jnp.zeros((d, LANE - 4 * hv), F32)], -1).astype(BF16)
    ws_t = ws_t.astype(BF16)
    halo_per_tile = ROW_TILE // HALO
    last_halo = nt // HALO - 1
    prev_spec = pl.BlockSpec((None, HALO, d),
                             lambda bi, t: (bi, jnp.maximum(t * halo_per_tile - 1, 0), 0))
    next_spec = pl.BlockSpec((None, HALO, d),
                             lambda bi, t: (bi, jnp.minimum((t + 1) * halo_per_tile, last_halo), 0))
    q, k, v, z, small, small_t = pl.pallas_call(
        functools.partial(_gdn_proj_kernel, seq=geom.seq, nt=nt),
        grid=(b, geom.tiles),
        in_specs=[prev_spec, geom.row_spec(d), next_spec, geom.mod_spec(0), geom.mod_spec(1),
                  _const_spec(wc.shape), _const_spec(wz.shape), _const_spec(ws.shape),
                  _const_spec(ws_t.shape), _const_spec(conv_w.shape)],
        out_specs=[geom.row_spec(GDN_KEY_DIM), geom.row_spec(GDN_KEY_DIM),
                   geom.row_spec(GDN_VALUE_DIM), geom.row_spec(GDN_VALUE_DIM), geom.row_spec(LANE),
                   pl.BlockSpec((None, 4 * hv, ROW_TILE), lambda bi, t: (bi, 0, t))],
        out_shape=[jax.ShapeDtypeStruct((b, nt, GDN_KEY_DIM), F32),
                   jax.ShapeDtypeStruct((b, nt, GDN_KEY_DIM), F32),
                   jax.ShapeDtypeStruct((b, nt, GDN_VALUE_DIM), F32),
                   jax.ShapeDtypeStruct((b, nt, GDN_VALUE_DIM), BF16),
                   jax.ShapeDtypeStruct((b, nt, LANE), F32),
                   jax.ShapeDtypeStruct((b, 4 * hv, nt), F32)],
        compiler_params=_cparams("parallel", "parallel"),
        name="gdn_proj",
    )(hx, hx, hx, mod, mod, wc, wz, ws, ws_t, conv_w)

    heads = 16
    groups = hv // heads

    def scan(rev, a_log, dt_bias, lane_off):
        blk = _chain_block(geom, rev)
        place = lambda p: jnp.zeros((1, LANE), F32).at[0, 2 * hv + lane_off:3 * hv + lane_off].set(p)
        return pl.pallas_call(
            functools.partial(_gdn_scan_kernel, rev=rev, lane_off=lane_off, heads=heads),
            grid=(b, groups, geom.tiles),
            in_specs=[
                pl.BlockSpec((None, ROW_TILE, heads // 2 * LANE), lambda bi, g, j: (bi, blk(j), g)),
                pl.BlockSpec((None, ROW_TILE, heads // 2 * LANE), lambda bi, g, j: (bi, blk(j), g)),
                pl.BlockSpec((None, ROW_TILE, heads * LANE), lambda bi, g, j: (bi, blk(j), g)),
                pl.BlockSpec((None, ROW_TILE, LANE), lambda bi, g, j: (bi, blk(j), 0)),
                pl.BlockSpec((None, 4 * hv, ROW_TILE), lambda bi, g, j: (bi, 0, blk(j))),
                _const_spec((1, LANE)), _const_spec((1, LANE)),
                _const_spec((hv, 1)), _const_spec((hv, 1))],
            out_specs=pl.BlockSpec((None, ROW_TILE, heads * LANE),
                                   lambda bi, g, j: (bi, blk(j), g)),
            out_shape=jax.ShapeDtypeStruct((b, nt, GDN_VALUE_DIM), BF16),
            scratch_shapes=[pltpu.VMEM((heads, GDN_HEAD_DIM, GDN_HEAD_DIM), F32)],
            compiler_params=_cparams("parallel", "parallel", "arbitrary"),
            name="gdn_scan_bwd" if rev else "gdn_scan_fwd",
        )(q, k, v, small, small_t, place(a_log), place(dt_bias),
          a_log.reshape(hv, 1), dt_bias.reshape(hv, 1))

    return (scan(False, a_log_fwd, dt_bias_fwd, 0), scan(True, a_log_bwd, dt_bias_bwd, hv), z)


def _residual_norm(h, gate, y, g, b):
    return _layer_norm(ALPHA * h + gate * y, g, b)


def _sublayers_tail(project, h_ref, lat_ref, ctx_ref, ln1g_ref, ln1b_ref, w1_ref, w2_ref,
                    ln2g_ref, ln2b_ref, out_ref, *, seq):
    n_sub = h_ref.shape[0] // ROW_TILE
    quarter = ROW_TILE // 4
    ff_blk = 1024
    y, h1, y2 = {}, {}, {}
    mods = functools.partial(_slice_mods, lat_ref, ctx_ref, h_ref.shape[0], seq=seq)

    def proj(s):
        y[s] = project(slice(s * ROW_TILE, (s + 1) * ROW_TILE))
        yield

    def norm1(s):
        gate = mods(s)[2]
        parts = []
        for qi in range(4):
            rows = slice(s * ROW_TILE + qi * quarter, s * ROW_TILE + (qi + 1) * quarter)
            parts.append(_residual_norm(h_ref[rows, :], gate, y[s][qi * quarter:(qi + 1) * quarter],
                                        ln1g_ref[...], ln1b_ref[...]))
            yield
        h1[s] = jnp.concatenate(parts, 0)

    def mlp(s):
        m = mods(s)
        u = (h1[s] * (1.0 + m[4]) + m[3]).astype(BF16)
        acc = None
        for c in range(0, D_FF, ff_blk):
            a = jnp.square(jnp.maximum(
                jnp.dot(u, w1_ref[:, c:c + ff_blk], preferred_element_type=F32), 0.0))
            part = jnp.dot(a.astype(BF16), w2_ref[c:c + ff_blk, :], preferred_element_type=F32)
            acc = part if acc is None else acc + part
            yield
        y2[s] = acc

    def norm2(s):
        gate = mods(s)[5]
        for qi in range(4):
            sl = slice(qi * quarter, (qi + 1) * quarter)
            rows = slice(s * ROW_TILE + qi * quarter, s * ROW_TILE + (qi + 1) * quarter)
            out_ref[rows, :] = _residual_norm(h1[s][sl], gate, y2[s][sl], ln2g_ref[...],
                                              ln2b_ref[...])
            yield

    stages = [proj, norm1, mlp, norm2]
    for step in range(n_sub + len(stages) - 1):
        _interleave(*[stages[step - s](s) for s in range(n_sub) if 0 <= step - s < len(stages)])


def _attn_tail_kernel(o_ref, w_ref, *rest, seq):
    _sublayers_tail(lambda rows: jnp.dot(o_ref[rows, :], w_ref[...], preferred_element_type=F32),
                    *rest, seq=seq)


def _gated_tail_kernel(of_ref, ob_ref, z_ref, ng_ref, w_ref, *rest, head_dim, seq):
    def project(rows):
        o = of_ref[rows, :].astype(F32) + ob_ref[rows, :].astype(F32)
        ng = ng_ref[...]
        parts = [_rms(o[:, c:c + head_dim], ng) for c in range(0, o.shape[1], head_dim)]
        x = jnp.concatenate(parts, -1) * _silu(z_ref[rows, :].astype(F32))
        return _dot(x, w_ref[...])

    _sublayers_tail(project, *rest, seq=seq)


def _tail_call(kernel, geom, rows, tile, acts, mixer_consts, hx, mod, ln1, w1, w2, ln2, name):
    d = D_MODEL
    consts = list(ln1) + [w1, w2] + list(ln2)
    batch = geom.batch
    in_specs = ([geom.row_spec(a.shape[-1], tile) for a in acts]
                + [_const_spec(c.shape) for c in mixer_consts]
                + [geom.row_spec(d, tile)] + geom.mod_pair_specs()
                + [_const_spec(c.shape) for c in consts])
    return pl.pallas_call(
        functools.partial(kernel, seq=geom.seq),
        grid=(batch, rows // tile),
        in_specs=in_specs,
        out_specs=geom.row_spec(d, tile),
        out_shape=jax.ShapeDtypeStruct((batch, rows, d), F32),
        compiler_params=_cparams("parallel", "parallel"),
        name=name,
    )(*acts, *mixer_consts, hx, mod, mod, *consts)


def kernel(x, c, ctx, c_ctx, ada_w, ada_b, ln1_g, ln1_b, ln2_g, ln2_b, mlp_w1, mlp_w2, mla_w_in, mla_q_norm, mla_kv_norm, mla_w_qb, mla_w_kvb, mla_w_out, diff_w_in, diff_lambda_q1, diff_lambda_k1, diff_lambda_q2, diff_lambda_k2, diff_subln, diff_w_out, gla_w_in, gla_gate_w_fwd, gla_gate_b_fwd, gla_gate_w_bwd, gla_gate_b_bwd, gla_norm, gla_w_out, gdn_w_in, gdn_conv_w, gdn_a_log_fwd, gdn_dt_bias_fwd, gdn_a_log_bwd, gdn_dt_bias_bwd, gdn_norm, gdn_w_out):
    batch, seq, d = x.shape
    geom = _Geom(batch, seq, ctx.shape[1])
    depth = ada_w.shape[0]
    cond_rows = -(-(batch + 1) // 8) * 8
    s = jnp.concatenate([c, c_ctx[None], jnp.zeros((cond_rows - batch - 1, d), F32)], 0)
    mods = _adaln(s, ada_w, ada_b)
    hx = (x, ctx)
    for i in range(depth):
        last = i == depth - 1
        rows = geom.seq if last else geom.nt
        tile = geom.big_tile(rows)
        kind, j = i % 4, i // 4
        mod = mods[i].reshape(cond_rows, 1, 6 * d)
        ln1 = [ln1_g[i].reshape(1, d), ln1_b[i].reshape(1, d)]
        ln2 = [ln2_g[i].reshape(1, d), ln2_b[i].reshape(1, d)]
        if kind == 0:
            o, hx = _mla_layer(geom, hx, mod, mla_w_in[j], mla_q_norm[j], mla_kv_norm[j],
                               mla_w_qb[j], mla_w_kvb[j])
        elif isinstance(hx, tuple):
            hx = jnp.concatenate(hx, 1)
        tail = functools.partial(_tail_call, hx=hx, mod=mod, ln1=ln1, w1=mlp_w1[i].astype(BF16),
                                 w2=mlp_w2[i].astype(BF16), ln2=ln2)
        if kind == 0:
            hx = tail(_attn_tail_kernel, geom, rows, tile, [o], [mla_w_out[j].astype(BF16)],
                      name="mla_tail")
        elif kind == 1:
            lambda_init = 0.8 - 0.6 * math.exp(-0.3 * i)
            o = _diff_layer(geom, hx, mod, diff_w_in[j], diff_lambda_q1[j], diff_lambda_k1[j],
                            diff_lambda_q2[j], diff_lambda_k2[j], diff_subln[j], lambda_init)
            hx = tail(_attn_tail_kernel, geom, rows, tile, [o], [diff_w_out[j].astype(BF16)],
                      name="diff_tail")
        elif kind == 2:
            o_f, o_b, gate = _gla_layer(geom, hx, mod, gla_w_in[j], gla_gate_w_fwd[j],
                                        gla_gate_b_fwd[j], gla_gate_w_bwd[j], gla_gate_b_bwd[j])
            hx = tail(functools.partial(_gated_tail_kernel, head_dim=GLA_DV), geom, rows, tile,
                      [o_f, o_b, gate], [gla_norm[j].reshape(1, -1), gla_w_out[j].astype(BF16)],
                      name="gla_tail")
        else:
            o_f, o_b, gate = _gdn_layer(geom, hx, mod, gdn_w_in[j], gdn_conv_w[j], gdn_a_log_fwd[j],
                                        gdn_dt_bias_fwd[j], gdn_a_log_bwd[j], gdn_dt_bias_bwd[j])
            hx = tail(functools.partial(_gated_tail_kernel, head_dim=GDN_HEAD_DIM), geom, rows, tile,
                      [o_f, o_b, gate], [gdn_norm[j].reshape(1, -1), gdn_w_out[j].astype(BF16)],
                      name="gdn_tail")
    return hx[:, :seq]
```
